```python
import math
import jax, jax.numpy as jnp
from jax import lax
import numpy as np

D_MODEL = 1024
BATCH = 16
SEQ = 2048
DEPTH = 1

RW_WIDTH = 512
RW_HEAD = 64
RW_HEADS = RW_WIDTH // RW_HEAD
RW_DECAY_RANK = 64
RW_AAA_RANK = 64
RW_GN_EPS = RW_HEAD * 1e-5
GD_WIDTH = 512
GD_HEAD = 128
GD_HEADS = GD_WIDTH // GD_HEAD
GD_CONV = 4
GD_CHUNK = 64
NORM_EPS = 1e-6

RW_SHIFT_COLS = 3 * RW_WIDTH + RW_DECAY_RANK + RW_AAA_RANK
SPLIT_RW_Z = RW_SHIFT_COLS
SPLIT_GD_QKV = SPLIT_RW_Z + RW_WIDTH
SPLIT_GD_Z = SPLIT_GD_QKV + 3 * GD_WIDTH
SPLIT_GD_BETA = SPLIT_GD_Z + GD_WIDTH
SPLIT_GD_ALPHA = SPLIT_GD_BETA + GD_HEADS
SPLIT_GATES = SPLIT_GD_ALPHA + GD_HEADS
IN_COLS = SPLIT_GATES + 2 * D_MODEL

kernel_name = "rwkv7_gdn_gated_parallel_block"


def rms_norm(x, g, eps=NORM_EPS):
    xf = x.astype(jnp.float32)
    y = xf * lax.rsqrt(jnp.mean(xf * xf, axis=-1, keepdims=True) + eps)
    return (y * g.astype(jnp.float32)).astype(x.dtype)


def l2_normalize(x, eps=1e-12):
    xf = x.astype(jnp.float32)
    return (xf * lax.rsqrt(jnp.sum(xf * xf, axis=-1, keepdims=True) + eps)).astype(x.dtype)


def token_shift(p, mu):
    prev = jnp.pad(p, ((0, 0), (1, 0), (0, 0)))[:, :-1]
    return p + (prev - p) * mu


def causal_depthwise_conv(x, w):
    K, C = w.shape
    return lax.conv_general_dilated(
        x, w[:, None, :].astype(x.dtype), window_strides=(1,), padding=[(K - 1, 0)],
        dimension_numbers=("NWC", "WIO", "NWC"), feature_group_count=C)


def rwkv7_recurrence(r, w, k, v, kk, b):
    f32 = jnp.float32
    B, T, H, N = r.shape

    def step(S, inp):
        r_t, w_t, k_t, v_t, kk_t, b_t = inp
        sa = jnp.einsum("bhvk,bhk->bhv", S, -kk_t)
        S = S * w_t[:, :, None, :] + sa[..., :, None] * b_t[..., None, :] + v_t[..., :, None] * k_t[..., None, :]
        return S, jnp.einsum("bhvk,bhk->bhv", S, r_t)

    xs = tuple(jnp.moveaxis(t.astype(f32), 1, 0) for t in (r, w, k, v, kk, b))
    _, y = lax.scan(step, jnp.zeros((B, H, N, N), f32), xs)
    return jnp.moveaxis(y, 0, 1)


def rwkv7_branch(p_rw, z_rw, mu, w0, w2, a0, a2, k_k, k_a, r_k, gn_w, gn_b):
    B, T, _ = p_rw.shape
    xs = token_shift(p_rw, mu)
    r, k, v, wd, ad = jnp.split(
        xs, [RW_WIDTH, 2 * RW_WIDTH, 3 * RW_WIDTH, 3 * RW_WIDTH + RW_DECAY_RANK], axis=-1)
    log_w = -jax.nn.softplus(-(w0 + jnp.tanh(wd) @ w2)) - 0.5
    decay = jnp.exp(-jnp.exp(log_w.astype(jnp.float32)))
    a = jax.nn.sigmoid(a0 + ad @ a2)
    heads = lambda t: t.reshape(B, T, RW_HEADS, RW_HEAD)
    kk = l2_normalize(heads(k * k_k))
    k = k * (1 + (a - 1) * k_a)
    r_h, k_h, v_h, a_h, w_h = heads(r), heads(k), heads(v), heads(a), heads(decay)
    y = rwkv7_recurrence(r_h, w_h, k_h, v_h, kk, kk * a_h)
    mean = jnp.mean(y, axis=-1, keepdims=True)
    var = jnp.mean(jnp.square(y - mean), axis=-1, keepdims=True)
    y = ((y - mean) * lax.rsqrt(var + RW_GN_EPS)).reshape(B, T, RW_WIDTH) * gn_w + gn_b
    bonus = jnp.sum(r_h * k_h * r_k, axis=-1, keepdims=True) * v_h
    y = (y + bonus.reshape(B, T, RW_WIDTH)).astype(p_rw.dtype)
    return y * jax.nn.silu(z_rw)


def chunk_gated_delta_rule(q, k, v, g, beta):
    f32 = jnp.float32
    B, H, T, D = q.shape
    C = GD_CHUNK
    N = T // C
    chunks = lambda t: t.astype(f32).reshape((B, H, N, C) + t.shape[3:])
    q = chunks(q) * (D ** -0.5)
    k, v = chunks(k), chunks(v)
    g = jnp.cumsum(chunks(g), axis=-1)
    beta = chunks(beta)
    k_beta = k * beta[..., None]
    v_beta = v * beta[..., None]
    causal = jnp.tril(jnp.ones((C, C), bool))
    strict = jnp.tril(jnp.ones((C, C), bool), -1)
    decay = jnp.exp(jnp.where(causal, g[..., :, None] - g[..., None, :], -jnp.inf))
    eye = jnp.eye(C, dtype=f32)
    A = jnp.where(strict, jnp.einsum("bhnid,bhnjd->bhnij", k_beta, k) * decay, 0.0)
    t_inv = lax.linalg.triangular_solve(eye + A, jnp.broadcast_to(eye, A.shape),
                                        left_side=True, lower=True, unit_diagonal=True)
    u = t_inv @ v_beta
    w = t_inv @ (k_beta * jnp.exp(g)[..., None])
    qk = jnp.einsum("bhnid,bhnjd->bhnij", q, k) * decay
    q_decayed = q * jnp.exp(g)[..., None]
    g_last = g[..., -1]
    k_to_end = k * jnp.exp(g_last[..., None] - g)[..., None]

    def step(S, inp):
        qd_c, kte_c, u_c, w_c, qk_c, gl_c = inp
        v_new = u_c - w_c @ S
        o = qd_c @ S + qk_c @ v_new
        S = S * jnp.exp(gl_c)[..., None, None] + jnp.einsum("bhck,bhcv->bhkv", kte_c, v_new)
        return S, o

    xs = tuple(jnp.moveaxis(t, 2, 0) for t in (q_decayed, k_to_end, u, w, qk, g_last))
    _, o = lax.scan(step, jnp.zeros((B, H, D, v.shape[-1]), f32), xs)
    return jnp.moveaxis(o, 0, 2).reshape(B, H, T, -1)


def gdn_branch(qkv, z, beta_logit, alpha, conv_w, A_log, dt_bias, o_norm_w):
    B, T, _ = qkv.shape
    f32 = jnp.float32
    qkv = jax.nn.silu(causal_depthwise_conv(qkv, conv_w))
    q, k, v = jnp.split(qkv, 3, axis=-1)
    heads = lambda t: jnp.swapaxes(t.reshape(B, T, GD_HEADS, GD_HEAD), 1, 2)
    q, k, v = l2_normalize(heads(q)), l2_normalize(heads(k)), heads(v)
    beta = jnp.swapaxes(jax.nn.sigmoid(beta_logit.astype(f32)), 1, 2)
    g = -jnp.exp(A_log.astype(f32)) * jax.nn.softplus(alpha.astype(f32) + dt_bias.astype(f32))
    g = jnp.swapaxes(g, 1, 2)
    o = jnp.swapaxes(chunk_gated_delta_rule(q, k, v, g, beta), 1, 2)
    o = rms_norm(o, o_norm_w) * jax.nn.silu(z.reshape(B, T, GD_HEADS, GD_HEAD).astype(f32))
    return o.reshape(B, T, GD_WIDTH).astype(qkv.dtype)


def setup_inputs(seed: int = 0) -> dict:
    key = jax.random.key(seed)
    ks = jax.random.split(key, 24)
    f32 = jnp.float32
    L = DEPTH
    nrm = lambda k, shape, scale: jax.random.normal(k, shape, f32) * scale
    x = nrm(ks[0], (BATCH, SEQ, D_MODEL), 1.0)
    norm_in_w = 1.0 + nrm(ks[1], (L, D_MODEL), 0.02)
    w_in = nrm(ks[2], (L, D_MODEL, IN_COLS), D_MODEL ** -0.5)
    rw_mu = jax.random.uniform(ks[3], (L, RW_SHIFT_COLS), f32)
    rw_w0 = jax.random.uniform(ks[4], (L, RW_WIDTH), f32, -5.0, 1.0)
    rw_w2 = nrm(ks[5], (L, RW_DECAY_RANK, RW_WIDTH), 0.1)
    rw_a0 = nrm(ks[6], (L, RW_WIDTH), 0.1)
    rw_a2 = nrm(ks[7], (L, RW_AAA_RANK, RW_WIDTH), 0.1)
    rw_k_k = 0.85 + nrm(ks[8], (L, RW_WIDTH), 0.02)
    rw_k_a = 1.0 + nrm(ks[9], (L, RW_WIDTH), 0.02)
    rw_r_k = nrm(ks[10], (L, RW_HEADS, RW_HEAD), 0.1)
    rw_gn_w = 1.0 + nrm(ks[11], (L, RW_WIDTH), 0.02)
    rw_gn_b = nrm(ks[12], (L, RW_WIDTH), 0.02)
    gd_conv_w = nrm(ks[13], (L, GD_CONV, 3 * GD_WIDTH), GD_CONV ** -0.5)
    gd_A_log = jnp.log(jax.random.uniform(ks[14], (L, GD_HEADS), f32, 1.0, 16.0))
    dt = jnp.exp(jax.random.uniform(ks[15], (L, GD_HEADS), f32, math.log(1e-3), math.log(1e-1)))
    gd_dt_bias = dt + jnp.log(-jnp.expm1(-dt))
    gd_o_norm_w = 1.0 + nrm(ks[16], (L, GD_HEAD), 0.02)
    w_branch_a = nrm(ks[17], (L, RW_WIDTH, D_MODEL), RW_WIDTH ** -0.5)
    w_branch_b = nrm(ks[18], (L, GD_WIDTH, D_MODEL), GD_WIDTH ** -0.5)
    w_out = nrm(ks[19], (L, D_MODEL, D_MODEL), D_MODEL ** -0.5)
    norm_out_w = 1.0 + nrm(ks[20], (D_MODEL,), 0.02)
    return {"x": x, "norm_in_w": norm_in_w, "w_in": w_in, "rw_mu": rw_mu, "rw_w0": rw_w0,
            "rw_w2": rw_w2, "rw_a0": rw_a0, "rw_a2": rw_a2, "rw_k_k": rw_k_k, "rw_k_a": rw_k_a,
            "rw_r_k": rw_r_k, "rw_gn_w": rw_gn_w, "rw_gn_b": rw_gn_b, "gd_conv_w": gd_conv_w,
            "gd_A_log": gd_A_log, "gd_dt_bias": gd_dt_bias, "gd_o_norm_w": gd_o_norm_w,
            "w_branch_a": w_branch_a, "w_branch_b": w_branch_b, "w_out": w_out,
            "norm_out_w": norm_out_w}


def reference(x, norm_in_w, w_in, rw_mu, rw_w0, rw_w2, rw_a0, rw_a2, rw_k_k, rw_k_a, rw_r_k,
              rw_gn_w, rw_gn_b, gd_conv_w, gd_A_log, gd_dt_bias, gd_o_norm_w,
              w_branch_a, w_branch_b, w_out, norm_out_w):
    for l in range(DEPTH):
        h = rms_norm(x, norm_in_w[l])
        p = h @ w_in[l]
        p_rw, z_rw, qkv_gd, z_gd, beta_gd, alpha_gd, gates = jnp.split(
            p, [SPLIT_RW_Z, SPLIT_GD_QKV, SPLIT_GD_Z, SPLIT_GD_BETA, SPLIT_GD_ALPHA, SPLIT_GATES],
            axis=-1)
        y_a = rwkv7_branch(p_rw, z_rw, rw_mu[l], rw_w0[l], rw_w2[l], rw_a0[l], rw_a2[l],
                           rw_k_k[l], rw_k_a[l], rw_r_k[l], rw_gn_w[l], rw_gn_b[l])
        y_b = gdn_branch(qkv_gd, z_gd, beta_gd, alpha_gd, gd_conv_w[l], gd_A_log[l],
                         gd_dt_bias[l], gd_o_norm_w[l])
        gate_a, gate_b = jnp.split(gates, 2, axis=-1)
        merged = (jax.nn.sigmoid(gate_a) * (y_a @ w_branch_a[l])
                  + jax.nn.sigmoid(gate_b) * (y_b @ w_branch_b[l]))
        x = x + merged @ w_out[l]
    return rms_norm(x, norm_out_w)
```

```python
import functools

import jax
import jax.numpy as jnp
from jax import lax
from jax.experimental import pallas as pl
from jax.experimental.pallas import tpu as pltpu

F32 = jnp.float32
BF16 = jnp.bfloat16

D_MODEL = 1024
RW_WIDTH = 512
RW_HEAD = 64
RW_RANK = 64
RW_GN_EPS = RW_HEAD * 1e-5
GD_WIDTH = 512
GD_HEAD = 128
GD_HEADS = GD_WIDTH // GD_HEAD
GD_CONV = 4
NORM_EPS = 1e-6
L2_EPS = 1e-12

LANES = 128
CHUNK = 64
N_PAIRS = RW_WIDTH // LANES
RW_SHIFT_COLS = 3 * RW_WIDTH + 2 * RW_RANK
C_ZRW = RW_SHIFT_COLS
C_QKV = C_ZRW + RW_WIDTH
C_ZGD = C_QKV + 3 * GD_WIDTH
C_GATES = C_ZGD + GD_WIDTH
C_BA = C_GATES + 2 * D_MODEL
PROJ_COLS = C_BA + LANES

PROJ_TM = 256
PREV_ROWS = 16
REC_TT = 256
MERGE_TM = 512
VMEM_LIMIT = 56 * 1024 * 1024


def _mm(a, b):
    return jnp.dot(a.astype(BF16), b.astype(BF16), preferred_element_type=F32)


def _mm_nt(a, b):
    return lax.dot_general(a.astype(BF16), b.astype(BF16), (((1,), (1,)), ((), ())),
                           preferred_element_type=F32)


def _mm_tn(a, b):
    return lax.dot_general(a.astype(BF16), b.astype(BF16), (((0,), (0,)), ((), ())),
                           preferred_element_type=F32)


def _sigmoid(x):
    return 1.0 / (1.0 + jnp.exp(-x))


def _silu(x):
    return x * _sigmoid(x)


def _softplus(x):
    return jnp.maximum(x, 0.0) + jnp.log(1.0 + jnp.exp(-jnp.abs(x)))


def _split3(x):
    hi = x.astype(BF16)
    r1 = x - hi.astype(F32)
    mid = r1.astype(BF16)
    lo = (r1 - mid.astype(F32)).astype(BF16)
    return hi, mid, lo


def _blk(idx, size):
    return lax.shift_right_logical(idx, size.bit_length() - 1)


def _tri_masks(n):
    ri = lax.broadcasted_iota(jnp.int32, (n, n), 0)
    ci = lax.broadcasted_iota(jnp.int32, (n, n), 1)
    same = _blk(ri, CHUNK) == _blk(ci, CHUNK)
    return ri, ci, same


def _unit_lower_inverse(lmat, ri, ci):
    eye = (ri == ci).astype(F32)
    d = jnp.where(_blk(ri, 8) == _blk(ci, 8), lmat, 0.0)
    d2 = _mm(d, d)
    d4 = _mm(d2, d2)
    t = eye + d
    t = t + _mm(t, d2)
    t = t + _mm(t, d4)
    s = 8
    while s < CHUNK:
        e = jnp.where((_blk(ri, 2 * s) == _blk(ci, 2 * s)) & (_blk(ri, s) != _blk(ci, s)),
                      lmat, 0.0)
        t = t + _mm(t, _mm(e, t))
        s *= 2
    return t


def _proj_kernel(xp_ref, x_ref, g_ref, w_ref, wbat_ref, mu_ref, cw_ref,
                 rkv_ref, wa_ref, zrw_ref, qkv_ref, zgd_ref, gates_ref, ba_ref, bat_ref,
                 *, tiles_per_seq):
    i = pl.program_id(0)
    tm = x_ref.shape[0]

    def normed(v):
        ms = jnp.mean(v * v, axis=-1, keepdims=True)
        return v * lax.rsqrt(ms + NORM_EPS) * g_ref[...]

    keep = (i % tiles_per_seq != 0).astype(F32)
    hp = (normed(xp_ref[...]) * keep).astype(BF16)
    hc = normed(x_ref[...]).astype(BF16)
    he = jnp.concatenate([hp, hc], axis=0)

    def proj(c0, c1):
        return jnp.dot(he, w_ref[:, c0:c1], preferred_element_type=F32)

    def shifted(pe, s):
        return pltpu.roll(pe, s, 0)[PREV_ROWS:]

    for c0 in range(0, RW_SHIFT_COLS, 512):
        c1 = min(c0 + 512, RW_SHIFT_COLS)
        pe = proj(c0, c1)
        cur = pe[PREV_ROWS:]
        xs = cur + (shifted(pe, 1) - cur) * mu_ref[:, c0:c1]
        for j in range((c1 - c0) // LANES):
            g = c0 // LANES + j
            blk = xs[:, j * LANES:(j + 1) * LANES].astype(BF16)
            if g < 3 * N_PAIRS:
                rkv_ref[g] = blk
            else:
                wa_ref[...] = blk

    pe = proj(C_ZRW, C_QKV)[PREV_ROWS:]
    for j in range(N_PAIRS):
        zrw_ref[j] = pe[:, j * LANES:(j + 1) * LANES].astype(BF16)

    for c0 in range(0, 3 * GD_WIDTH, 512):
        pe = proj(C_QKV + c0, C_QKV + c0 + 512)
        acc = pe[PREV_ROWS:] * cw_ref[GD_CONV - 1:GD_CONV, c0:c0 + 512]
        for s in range(1, GD_CONV):
            acc = acc + shifted(pe, s) * cw_ref[GD_CONV - 1 - s:GD_CONV - s, c0:c0 + 512]
        act = _silu(acc)
        for j in range(512 // LANES):
            qkv_ref[c0 // LANES + j] = act[:, j * LANES:(j + 1) * LANES].astype(BF16)

    pe = proj(C_ZGD, C_GATES)[PREV_ROWS:]
    for j in range(GD_HEADS):
        zgd_ref[j] = pe[:, j * LANES:(j + 1) * LANES].astype(BF16)

    for c0 in range(0, 2 * D_MODEL, 512):
        gates_ref[:, c0:c0 + 512] = proj(C_GATES + c0, C_GATES + c0 + 512)[PREV_ROWS:].astype(BF16)

    ba_ref[...] = proj(C_BA, PROJ_COLS)[PREV_ROWS:]
    bat_ref[...] = lax.dot_general(wbat_ref[...], hc, (((1,), (1,)), ((), ())),
                                   preferred_element_type=F32)


def _projection(x2, norm_w, w_perm, w_bat, mu, conv_w, seq_len):
    bt = x2.shape[0]
    tm = PROJ_TM
    n_tiles = bt // tm
    tiles_per_seq = seq_len // tm
    prev_blocks = tm // PREV_ROWS
    const = lambda i: (0, 0)
    out_shapes = (
        jax.ShapeDtypeStruct((3 * N_PAIRS, bt, LANES), BF16),
        jax.ShapeDtypeStruct((bt, LANES), BF16),
        jax.ShapeDtypeStruct((N_PAIRS, bt, LANES), BF16),
        jax.ShapeDtypeStruct((3 * GD_HEADS, bt, LANES), BF16),
        jax.ShapeDtypeStruct((GD_HEADS, bt, LANES), BF16),
        jax.ShapeDtypeStruct((bt, 2 * D_MODEL), BF16),
        jax.ShapeDtypeStruct((bt, LANES), F32),
        jax.ShapeDtypeStruct((8, bt), F32),
    )
    out_specs = (
        pl.BlockSpec((3 * N_PAIRS, tm, LANES), lambda i: (0, i, 0)),
        pl.BlockSpec((tm, LANES), lambda i: (i, 0)),
        pl.BlockSpec((N_PAIRS, tm, LANES), lambda i: (0, i, 0)),
        pl.BlockSpec((3 * GD_HEADS, tm, LANES), lambda i: (0, i, 0)),
        pl.BlockSpec((GD_HEADS, tm, LANES), lambda i: (0, i, 0)),
        pl.BlockSpec((tm, 2 * D_MODEL), lambda i: (i, 0)),
        pl.BlockSpec((tm, LANES), lambda i: (i, 0)),
        pl.BlockSpec((8, tm), lambda i: (0, i)),
    )
    in_specs = [
        pl.BlockSpec((PREV_ROWS, D_MODEL), lambda i: (jnp.maximum(i * prev_blocks - 1, 0), 0)),
        pl.BlockSpec((tm, D_MODEL), lambda i: (i, 0)),
        pl.BlockSpec((1, D_MODEL), const),
        pl.BlockSpec((D_MODEL, PROJ_COLS), const, pipeline_mode=pl.Buffered(1)),
        pl.BlockSpec((8, D_MODEL), const),
        pl.BlockSpec((1, RW_SHIFT_COLS), const),
        pl.BlockSpec((GD_CONV, 3 * GD_WIDTH), const),
    ]
    return pl.pallas_call(
        functools.partial(_proj_kernel, tiles_per_seq=tiles_per_seq),
        grid=(n_tiles,),
        in_specs=in_specs,
        out_specs=out_specs,
        out_shape=out_shapes,
        compiler_params=pltpu.CompilerParams(
            dimension_semantics=("arbitrary",), vmem_limit_bytes=VMEM_LIMIT),
        name="rmsnorm_in_proj",
    )(x2, x2, norm_w, w_perm, w_bat, mu, conv_w)


def _rwkv_kernel(r_ref, k_ref, v_ref, wa_ref, z_ref, ltri_ref, wda_ref, w0_ref, a0_ref,
                 kk_ref, ka_ref, rk_ref, gnw_ref, gnb_ref, y_ref, s_scr):
    t_idx = pl.program_id(2)

    @pl.when(t_idx == 0)
    def _():
        s_scr[...] = jnp.zeros_like(s_scr)

    tt = r_ref.shape[0]
    r = r_ref[...].astype(F32)
    k = k_ref[...].astype(F32)
    v = v_ref[...].astype(F32)
    wa = wa_ref[...].astype(F32)
    lane = lax.broadcasted_iota(jnp.int32, (1, LANES), 1)
    head_a = lane < RW_HEAD
    ri, ci, same = _tri_masks(LANES)
    strict = same & (ci < ri)
    causal = same & (ci <= ri)
    seg = (_blk(ri, RW_HEAD) == _blk(ci, RW_HEAD)).astype(BF16)

    da = jnp.where(lane < RW_RANK, jnp.tanh(wa), wa)
    pre = _mm(da, wda_ref[...])
    log_w = -_softplus(-(w0_ref[...] + pre[:, :LANES])) - 0.5
    lw = -jnp.exp(log_w)
    lr = _sigmoid(a0_ref[...] + pre[:, LANES:])
    kkr = k * kk_ref[...]
    kk = kkr * lax.rsqrt(_mm(kkr * kkr, seg) + L2_EPS)
    k2 = k * (1.0 + (lr - 1.0) * ka_ref[...])
    b = kk * lr
    bonus = _mm(r * k2 * rk_ref[...], seg) * v

    lw_hi = lw.astype(BF16)
    lw_lo = (lw - lw_hi.astype(F32)).astype(BF16)
    cl2 = _mm(ltri_ref[...], jnp.concatenate([lw_hi, lw_lo], axis=1))
    clw = cl2[:, :LANES] + cl2[:, LANES:]
    e_neg = jnp.exp(-clw)
    rt = r * jnp.exp(clw)
    at = -kk * jnp.exp(clw - lw)
    bt = b * e_neg
    kt = k2 * e_neg

    def pair_rows(x):
        return jnp.concatenate([jnp.where(head_a, x, 0.0), jnp.where(head_a, 0.0, x)], axis=0)

    ys = []
    for c in range(tt // CHUNK):
        r0 = c * CHUNK
        sl = slice(r0, r0 + CHUNK)
        cl_last = clw[r0 + CHUNK - 1:r0 + CHUNK, :]
        e_end = jnp.exp(cl_last - clw[sl])
        at_p, rt_p = pair_rows(at[sl]), pair_rows(rt[sl])
        v_p = pair_rows(v[sl])
        lhs = jnp.concatenate([at_p, rt_p], axis=0)
        m_b = _mm_nt(lhs, pair_rows(bt[sl]))
        m_k = _mm_nt(lhs, pair_rows(kt[sl]))
        t_inv = _unit_lower_inverse(jnp.where(strict, m_b[:LANES], 0.0), ri, ci)
        akv = _mm(jnp.where(strict, m_k[:LANES], 0.0), v_p)
        tw = _mm(t_inv, jnp.concatenate([at_p, akv], axis=1))
        s0 = s_scr[...]
        g = _mm_nt(jnp.concatenate([tw[:, :LANES], rt_p], axis=0), s0)
        u = g[:LANES] + tw[:, LANES:]
        uv = jnp.concatenate([u, v_p], axis=0)
        rbk = jnp.concatenate([jnp.where(causal, m_b[LANES:], 0.0),
                               jnp.where(causal, m_k[LANES:], 0.0)], axis=1)
        y_p = g[LANES:] + _mm(rbk, uv)
        ys.append(y_p[:CHUNK] + y_p[CHUNK:])
        bk_end = jnp.concatenate([pair_rows(b[sl] * e_end), pair_rows(k2[sl] * e_end)], axis=0)
        s_scr[...] = s0 * jnp.exp(cl_last) + _mm_tn(uv, bk_end)

    y = jnp.concatenate(ys, axis=0)
    inv_n = 1.0 / RW_HEAD
    yc = y - _mm(y, seg) * inv_n
    var = _mm(yc * yc, seg) * inv_n
    yn = yc * lax.rsqrt(var + RW_GN_EPS) * gnw_ref[...] + gnb_ref[...]
    y_ref[...] = ((yn + bonus) * _silu(z_ref[...].astype(F32))).astype(y_ref.dtype)


def _rwkv_branch(rkv, wa, zrw, ltri, wda, w0, a0, k_k, k_a, r_k, gn_w, gn_b, batch, seq_len):
    tt = REC_TT
    nt = seq_len // tt
    bt = batch * seq_len
    tok = lambda g0: pl.BlockSpec((None, tt, LANES), lambda b, j, t, g0=g0: (g0 + j, b * nt + t, 0))
    par = pl.BlockSpec((None, 1, LANES), lambda b, j, t: (j, 0, 0))
    in_specs = [
        tok(0), tok(N_PAIRS), tok(2 * N_PAIRS),
        pl.BlockSpec((tt, LANES), lambda b, j, t: (b * nt + t, 0)),
        pl.BlockSpec((None, tt, LANES), lambda b, j, t: (j, b * nt + t, 0)),
        pl.BlockSpec((tt, tt), lambda b, j, t: (0, 0)),
        pl.BlockSpec((None, LANES, 2 * LANES), lambda b, j, t: (j, 0, 0)),
        par, par, par, par, par, par, par,
    ]
    return pl.pallas_call(
        _rwkv_kernel,
        grid=(batch, N_PAIRS, nt),
        in_specs=in_specs,
        out_specs=pl.BlockSpec((None, tt, LANES), lambda b, j, t: (j, b * nt + t, 0)),
        out_shape=jax.ShapeDtypeStruct((N_PAIRS, bt, LANES), BF16),
        scratch_shapes=[pltpu.VMEM((LANES, LANES), F32)],
        compiler_params=pltpu.CompilerParams(
            dimension_semantics=("arbitrary", "arbitrary", "arbitrary"),
            vmem_limit_bytes=VMEM_LIMIT),
        name="rwkv7_chunked",
    )(rkv, rkv, rkv, wa, zrw, ltri, wda, w0, a0, k_k, k_a, r_k, gn_w, gn_b)


def _gdn_kernel(q_ref, k_ref, v_ref, z_ref, ba_ref, bat_ref, ltri_ref, alog_ref, dtb_ref,
                onw_ref, y_ref, s_scr):
    h = pl.program_id(1)
    t_idx = pl.program_id(2)

    @pl.when(t_idx == 0)
    def _():
        s_scr[...] = jnp.zeros_like(s_scr)

    tt = q_ref.shape[0]
    q = q_ref[...].astype(F32)
    k = k_ref[...].astype(F32)
    v = v_ref[...].astype(F32)
    ones = jnp.ones((LANES, LANES), BF16)
    qn = q * (lax.rsqrt(_mm(q * q, ones) + L2_EPS) * (GD_HEAD ** -0.5))
    kn = k * lax.rsqrt(_mm(k * k, ones) + L2_EPS)

    lane = lax.broadcasted_iota(jnp.int32, (1, LANES), 1)
    ba = ba_ref[...]
    beta = _sigmoid(jnp.sum(jnp.where(lane == h, ba, 0.0), axis=-1, keepdims=True))
    alpha = jnp.sum(jnp.where(lane == GD_HEADS + h, ba, 0.0), axis=-1, keepdims=True)
    neg_a = -jnp.exp(alog_ref[...])
    g_col = neg_a * _softplus(alpha + dtb_ref[...])
    alpha_row = bat_ref[pl.ds(GD_HEADS + h, 1), :]
    g_row = neg_a[:, :1] * _softplus(alpha_row + dtb_ref[:, :1])

    ltri = ltri_ref[...]
    gc3 = _mm(ltri, jnp.concatenate(_split3(g_col), axis=1))
    gc_col = gc3[:, :LANES] + gc3[:, LANES:2 * LANES] + gc3[:, 2 * LANES:]
    g_row8 = jnp.broadcast_to(g_row, (8, tt))
    gr3 = _mm_nt(jnp.concatenate(_split3(g_row8), axis=0), ltri)
    gc_row = gr3[0:8] + gr3[8:16] + gr3[16:24]

    ri, ci, same = _tri_masks(LANES)
    strict = same & (ci < ri)
    causal = same & (ci <= ri)
    e_col = jnp.exp(gc_col)
    kb = kn * beta
    vb = v * beta
    qd = qn * e_col

    os = []
    for sc in range(tt // LANES):
        p0 = sc * LANES
        sl = slice(p0, p0 + LANES)
        diff = gc_col[sl] - gc_row[0:1, p0:p0 + LANES]
        dec = jnp.exp(jnp.where(causal, diff, -1e30))
        m = _mm_nt(jnp.concatenate([kb[sl], qn[sl]], axis=0), kn[sl])
        a_mat = jnp.where(strict, m[:LANES] * dec, 0.0)
        t_inv = _unit_lower_inverse(-a_mat, ri, ci)
        qk = m[LANES:] * dec
        uw = _mm(t_inv, jnp.concatenate([vb[sl], kb[sl] * e_col[sl]], axis=1))
        zeros = jnp.zeros((CHUNK, LANES), F32)
        for cc in range(LANES // CHUNK):
            r0 = cc * CHUNK
            rows = slice(p0 + r0, p0 + r0 + CHUNK)
            s0 = s_scr[...]
            g = _mm(jnp.concatenate([uw[r0:r0 + CHUNK, LANES:], qd[rows]], axis=0), s0)
            v_new = uw[r0:r0 + CHUNK, :LANES] - g[:CHUNK]
            v_pad = jnp.concatenate([v_new, zeros] if cc == 0 else [zeros, v_new], axis=0)
            os.append(g[CHUNK:] + _mm(qk[r0:r0 + CHUNK], v_pad))
            g_last = gc_col[p0 + r0 + CHUNK - 1:p0 + r0 + CHUNK, :]
            kte = kn[rows] * jnp.exp(g_last - gc_col[rows])
            s_scr[...] = s0 * jnp.exp(g_last) + _mm_tn(kte, v_new)

    o = jnp.concatenate(os, axis=0)
    ms = _mm(o * o, ones) * (1.0 / GD_HEAD)
    on = o * lax.rsqrt(ms + NORM_EPS) * onw_ref[...]
    y_ref[...] = (on * _silu(z_ref[...].astype(F32))).astype(y_ref.dtype)


def _gdn_branch(qkv, zgd, ba, bat, ltri, alog, dtb, onw, batch, seq_len):
    tt = REC_TT
    nt = seq_len // tt
    bt = batch * seq_len
    tok = lambda g0: pl.BlockSpec((None, tt, LANES), lambda b, h, t, g0=g0: (g0 + h, b * nt + t, 0))
    par = pl.BlockSpec((None, 1, LANES), lambda b, h, t: (h, 0, 0))
    in_specs = [
        tok(0), tok(GD_HEADS), tok(2 * GD_HEADS),
        pl.BlockSpec((None, tt, LANES), lambda b, h, t: (h, b * nt + t, 0)),
        pl.BlockSpec((tt, LANES), lambda b, h, t: (b * nt + t, 0)),
        pl.BlockSpec((8, tt), lambda b, h, t: (0, b * nt + t)),
        pl.BlockSpec((tt, tt), lambda b, h, t: (0, 0)),
        par, par,
        pl.BlockSpec((1, LANES), lambda b, h, t: (0, 0)),
    ]
    return pl.pallas_call(
        _gdn_kernel,
        grid=(batch, GD_HEADS, nt),
        in_specs=in_specs,
        out_specs=pl.BlockSpec((None, tt, LANES), lambda b, h, t: (h, b * nt + t, 0)),
        out_shape=jax.ShapeDtypeStruct((GD_HEADS, bt, LANES), BF16),
        scratch_shapes=[pltpu.VMEM((LANES, LANES), F32)],
        compiler_params=pltpu.CompilerParams(
            dimension_semantics=("arbitrary", "arbitrary", "arbitrary"),
            vmem_limit_bytes=VMEM_LIMIT),
        name="gated_delta_chunked",
    )(qkv, qkv, qkv, zgd, ba, bat, ltri, alog, dtb, onw)


def _merge_kernel(x_ref, ya_ref, yb_ref, gates_ref, wa_ref, wb_ref, wo_ref, g_ref, o_ref):
    ya = jnp.concatenate([ya_ref[j] for j in range(N_PAIRS)], axis=1)
    yb = jnp.concatenate([yb_ref[j] for j in range(GD_HEADS)], axis=1)
    ma = jnp.dot(ya, wa_ref[...], preferred_element_type=F32)
    mb = jnp.dot(yb, wb_ref[...], preferred_element_type=F32)
    gates = gates_ref[...].astype(F32)
    merged = _sigmoid(gates[:, :D_MODEL]) * ma + _sigmoid(gates[:, D_MODEL:]) * mb
    res = x_ref[...] + jnp.dot(merged.astype(BF16), wo_ref[...], preferred_element_type=F32)
    ms = jnp.mean(res * res, axis=-1, keepdims=True)
    o_ref[...] = res * lax.rsqrt(ms + NORM_EPS) * g_ref[...]


def _merge(x2, ya, yb, gates, w_a, w_b, w_o, norm_w):
    bt = x2.shape[0]
    tm = MERGE_TM
    const = lambda i: (0, 0)
    in_specs = [
        pl.BlockSpec((tm, D_MODEL), lambda i: (i, 0)),
        pl.BlockSpec((N_PAIRS, tm, LANES), lambda i: (0, i, 0)),
        pl.BlockSpec((GD_HEADS, tm, LANES), lambda i: (0, i, 0)),
        pl.BlockSpec((tm, 2 * D_MODEL), lambda i: (i, 0)),
        pl.BlockSpec((RW_WIDTH, D_MODEL), const),
        pl.BlockSpec((GD_WIDTH, D_MODEL), const),
        pl.BlockSpec((D_MODEL, D_MODEL), const),
        pl.BlockSpec((1, D_MODEL), const),
    ]
    return pl.pallas_call(
        _merge_kernel,
        grid=(bt // tm,),
        in_specs=in_specs,
        out_specs=pl.BlockSpec((tm, D_MODEL), lambda i: (i, 0)),
        out_shape=jax.ShapeDtypeStruct((bt, D_MODEL), F32),
        compiler_params=pltpu.CompilerParams(
            dimension_semantics=("arbitrary",), vmem_limit_bytes=VMEM_LIMIT),
        name="merge_out_proj",
    )(x2, ya, yb, gates, w_a, w_b, w_o, norm_w)


def _block_lower_tri(n):
    i = jnp.arange(n)
    return (((i[:, None] // CHUNK) == (i[None, :] // CHUNK)) & (i[None, :] <= i[:, None])).astype(BF16)


def _layer(x2, batch, seq_len, norm_in_w, w_in, rw_mu, rw_w0, rw_w2, rw_a0, rw_a2, rw_k_k,
           rw_k_a, rw_r_k, rw_gn_w, rw_gn_b, gd_conv_w, gd_A_log, gd_dt_bias, gd_o_norm_w,
           w_branch_a, w_branch_b, w_out):
    c_beta = C_ZGD + GD_WIDTH
    c_gates = c_beta + 2 * GD_HEADS
    w_perm = jnp.concatenate(
        [w_in[:, :c_beta], w_in[:, c_gates:], w_in[:, c_beta:c_gates],
         jnp.zeros((D_MODEL, LANES - 2 * GD_HEADS), w_in.dtype)], axis=1).astype(BF16)
    w_bat = w_in[:, c_beta:c_gates].T.astype(BF16)

    rkv, wa, zrw, qkv, zgd, gates, ba, bat = _projection(
        x2, norm_in_w.reshape(1, D_MODEL), w_perm, w_bat,
        rw_mu.reshape(1, RW_SHIFT_COLS), gd_conv_w, seq_len)

    ltri = _block_lower_tri(REC_TT)
    pair = lambda p: p.reshape(N_PAIRS, 1, LANES)
    zero = jnp.zeros((RW_RANK, LANES), F32)
    wda = jnp.stack([
        jnp.concatenate([
            jnp.concatenate([rw_w2[:, j * LANES:(j + 1) * LANES], zero], axis=1),
            jnp.concatenate([zero, rw_a2[:, j * LANES:(j + 1) * LANES]], axis=1)], axis=0)
        for j in range(N_PAIRS)]).astype(BF16)
    ya = _rwkv_branch(rkv, wa, zrw, ltri, wda, pair(rw_w0), pair(rw_a0), pair(rw_k_k),
                      pair(rw_k_a), pair(rw_r_k.reshape(RW_WIDTH)), pair(rw_gn_w),
                      pair(rw_gn_b), batch, seq_len)

    head = lambda p: jnp.broadcast_to(p.reshape(GD_HEADS, 1, 1), (GD_HEADS, 1, LANES))
    yb = _gdn_branch(qkv, zgd, ba, bat, ltri, head(gd_A_log), head(gd_dt_bias),
                     gd_o_norm_w.reshape(1, GD_HEAD), batch, seq_len)
    return gates, ya, yb


def kernel(x, norm_in_w, w_in, rw_mu, rw_w0, rw_w2, rw_a0, rw_a2, rw_k_k, rw_k_a, rw_r_k,
           rw_gn_w, rw_gn_b, gd_conv_w, gd_A_log, gd_dt_bias, gd_o_norm_w,
           w_branch_a, w_branch_b, w_out, norm_out_w):
    batch, seq_len, d = x.shape
    assert d == D_MODEL and norm_in_w.shape[0] == 1
    assert seq_len % max(PROJ_TM, REC_TT) == 0 and (batch * seq_len) % MERGE_TM == 0
    x2 = x.reshape(batch * seq_len, d)
    gates, ya, yb = _layer(
        x2, batch, seq_len, norm_in_w[0], w_in[0], rw_mu[0], rw_w0[0], rw_w2[0], rw_a0[0],
        rw_a2[0], rw_k_k[0], rw_k_a[0], rw_r_k[0], rw_gn_w[0], rw_gn_b[0], gd_conv_w[0],
        gd_A_log[0], gd_dt_bias[0], gd_o_norm_w[0], w_branch_a[0], w_branch_b[0], w_out[0])
    out = _merge(x2, ya, yb, gates, w_branch_a[0].astype(BF16), w_branch_b[0].astype(BF16),
                 w_out[0].astype(BF16), norm_out_w.reshape(1, D_MODEL))
    return out.reshape(batch, seq_len, d)
```

```python
import functools

import jax
import jax.numpy as jnp
from jax import lax
from jax.experimental import pallas as pl
from jax.experimental.pallas import tpu as pltpu

F32 = jnp.float32
BF16 = jnp.bfloat16

D_MODEL = 1024
RW_WIDTH = 512
RW_HEAD = 64
RW_RANK = 64
RW_GN_EPS = RW_HEAD * 1e-5
GD_WIDTH = 512
GD_HEAD = 128
GD_HEADS = GD_WIDTH // GD_HEAD
GD_CONV = 4
NORM_EPS = 1e-6
L2_EPS = 1e-12

LANES = 128
CHUNK = 64
N_PAIRS = RW_WIDTH // LANES
RW_SHIFT_COLS = 3 * RW_WIDTH + 2 * RW_RANK
C_ZRW = RW_SHIFT_COLS
C_QKV = C_ZRW + RW_WIDTH
C_ZGD = C_QKV + 3 * GD_WIDTH
C_GATES = C_ZGD + GD_WIDTH
C_BA = C_GATES + 2 * D_MODEL
PROJ_COLS = C_BA + LANES

PROJ_TM = 256
PREV_ROWS = 16
REC_TT = 256
MERGE_TM = 512
VMEM_LIMIT = 56 * 1024 * 1024


def _mm(a, b):
    return jnp.dot(a.astype(BF16), b.astype(BF16), preferred_element_type=F32)


def _mm_nt(a, b):
    return lax.dot_general(a.astype(BF16), b.astype(BF16), (((1,), (1,)), ((), ())),
                           preferred_element_type=F32)


def _mm_tn(a, b):
    return lax.dot_general(a.astype(BF16), b.astype(BF16), (((0,), (0,)), ((), ())),
                           preferred_element_type=F32)


def _sigmoid(x):
    return 1.0 / (1.0 + jnp.exp(-x))


def _silu(x):
    return x * _sigmoid(x)


def _softplus(x):
    return jnp.maximum(x, 0.0) + jnp.log(1.0 + jnp.exp(-jnp.abs(x)))


def _split3(x):
    hi = x.astype(BF16)
    r1 = x - hi.astype(F32)
    mid = r1.astype(BF16)
    lo = (r1 - mid.astype(F32)).astype(BF16)
    return hi, mid, lo


def _blk(idx, size):
    return lax.shift_right_logical(idx, size.bit_length() - 1)


def _tri_masks(n):
    ri = lax.broadcasted_iota(jnp.int32, (n, n), 0)
    ci = lax.broadcasted_iota(jnp.int32, (n, n), 1)
    same = _blk(ri, CHUNK) == _blk(ci, CHUNK)
    return ri, ci, same


def _unit_lower_inverses(lmats, ri, ci):
    eye = (ri == ci).astype(F32)
    blk8 = _blk(ri, 8) == _blk(ci, 8)
    ds = [jnp.where(blk8, l, 0.0) for l in lmats]
    d2s = [_mm(d, d) for d in ds]
    d4s = [_mm(d2, d2) for d2 in d2s]
    ts = [eye + d for d in ds]
    ts = [t + _mm(t, d2) for t, d2 in zip(ts, d2s)]
    ts = [t + _mm(t, d4) for t, d4 in zip(ts, d4s)]
    s = 8
    while s < CHUNK:
        off = (_blk(ri, 2 * s) == _blk(ci, 2 * s)) & (_blk(ri, s) != _blk(ci, s))
        xs = [_mm(jnp.where(off, l, 0.0), t) for l, t in zip(lmats, ts)]
        ts = [t + _mm(t, x) for t, x in zip(ts, xs)]
        s *= 2
    return ts


def _proj_kernel(xp_ref, x_ref, g_ref, w_ref, wbat_ref, mu_ref, cw_ref,
                 rkv_ref, wa_ref, zrw_ref, qkv_ref, zgd_ref, gates_ref, ba_ref, bat_ref,
                 *, tiles_per_seq):
    i = pl.program_id(0)
    tm = x_ref.shape[0]

    def normed(v):
        ms = jnp.mean(v * v, axis=-1, keepdims=True)
        return v * lax.rsqrt(ms + NORM_EPS) * g_ref[...]

    keep = (i % tiles_per_seq != 0).astype(F32)
    hp = (normed(xp_ref[...]) * keep).astype(BF16)
    hc = normed(x_ref[...]).astype(BF16)
    he = jnp.concatenate([hp, hc], axis=0)

    def proj(c0, c1):
        return jnp.dot(he, w_ref[:, c0:c1], preferred_element_type=F32)

    def shifted(pe, s):
        return pltpu.roll(pe, s, 0)[PREV_ROWS:]

    for c0 in range(0, RW_SHIFT_COLS, 512):
        c1 = min(c0 + 512, RW_SHIFT_COLS)
        pe = proj(c0, c1)
        cur = pe[PREV_ROWS:]
        xs = cur + (shifted(pe, 1) - cur) * mu_ref[:, c0:c1]
        for j in range((c1 - c0) // LANES):
            g = c0 // LANES + j
            blk = xs[:, j * LANES:(j + 1) * LANES].astype(BF16)
            if g < 3 * N_PAIRS:
                rkv_ref[g] = blk
            else:
                wa_ref[...] = blk

    pe = proj(C_ZRW, C_QKV)[PREV_ROWS:]
    for j in range(N_PAIRS):
        zrw_ref[j] = pe[:, j * LANES:(j + 1) * LANES].astype(BF16)

    for c0 in range(0, 3 * GD_WIDTH, 512):
        pe = proj(C_QKV + c0, C_QKV + c0 + 512)
        acc = pe[PREV_ROWS:] * cw_ref[GD_CONV - 1:GD_CONV, c0:c0 + 512]
        for s in range(1, GD_CONV):
            acc = acc + shifted(pe, s) * cw_ref[GD_CONV - 1 - s:GD_CONV - s, c0:c0 + 512]
        act = _silu(acc)
        for j in range(512 // LANES):
            qkv_ref[c0 // LANES + j] = act[:, j * LANES:(j + 1) * LANES].astype(BF16)

    pe = proj(C_ZGD, C_GATES)[PREV_ROWS:]
    for j in range(GD_HEADS):
        zgd_ref[j] = pe[:, j * LANES:(j + 1) * LANES].astype(BF16)

    for c0 in range(0, 2 * D_MODEL, 512):
        gates_ref[:, c0:c0 + 512] = proj(C_GATES + c0, C_GATES + c0 + 512)[PREV_ROWS:].astype(BF16)

    ba_ref[...] = proj(C_BA, PROJ_COLS)[PREV_ROWS:]
    bat_ref[...] = lax.dot_general(wbat_ref[...], hc, (((1,), (1,)), ((), ())),
                                   preferred_element_type=F32)


def _projection(x2, norm_w, w_perm, w_bat, mu, conv_w, seq_len):
    bt = x2.shape[0]
    tm = PROJ_TM
    n_tiles = bt // tm
    tiles_per_seq = seq_len // tm
    prev_blocks = tm // PREV_ROWS
    const = lambda i: (0, 0)
    out_shapes = (
        jax.ShapeDtypeStruct((3 * N_PAIRS, bt, LANES), BF16),
        jax.ShapeDtypeStruct((bt, LANES), BF16),
        jax.ShapeDtypeStruct((N_PAIRS, bt, LANES), BF16),
        jax.ShapeDtypeStruct((3 * GD_HEADS, bt, LANES), BF16),
        jax.ShapeDtypeStruct((GD_HEADS, bt, LANES), BF16),
        jax.ShapeDtypeStruct((bt, 2 * D_MODEL), BF16),
        jax.ShapeDtypeStruct((bt, LANES), F32),
        jax.ShapeDtypeStruct((8, bt), F32),
    )
    out_specs = (
        pl.BlockSpec((3 * N_PAIRS, tm, LANES), lambda i: (0, i, 0)),
        pl.BlockSpec((tm, LANES), lambda i: (i, 0)),
        pl.BlockSpec((N_PAIRS, tm, LANES), lambda i: (0, i, 0)),
        pl.BlockSpec((3 * GD_HEADS, tm, LANES), lambda i: (0, i, 0)),
        pl.BlockSpec((GD_HEADS, tm, LANES), lambda i: (0, i, 0)),
        pl.BlockSpec((tm, 2 * D_MODEL), lambda i: (i, 0)),
        pl.BlockSpec((tm, LANES), lambda i: (i, 0)),
        pl.BlockSpec((8, tm), lambda i: (0, i)),
    )
    in_specs = [
        pl.BlockSpec((PREV_ROWS, D_MODEL), lambda i: (jnp.maximum(i * prev_blocks - 1, 0), 0)),
        pl.BlockSpec((tm, D_MODEL), lambda i: (i, 0)),
        pl.BlockSpec((1, D_MODEL), const),
        pl.BlockSpec((D_MODEL, PROJ_COLS), const, pipeline_mode=pl.Buffered(1)),
        pl.BlockSpec((8, D_MODEL), const),
        pl.BlockSpec((1, RW_SHIFT_COLS), const),
        pl.BlockSpec((GD_CONV, 3 * GD_WIDTH), const),
    ]
    return pl.pallas_call(
        functools.partial(_proj_kernel, tiles_per_seq=tiles_per_seq),
        grid=(n_tiles,),
        in_specs=in_specs,
        out_specs=out_specs,
        out_shape=out_shapes,
        compiler_params=pltpu.CompilerParams(
            dimension_semantics=("arbitrary",), vmem_limit_bytes=VMEM_LIMIT),
        name="rmsnorm_in_proj",
    )(x2, x2, norm_w, w_perm, w_bat, mu, conv_w)


def _rwkv_kernel(rkv_ref, wa_ref, z_ref, ltri_ref, wda_ref, w0_ref, a0_ref,
                 kk_ref, ka_ref, rk_ref, gnw_ref, gnb_ref, y_ref, s_scr):
    t_idx = pl.program_id(1)

    @pl.when(t_idx == 0)
    def _():
        s_scr[...] = jnp.zeros_like(s_scr)

    tt = wa_ref.shape[0]
    n_chunks = tt // CHUNK
    lane = lax.broadcasted_iota(jnp.int32, (1, LANES), 1)
    head_a = lane < RW_HEAD
    ri, ci, same = _tri_masks(LANES)
    strict = same & (ci < ri)
    causal = same & (ci <= ri)
    seg = (_blk(ri, RW_HEAD) == _blk(ci, RW_HEAD)).astype(BF16)
    ltri = ltri_ref[...]

    def pair_rows(x):
        return jnp.concatenate([jnp.where(head_a, x, 0.0), jnp.where(head_a, 0.0, x)], axis=0)

    wa = wa_ref[...].astype(F32)
    da = jnp.where(lane < RW_RANK, jnp.tanh(wa), wa)
    pre = _mm(da, wda_ref[...])

    bonus, p_end, work = [], [], []
    for j in range(N_PAIRS):
        r = rkv_ref[j].astype(F32)
        k = rkv_ref[N_PAIRS + j].astype(F32)
        v = rkv_ref[2 * N_PAIRS + j].astype(F32)
        cols = slice(j * LANES, (j + 1) * LANES)
        log_w = -_softplus(-(w0_ref[j] + pre[:, cols])) - 0.5
        lw = -jnp.exp(log_w)
        lr = _sigmoid(a0_ref[j] + pre[:, RW_WIDTH + j * LANES:RW_WIDTH + (j + 1) * LANES])
        kkr = k * kk_ref[j]
        kk = kkr * lax.rsqrt(_mm(kkr * kkr, seg) + L2_EPS)
        k2 = k * (1.0 + (lr - 1.0) * ka_ref[j])
        b = kk * lr
        bonus.append(_mm(r * k2 * rk_ref[j], seg) * v)
        lw_hi = lw.astype(BF16)
        lw_lo = (lw - lw_hi.astype(F32)).astype(BF16)
        cl2 = _mm(ltri, jnp.concatenate([lw_hi, lw_lo], axis=1))
        clw = cl2[:, :LANES] + cl2[:, LANES:]
        e_neg = jnp.exp(-clw)
        rt = r * jnp.exp(clw)
        at = -kk * jnp.exp(clw - lw)
        bt = b * e_neg
        kt = k2 * e_neg
        for c in range(n_chunks):
            sl = slice(c * CHUNK, (c + 1) * CHUNK)
            cl_last = clw[(c + 1) * CHUNK - 1:(c + 1) * CHUNK, :]
            e_end = jnp.exp(cl_last - clw[sl])
            p_end.append(jnp.exp(cl_last))
            work.append(dict(
                at=pair_rows(at[sl]), rt=pair_rows(rt[sl]), bt=pair_rows(bt[sl]),
                kt=pair_rows(kt[sl]), v=pair_rows(v[sl]),
                bk_end=jnp.concatenate([pair_rows(b[sl] * e_end), pair_rows(k2[sl] * e_end)],
                                       axis=0)))

    n = len(work)
    for w in work:
        lhs = jnp.concatenate([w["at"], w["rt"]], axis=0)
        w["m_b"] = _mm_nt(lhs, w["bt"])
        w["m_k"] = _mm_nt(lhs, w["kt"])
    t_invs = _unit_lower_inverses([jnp.where(strict, w["m_b"][:LANES], 0.0) for w in work], ri, ci)
    akvs = [_mm(jnp.where(strict, w["m_k"][:LANES], 0.0), w["v"]) for w in work]
    tws = [_mm(t, jnp.concatenate([w["at"], akv], axis=1))
           for t, w, akv in zip(t_invs, work, akvs)]
    for w, tw in zip(work, tws):
        w["u0v"] = jnp.concatenate([tw[:, LANES:], w["v"]], axis=0)
        w["rbk"] = jnp.concatenate([jnp.where(causal, w["m_b"][LANES:], 0.0),
                                    jnp.where(causal, w["m_k"][LANES:], 0.0)], axis=1)
    trans = [_mm_tn(tw[:, :LANES], w["bk_end"][:LANES]) for tw, w in zip(tws, work)]
    offs = [_mm_tn(w["u0v"], w["bk_end"]) for w in work]
    qcs = [w["rt"] + _mm(w["rbk"][:, :LANES], tw[:, :LANES]) for w, tw in zip(work, tws)]
    y0s = [_mm(w["rbk"], w["u0v"]) for w in work]

    states = [None] * n
    for c in range(n_chunks):
        for j in range(N_PAIRS):
            i = j * n_chunks + c
            if c == 0:
                states[i] = s_scr[j]
            s0 = states[i]
            s1 = s0 * p_end[i] + _mm(s0, trans[i]) + offs[i]
            if c == n_chunks - 1:
                s_scr[j] = s1
            else:
                states[i + 1] = s1

    for j in range(N_PAIRS):
        ys = []
        for c in range(n_chunks):
            i = j * n_chunks + c
            y_p = _mm_nt(qcs[i], states[i]) + y0s[i]
            ys.append(y_p[:CHUNK] + y_p[CHUNK:])
        y = jnp.concatenate(ys, axis=0)
        inv_n = 1.0 / RW_HEAD
        yc = y - _mm(y, seg) * inv_n
        var = _mm(yc * yc, seg) * inv_n
        yn = yc * lax.rsqrt(var + RW_GN_EPS) * gnw_ref[j] + gnb_ref[j]
        y_ref[j] = ((yn + bonus[j]) * _silu(z_ref[j].astype(F32))).astype(y_ref.dtype)


def _rwkv_branch(rkv, wa, zrw, ltri, wda, w0, a0, k_k, k_a, r_k, gn_w, gn_b, batch, seq_len):
    tt = REC_TT
    nt = seq_len // tt
    bt = batch * seq_len
    tok = lambda n: pl.BlockSpec((n, tt, LANES), lambda b, t: (0, b * nt + t, 0))
    par = pl.BlockSpec((N_PAIRS, 1, LANES), lambda b, t: (0, 0, 0))
    in_specs = [
        tok(3 * N_PAIRS),
        pl.BlockSpec((tt, LANES), lambda b, t: (b * nt + t, 0)),
        tok(N_PAIRS),
        pl.BlockSpec((tt, tt), lambda b, t: (0, 0)),
        pl.BlockSpec((LANES, 2 * RW_WIDTH), lambda b, t: (0, 0)),
        par, par, par, par, par, par, par,
    ]
    return pl.pallas_call(
        _rwkv_kernel,
        grid=(batch, nt),
        in_specs=in_specs,
        out_specs=tok(N_PAIRS),
        out_shape=jax.ShapeDtypeStruct((N_PAIRS, bt, LANES), BF16),
        scratch_shapes=[pltpu.VMEM((N_PAIRS, LANES, LANES), F32)],
        compiler_params=pltpu.CompilerParams(
            dimension_semantics=("arbitrary", "arbitrary"), vmem_limit_bytes=VMEM_LIMIT),
        name="rwkv7_chunked",
    )(rkv, wa, zrw, ltri, wda, w0, a0, k_k, k_a, r_k, gn_w, gn_b)


def _gdn_kernel(qkv_ref, z_ref, ba_ref, bat_ref, ltri_ref, alog_ref, dtb_ref, onw_ref,
                y_ref, s_scr):
    t_idx = pl.program_id(1)

    @pl.when(t_idx == 0)
    def _():
        s_scr[...] = jnp.zeros_like(s_scr)

    tt = ba_ref.shape[0]
    n_super = tt // LANES
    per_super = LANES // CHUNK
    ones = jnp.ones((LANES, LANES), BF16)
    ri, ci, same = _tri_masks(LANES)
    strict = same & (ci < ri)
    causal = same & (ci <= ri)
    ltri = ltri_ref[...]
    ba = ba_ref[...]

    work = []
    for h in range(GD_HEADS):
        q = qkv_ref[h].astype(F32)
        k = qkv_ref[GD_HEADS + h].astype(F32)
        v = qkv_ref[2 * GD_HEADS + h].astype(F32)
        qn = q * (lax.rsqrt(_mm(q * q, ones) + L2_EPS) * (GD_HEAD ** -0.5))
        kn = k * lax.rsqrt(_mm(k * k, ones) + L2_EPS)
        beta = jnp.broadcast_to(_sigmoid(ba[:, h:h + 1]), (tt, LANES))
        neg_a = -jnp.exp(alog_ref[h])
        g_col = neg_a * _softplus(ba[:, GD_HEADS + h:GD_HEADS + h + 1] + dtb_ref[h])
        g_row = neg_a[:, :1] * _softplus(bat_ref[GD_HEADS + h:GD_HEADS + h + 1, :]
                                         + dtb_ref[h][:, :1])
        gc3 = _mm(ltri, jnp.concatenate(_split3(g_col), axis=1))
        gc_col = gc3[:, :LANES] + gc3[:, LANES:2 * LANES] + gc3[:, 2 * LANES:]
        gr3 = _mm_nt(jnp.concatenate(_split3(jnp.broadcast_to(g_row, (8, tt))), axis=0), ltri)
        gc_row = gr3[0:8] + gr3[8:16] + gr3[16:24]
        e_col = jnp.exp(gc_col)
        kb = kn * beta
        vb = v * beta
        qd = qn * e_col
        kbe = kb * e_col
        for sc in range(n_super):
            sl = slice(sc * LANES, (sc + 1) * LANES)
            diff = gc_col[sl] - gc_row[0:1, sl]
            chunks = []
            for cc in range(per_super):
                rows = slice(sc * LANES + cc * CHUNK, sc * LANES + (cc + 1) * CHUNK)
                g_last = gc_col[rows.stop - 1:rows.stop, :]
                chunks.append(dict(kte=kn[rows] * jnp.exp(g_last - gc_col[rows]),
                                   decay=jnp.exp(g_last)))
            work.append(dict(h=h, dec=jnp.exp(jnp.where(causal, diff, -1e30)),
                             kq=jnp.concatenate([kb[sl], qn[sl]], axis=0), kn=kn[sl],
                             rhs=jnp.concatenate([vb[sl], kbe[sl]], axis=1), qd=qd[sl],
                             chunks=chunks))

    for w in work:
        w["m"] = _mm_nt(w["kq"], w["kn"])
    t_invs = _unit_lower_inverses(
        [-jnp.where(strict, w["m"][:LANES] * w["dec"], 0.0) for w in work], ri, ci)
    uws = [_mm(t, w["rhs"]) for t, w in zip(t_invs, work)]
    qws = [_mm(w["m"][LANES:] * w["dec"], uw) for w, uw in zip(work, uws)]
    for w, uw, qw in zip(work, uws, qws):
        w["qc"] = w["qd"] - qw[:, LANES:]
        w["o0"] = qw[:, :LANES]
        for cc, ch in enumerate(w["chunks"]):
            ch["uw"] = uw[cc * CHUNK:(cc + 1) * CHUNK]
    for w in work:
        for ch in w["chunks"]:
            ch["tn"] = _mm_tn(ch["kte"], ch["uw"])

    state = [s_scr[h] for h in range(GD_HEADS)]
    for sc in range(n_super):
        for cc in range(per_super):
            for h in range(GD_HEADS):
                ch = work[h * n_super + sc]["chunks"][cc]
                s0 = state[h]
                ch["s0"] = s0
                state[h] = s0 * ch["decay"] - _mm(ch["tn"][:, LANES:], s0) + ch["tn"][:, :LANES]
    for h in range(GD_HEADS):
        s_scr[h] = state[h]

    for h in range(GD_HEADS):
        os = []
        for sc in range(n_super):
            w = work[h * n_super + sc]
            for cc, ch in enumerate(w["chunks"]):
                rows = slice(cc * CHUNK, (cc + 1) * CHUNK)
                os.append(_mm(w["qc"][rows], ch["s0"]) + w["o0"][rows])
        o = jnp.concatenate(os, axis=0)
        ms = _mm(o * o, ones) * (1.0 / GD_HEAD)
        on = o * lax.rsqrt(ms + NORM_EPS) * onw_ref[...]
        y_ref[h] = (on * _silu(z_ref[h].astype(F32))).astype(y_ref.dtype)


def _gdn_branch(qkv, zgd, ba, bat, ltri, alog, dtb, onw, batch, seq_len):
    tt = REC_TT
    nt = seq_len // tt
    bt = batch * seq_len
    tok = lambda n: pl.BlockSpec((n, tt, LANES), lambda b, t: (0, b * nt + t, 0))
    par = pl.BlockSpec((GD_HEADS, 1, LANES), lambda b, t: (0, 0, 0))
    in_specs = [
        tok(3 * GD_HEADS),
        tok(GD_HEADS),
        pl.BlockSpec((tt, LANES), lambda b, t: (b * nt + t, 0)),
        pl.BlockSpec((8, tt), lambda b, t: (0, b * nt + t)),
        pl.BlockSpec((tt, tt), lambda b, t: (0, 0)),
        par, par,
        pl.BlockSpec((1, LANES), lambda b, t: (0, 0)),
    ]
    return pl.pallas_call(
        _gdn_kernel,
        grid=(batch, nt),
        in_specs=in_specs,
        out_specs=tok(GD_HEADS),
        out_shape=jax.ShapeDtypeStruct((GD_HEADS, bt, LANES), BF16),
        scratch_shapes=[pltpu.VMEM((GD_HEADS, LANES, LANES), F32)],
        compiler_params=pltpu.CompilerParams(
            dimension_semantics=("arbitrary", "arbitrary"), vmem_limit_bytes=VMEM_LIMIT),
        name="gated_delta_chunked",
    )(qkv, zgd, ba, bat, ltri, alog, dtb, onw)


def _merge_kernel(x_ref, ya_ref, yb_ref, gates_ref, wa_ref, wb_ref, wo_ref, g_ref, o_ref):
    ya = jnp.concatenate([ya_ref[j] for j in range(N_PAIRS)], axis=1)
    yb = jnp.concatenate([yb_ref[j] for j in range(GD_HEADS)], axis=1)
    ma = jnp.dot(ya, wa_ref[...], preferred_element_type=F32)
    mb = jnp.dot(yb, wb_ref[...], preferred_element_type=F32)
    gates = gates_ref[...].astype(F32)
    merged = _sigmoid(gates[:, :D_MODEL]) * ma + _sigmoid(gates[:, D_MODEL:]) * mb
    res = x_ref[...] + jnp.dot(merged.astype(BF16), wo_ref[...], preferred_element_type=F32)
    ms = jnp.mean(res * res, axis=-1, keepdims=True)
    o_ref[...] = res * lax.rsqrt(ms + NORM_EPS) * g_ref[...]


def _merge(x2, ya, yb, gates, w_a, w_b, w_o, norm_w):
    bt = x2.shape[0]
    tm = MERGE_TM
    const = lambda i: (0, 0)
    in_specs = [
        pl.BlockSpec((tm, D_MODEL), lambda i: (i, 0)),
        pl.BlockSpec((N_PAIRS, tm, LANES), lambda i: (0, i, 0)),
        pl.BlockSpec((GD_HEADS, tm, LANES), lambda i: (0, i, 0)),
        pl.BlockSpec((tm, 2 * D_MODEL), lambda i: (i, 0)),
        pl.BlockSpec((RW_WIDTH, D_MODEL), const),
        pl.BlockSpec((GD_WIDTH, D_MODEL), const),
        pl.BlockSpec((D_MODEL, D_MODEL), const),
        pl.BlockSpec((1, D_MODEL), const),
    ]
    return pl.pallas_call(
        _merge_kernel,
        grid=(bt // tm,),
        in_specs=in_specs,
        out_specs=pl.BlockSpec((tm, D_MODEL), lambda i: (i, 0)),
        out_shape=jax.ShapeDtypeStruct((bt, D_MODEL), F32),
        compiler_params=pltpu.CompilerParams(
            dimension_semantics=("arbitrary",), vmem_limit_bytes=VMEM_LIMIT),
        name="merge_out_proj",
    )(x2, ya, yb, gates, w_a, w_b, w_o, norm_w)


def _block_lower_tri(n):
    i = jnp.arange(n)
    return (((i[:, None] // CHUNK) == (i[None, :] // CHUNK)) & (i[None, :] <= i[:, None])).astype(BF16)


def _layer(x2, batch, seq_len, norm_in_w, w_in, rw_mu, rw_w0, rw_w2, rw_a0, rw_a2, rw_k_k,
           rw_k_a, rw_r_k, rw_gn_w, rw_gn_b, gd_conv_w, gd_A_log, gd_dt_bias, gd_o_norm_w,
           w_branch_a, w_branch_b, w_out):
    c_beta = C_ZGD + GD_WIDTH
    c_gates = c_beta + 2 * GD_HEADS
    w_perm = jnp.concatenate(
        [w_in[:, :c_beta], w_in[:, c_gates:], w_in[:, c_beta:c_gates],
         jnp.zeros((D_MODEL, LANES - 2 * GD_HEADS), w_in.dtype)], axis=1).astype(BF16)
    w_bat = w_in[:, c_beta:c_gates].T.astype(BF16)

    rkv, wa, zrw, qkv, zgd, gates, ba, bat = _projection(
        x2, norm_in_w.reshape(1, D_MODEL), w_perm, w_bat,
        rw_mu.reshape(1, RW_SHIFT_COLS), gd_conv_w, seq_len)

    ltri = _block_lower_tri(REC_TT)
    pair = lambda p: p.reshape(N_PAIRS, 1, LANES)
    zero = jnp.zeros((RW_RANK, RW_WIDTH), F32)
    wda = jnp.concatenate([jnp.concatenate([rw_w2, zero], axis=1),
                           jnp.concatenate([zero, rw_a2], axis=1)], axis=0).astype(BF16)
    ya = _rwkv_branch(rkv, wa, zrw, ltri, wda, pair(rw_w0), pair(rw_a0), pair(rw_k_k),
                      pair(rw_k_a), pair(rw_r_k.reshape(RW_WIDTH)), pair(rw_gn_w),
                      pair(rw_gn_b), batch, seq_len)

    head = lambda p: jnp.broadcast_to(p.reshape(GD_HEADS, 1, 1), (GD_HEADS, 1, LANES))
    yb = _gdn_branch(qkv, zgd, ba, bat, ltri, head(gd_A_log), head(gd_dt_bias),
                     gd_o_norm_w.reshape(1, GD_HEAD), batch, seq_len)
    return gates, ya, yb


def kernel(x, norm_in_w, w_in, rw_mu, rw_w0, rw_w2, rw_a0, rw_a2, rw_k_k, rw_k_a, rw_r_k,
           rw_gn_w, rw_gn_b, gd_conv_w, gd_A_log, gd_dt_bias, gd_o_norm_w,
           w_branch_a, w_branch_b, w_out, norm_out_w):
    batch, seq_len, d = x.shape
    assert d == D_MODEL and norm_in_w.shape[0] == 1
    assert seq_len % max(PROJ_TM, REC_TT) == 0 and (batch * seq_len) % MERGE_TM == 0
    x2 = x.reshape(batch * seq_len, d)
    gates, ya, yb = _layer(
        x2, batch, seq_len, norm_in_w[0], w_in[0], rw_mu[0], rw_w0[0], rw_w2[0], rw_a0[0],
        rw_a2[0], rw_k_k[0], rw_k_a[0], rw_r_k[0], rw_gn_w[0], rw_gn_b[0], gd_conv_w[0],
        gd_A_log[0], gd_dt_bias[0], gd_o_norm_w[0], w_branch_a[0], w_branch_b[0], w_out[0])
    out = _merge(x2, ya, yb, gates, w_branch_a[0].astype(BF16), w_branch_b[0].astype(BF16),
                 w_out[0].astype(BF16), norm_out_w.reshape(1, D_MODEL))
    return out.reshape(batch, seq_len, d)
```

```python
import jax
import jax.numpy as jnp
from jax import lax
from jax.experimental import pallas as pl
from jax.experimental.pallas import tpu as pltpu

F32 = jnp.float32
BF16 = jnp.bfloat16

D_MODEL = 1024
RW_WIDTH = 512
RW_HEAD = 64
RW_RANK = 64
RW_GN_EPS = RW_HEAD * 1e-5
GD_WIDTH = 512
GD_HEAD = 128
GD_HEADS = GD_WIDTH // GD_HEAD
GD_CONV = 4
NORM_EPS = 1e-6
L2_EPS = 1e-12

LANES = 128
CHUNK = 64
N_PAIRS = RW_WIDTH // LANES
RW_SHIFT_COLS = 3 * RW_WIDTH + 2 * RW_RANK
C_ZRW = RW_SHIFT_COLS
C_QKV = C_ZRW + RW_WIDTH
C_ZGD = C_QKV + 3 * GD_WIDTH
C_GATES = C_ZGD + GD_WIDTH
C_BA = C_GATES + 2 * D_MODEL
PROJ_COLS = C_BA + LANES

PROJ_TM = 512
CARRY_ROWS = 8
REC_TT = 512
MERGE_TM = 512
VMEM_LIMIT = 56 * 1024 * 1024


def _mm(a, b):
    return jnp.dot(a.astype(BF16), b.astype(BF16), preferred_element_type=F32)


def _mm_nt(a, b):
    return lax.dot_general(a.astype(BF16), b.astype(BF16), (((1,), (1,)), ((), ())),
                           preferred_element_type=F32)


def _mm_tn(a, b):
    return lax.dot_general(a.astype(BF16), b.astype(BF16), (((0,), (0,)), ((), ())),
                           preferred_element_type=F32)


def _sigmoid(x):
    return 1.0 / (1.0 + jnp.exp(-x))


def _silu(x):
    return x * _sigmoid(x)


def _softplus(x):
    return jnp.maximum(x, 0.0) + jnp.log(1.0 + jnp.exp(-jnp.abs(x)))


def _split3(x):
    hi = x.astype(BF16)
    r1 = x - hi.astype(F32)
    mid = r1.astype(BF16)
    lo = (r1 - mid.astype(F32)).astype(BF16)
    return hi, mid, lo


def _blk(idx, size):
    return lax.shift_right_logical(idx, size.bit_length() - 1)


def _tri_masks(n):
    ri = lax.broadcasted_iota(jnp.int32, (n, n), 0)
    ci = lax.broadcasted_iota(jnp.int32, (n, n), 1)
    same = _blk(ri, CHUNK) == _blk(ci, CHUNK)
    return ri, ci, same


def _unit_lower_inverses(lmats, ri, ci):
    eye = (ri == ci).astype(F32)
    blk8 = _blk(ri, 8) == _blk(ci, 8)
    n = lmats[0].shape[0]
    ds = [jnp.where(blk8, l, 0.0) for l in lmats]
    d2s = [_mm(d, d) for d in ds]
    ts = [eye + d for d in ds]
    xs = [_mm(d2, jnp.concatenate([t, d2], axis=1)) for t, d2 in zip(ts, d2s)]
    ts = [t + x[:, :n] for t, x in zip(ts, xs)]
    ts = [t + _mm(x[:, n:], t) for t, x in zip(ts, xs)]
    s = 8
    while s < CHUNK:
        off = (_blk(ri, 2 * s) == _blk(ci, 2 * s)) & (_blk(ri, s) != _blk(ci, s))
        xs = [_mm(jnp.where(off, l, 0.0), t) for l, t in zip(lmats, ts)]
        ts = [t + _mm(t, x) for t, x in zip(ts, xs)]
        s *= 2
    return ts


def _shifted_rows(carry, cur, s):
    return pltpu.roll(jnp.concatenate([carry, cur], axis=0), s, 0)[CARRY_ROWS:]


def _proj_kernel(x_ref, w_ref, prw_ref, zrw_ref, pqkv_ref, zgd_ref, gates_ref, ba_ref):
    xc = x_ref[...]
    inv = lax.rsqrt(jnp.mean(xc * xc, axis=-1, keepdims=True) + NORM_EPS)
    xb = xc.astype(BF16)

    def proj(c0, c1):
        return jnp.dot(xb, w_ref[:, c0:c1], preferred_element_type=F32) * inv

    def lane_groups(c0, n_groups, out_ref):
        for g0 in range(0, n_groups, 4):
            ng = min(4, n_groups - g0)
            pe = proj(c0 + g0 * LANES, c0 + (g0 + ng) * LANES)
            for j in range(ng):
                out_ref[g0 + j] = pe[:, j * LANES:(j + 1) * LANES].astype(BF16)

    lane_groups(0, RW_SHIFT_COLS // LANES, prw_ref)
    lane_groups(C_ZRW, N_PAIRS, zrw_ref)
    lane_groups(C_QKV, 3 * GD_HEADS, pqkv_ref)
    lane_groups(C_ZGD, GD_HEADS, zgd_ref)
    for c0 in range(0, 2 * D_MODEL, 512):
        gates_ref[:, c0:c0 + 512] = proj(C_GATES + c0, C_GATES + c0 + 512).astype(BF16)
    ba_ref[...] = proj(C_BA, PROJ_COLS)


def _projection(x2, w_perm):
    bt = x2.shape[0]
    tm = PROJ_TM
    const = lambda i: (0, 0)
    slabs = lambda n: pl.BlockSpec((n, tm, LANES), lambda i: (0, i, 0))
    n_rw = RW_SHIFT_COLS // LANES
    out_shapes = (
        jax.ShapeDtypeStruct((n_rw, bt, LANES), BF16),
        jax.ShapeDtypeStruct((N_PAIRS, bt, LANES), BF16),
        jax.ShapeDtypeStruct((3 * GD_HEADS, bt, LANES), BF16),
        jax.ShapeDtypeStruct((GD_HEADS, bt, LANES), BF16),
        jax.ShapeDtypeStruct((bt, 2 * D_MODEL), BF16),
        jax.ShapeDtypeStruct((bt, LANES), F32),
    )
    out_specs = (
        slabs(n_rw), slabs(N_PAIRS), slabs(3 * GD_HEADS), slabs(GD_HEADS),
        pl.BlockSpec((tm, 2 * D_MODEL), lambda i: (i, 0)),
        pl.BlockSpec((tm, LANES), lambda i: (i, 0)),
    )
    in_specs = [
        pl.BlockSpec((tm, D_MODEL), lambda i: (i, 0)),
        pl.BlockSpec((D_MODEL, PROJ_COLS), const, pipeline_mode=pl.Buffered(1)),
    ]
    return pl.pallas_call(
        _proj_kernel,
        grid=(bt // tm,),
        in_specs=in_specs,
        out_specs=out_specs,
        out_shape=out_shapes,
        compiler_params=pltpu.CompilerParams(
            dimension_semantics=("arbitrary",), vmem_limit_bytes=VMEM_LIMIT),
        name="rmsnorm_in_proj",
    )(x2, w_perm)


def _rwkv_kernel(prw_ref, z_ref, ltri_ref, wda_ref, mu_ref, w0_ref, a0_ref,
                 kk_ref, ka_ref, rk_ref, gnw_ref, gnb_ref, y_ref, s_scr, carry_scr):
    t_idx = pl.program_id(1)

    @pl.when(t_idx == 0)
    def _():
        s_scr[...] = jnp.zeros_like(s_scr)
        carry_scr[...] = jnp.zeros_like(carry_scr)

    tt = z_ref.shape[1]
    n_chunks = tt // CHUNK

    def token_shift(g):
        cur = prw_ref[g].astype(F32)
        prev = _shifted_rows(carry_scr[g], cur, 1)
        carry_scr[g] = cur[tt - CARRY_ROWS:]
        return cur + (prev - cur) * mu_ref[g]

    lane = lax.broadcasted_iota(jnp.int32, (1, LANES), 1)
    head_a = lane < RW_HEAD
    ri, ci, same = _tri_masks(LANES)
    strict = same & (ci < ri)
    causal = same & (ci <= ri)
    seg = (_blk(ri, RW_HEAD) == _blk(ci, RW_HEAD)).astype(BF16)
    ltri = ltri_ref[...]

    def pair_rows(x):
        return jnp.concatenate([jnp.where(head_a, x, 0.0), jnp.where(head_a, 0.0, x)], axis=0)

    wa = token_shift(3 * N_PAIRS)
    da = jnp.where(lane < RW_RANK, jnp.tanh(wa), wa)
    pre = _mm(da, wda_ref[...])

    bonus, p_end, work = [], [], []
    for j in range(N_PAIRS):
        r = token_shift(j)
        k = token_shift(N_PAIRS + j)
        v = token_shift(2 * N_PAIRS + j)
        cols = slice(j * LANES, (j + 1) * LANES)
        log_w = -_softplus(-(w0_ref[j] + pre[:, cols])) - 0.5
        lw = -jnp.exp(log_w)
        lr = _sigmoid(a0_ref[j] + pre[:, RW_WIDTH + j * LANES:RW_WIDTH + (j + 1) * LANES])
        kkr = k * kk_ref[j]
        kk = kkr * lax.rsqrt(_mm(kkr * kkr, seg) + L2_EPS)
        k2 = k * (1.0 + (lr - 1.0) * ka_ref[j])
        b = kk * lr
        bonus.append(_mm(r * k2 * rk_ref[j], seg) * v)
        lw_hi = lw.astype(BF16)
        lw_lo = (lw - lw_hi.astype(F32)).astype(BF16)
        cl2 = _mm(ltri, jnp.concatenate([lw_hi, lw_lo], axis=1))
        clw = cl2[:, :LANES] + cl2[:, LANES:]
        e_neg = jnp.exp(-clw)
        rt = r * jnp.exp(clw)
        at = -kk * jnp.exp(clw - lw)
        bt = b * e_neg
        kt = k2 * e_neg
        for c in range(n_chunks):
            sl = slice(c * CHUNK, (c + 1) * CHUNK)
            cl_last = clw[(c + 1) * CHUNK - 1:(c + 1) * CHUNK, :]
            e_end = jnp.exp(cl_last - clw[sl])
            p_end.append(jnp.sum(jnp.where(ri == ci, jnp.exp(cl_last), 0.0), axis=-1,
                                 keepdims=True))
            work.append(dict(
                ar=jnp.concatenate([pair_rows(at[sl]), pair_rows(rt[sl])], axis=0),
                bk=jnp.concatenate([pair_rows(bt[sl]), pair_rows(kt[sl])], axis=0),
                v=pair_rows(v[sl]),
                bk_end=jnp.concatenate([pair_rows(b[sl] * e_end), pair_rows(k2[sl] * e_end)],
                                       axis=0)))

    n = len(work)
    strict2 = jnp.concatenate([strict, strict], axis=1)
    causal2 = jnp.concatenate([causal, causal], axis=1)
    zeros = jnp.zeros((LANES, LANES), F32)
    ms = [_mm_nt(w["ar"], w["bk"]) for w in work]
    lks = [jnp.where(strict2, m[:LANES], 0.0) for m in ms]
    t_invs = _unit_lower_inverses([lk[:, :LANES] for lk in lks], ri, ci)
    akvs = [_mm(lk[:, LANES:], w["v"]) for lk, w in zip(lks, work)]
    tws = [_mm(t, jnp.concatenate([w["ar"][:LANES], akv], axis=1))
           for t, w, akv in zip(t_invs, work, akvs)]
    xs = [jnp.concatenate([tw, jnp.concatenate([zeros, w["v"]], axis=1)], axis=0)
          for tw, w in zip(tws, work)]
    qys = [_mm(jnp.where(causal2, m[LANES:], 0.0), x) for m, x in zip(ms, xs)]
    zs = [_mm_tn(w["bk_end"], x) for w, x in zip(work, xs)]

    states = [None] * n
    for c in range(n_chunks):
        for j in range(N_PAIRS):
            i = j * n_chunks + c
            if c == 0:
                states[i] = s_scr[j]
            h0 = states[i]
            h1 = p_end[i] * h0 + _mm(zs[i][:, :LANES], h0) + zs[i][:, LANES:]
            if c == n_chunks - 1:
                s_scr[j] = h1
            else:
                states[i + 1] = h1

    for j in range(N_PAIRS):
        ys = []
        for c in range(n_chunks):
            i = j * n_chunks + c
            qc = work[i]["ar"][LANES:] + qys[i][:, :LANES]
            y_p = _mm(qc, states[i]) + qys[i][:, LANES:]
            ys.append(y_p[:CHUNK] + y_p[CHUNK:])
        y = jnp.concatenate(ys, axis=0)
        inv_n = 1.0 / RW_HEAD
        yc = y - _mm(y, seg) * inv_n
        var = _mm(yc * yc, seg) * inv_n
        yn = yc * lax.rsqrt(var + RW_GN_EPS) * gnw_ref[j] + gnb_ref[j]
        y_ref[j] = ((yn + bonus[j]) * _silu(z_ref[j].astype(F32))).astype(y_ref.dtype)


def _rwkv_branch(prw, zrw, ltri, wda, mu, w0, a0, k_k, k_a, r_k, gn_w, gn_b, batch, seq_len):
    tt = REC_TT
    nt = seq_len // tt
    bt = batch * seq_len
    tok = lambda n: pl.BlockSpec((n, tt, LANES), lambda b, t: (0, b * nt + t, 0))
    par = pl.BlockSpec((N_PAIRS, 1, LANES), lambda b, t: (0, 0, 0))
    n_rw = RW_SHIFT_COLS // LANES
    in_specs = [
        tok(n_rw),
        tok(N_PAIRS),
        pl.BlockSpec((tt, tt), lambda b, t: (0, 0)),
        pl.BlockSpec((LANES, 2 * RW_WIDTH), lambda b, t: (0, 0)),
        pl.BlockSpec((n_rw, 1, LANES), lambda b, t: (0, 0, 0)),
        par, par, par, par, par, par, par,
    ]
    return pl.pallas_call(
        _rwkv_kernel,
        grid=(batch, nt),
        in_specs=in_specs,
        out_specs=tok(N_PAIRS),
        out_shape=jax.ShapeDtypeStruct((N_PAIRS, bt, LANES), BF16),
        scratch_shapes=[pltpu.VMEM((N_PAIRS, LANES, LANES), F32),
                        pltpu.VMEM((n_rw, CARRY_ROWS, LANES), F32)],
        compiler_params=pltpu.CompilerParams(
            dimension_semantics=("arbitrary", "arbitrary"), vmem_limit_bytes=VMEM_LIMIT),
        name="rwkv7_chunked",
    )(prw, zrw, ltri, wda, mu, w0, a0, k_k, k_a, r_k, gn_w, gn_b)


def _gdn_kernel(pqkv_ref, z_ref, ba_ref, ltri_ref, cw_ref, alog_ref, dtb_ref, onw_ref,
                y_ref, s_scr, carry_scr):
    t_idx = pl.program_id(1)

    @pl.when(t_idx == 0)
    def _():
        s_scr[...] = jnp.zeros_like(s_scr)
        carry_scr[...] = jnp.zeros_like(carry_scr)

    tt = ba_ref.shape[0]

    def conv_silu(g):
        cur = pqkv_ref[g].astype(F32)
        carry = carry_scr[g]
        taps = cw_ref[g]
        acc = cur * taps[GD_CONV - 1:GD_CONV]
        for s in range(1, GD_CONV):
            acc = acc + _shifted_rows(carry, cur, s) * taps[GD_CONV - 1 - s:GD_CONV - s]
        carry_scr[g] = cur[tt - CARRY_ROWS:]
        return _silu(acc)

    n_super = tt // LANES
    per_super = LANES // CHUNK
    ones = jnp.ones((LANES, LANES), BF16)
    ri, ci, same = _tri_masks(LANES)
    strict = same & (ci < ri)
    causal = same & (ci <= ri)
    ltri = ltri_ref[...]
    ba = ba_ref[...]
    ba_t = ba.T

    work = []
    for h in range(GD_HEADS):
        q = conv_silu(h)
        k = conv_silu(GD_HEADS + h)
        v = conv_silu(2 * GD_HEADS + h)
        qn = q * (lax.rsqrt(_mm(q * q, ones) + L2_EPS) * (GD_HEAD ** -0.5))
        kn = k * lax.rsqrt(_mm(k * k, ones) + L2_EPS)
        beta = jnp.broadcast_to(_sigmoid(ba[:, h:h + 1]), (tt, LANES))
        neg_a = -jnp.exp(alog_ref[h])
        g_col = neg_a * _softplus(ba[:, GD_HEADS + h:GD_HEADS + h + 1] + dtb_ref[h])
        g_row = neg_a[:, :1] * _softplus(ba_t[GD_HEADS + h:GD_HEADS + h + 1, :]
                                         + dtb_ref[h][:, :1])
        gc3 = _mm(ltri, jnp.concatenate(_split3(g_col), axis=1))
        gc_col = gc3[:, :LANES] + gc3[:, LANES:2 * LANES] + gc3[:, 2 * LANES:]
        gr3 = _mm_nt(jnp.concatenate(_split3(jnp.broadcast_to(g_row, (8, tt))), axis=0), ltri)
        gc_row = gr3[0:8] + gr3[8:16] + gr3[16:24]
        e_col = jnp.exp(gc_col)
        kb = kn * beta
        vb = v * beta
        qd = qn * e_col
        kbe = kb * e_col
        for sc in range(n_super):
            sl = slice(sc * LANES, (sc + 1) * LANES)
            diff = gc_col[sl] - gc_row[0:1, sl]
            chunks = []
            for cc in range(per_super):
                rows = slice(sc * LANES + cc * CHUNK, sc * LANES + (cc + 1) * CHUNK)
                g_last = gc_col[rows.stop - 1:rows.stop, :]
                chunks.append(dict(kte=kn[rows] * jnp.exp(g_last - gc_col[rows]),
                                   decay=jnp.exp(g_last)))
            work.append(dict(h=h, dec=jnp.exp(jnp.where(causal, diff, -1e30)),
                             kq=jnp.concatenate([kb[sl], qn[sl]], axis=0), kn=kn[sl],
                             rhs=jnp.concatenate([vb[sl], kbe[sl]], axis=1), qd=qd[sl],
                             chunks=chunks))

    for w in work:
        w["m"] = _mm_nt(w["kq"], w["kn"])
    t_invs = _unit_lower_inverses(
        [-jnp.where(strict, w["m"][:LANES] * w["dec"], 0.0) for w in work], ri, ci)
    uws = [_mm(t, w["rhs"]) for t, w in zip(t_invs, work)]
    qws = [_mm(w["m"][LANES:] * w["dec"], uw) for w, uw in zip(work, uws)]
    for w, uw, qw in zip(work, uws, qws):
        w["qc"] = w["qd"] - qw[:, LANES:]
        w["o0"] = qw[:, :LANES]
        for cc, ch in enumerate(w["chunks"]):
            ch["uw"] = uw[cc * CHUNK:(cc + 1) * CHUNK]
    for w in work:
        for ch in w["chunks"]:
            ch["tn"] = _mm_tn(ch["kte"], ch["uw"])

    state = [s_scr[h] for h in range(GD_HEADS)]
    for sc in range(n_super):
        for cc in range(per_super):
            for h in range(GD_HEADS):
                ch = work[h * n_super + sc]["chunks"][cc]
                s0 = state[h]
                ch["s0"] = s0
                state[h] = s0 * ch["decay"] - _mm(ch["tn"][:, LANES:], s0) + ch["tn"][:, :LANES]
    for h in range(GD_HEADS):
        s_scr[h] = state[h]

    for h in range(GD_HEADS):
        os = []
        for sc in range(n_super):
            w = work[h * n_super + sc]
            for cc, ch in enumerate(w["chunks"]):
                rows = slice(cc * CHUNK, (cc + 1) * CHUNK)
                os.append(_mm(w["qc"][rows], ch["s0"]) + w["o0"][rows])
        o = jnp.concatenate(os, axis=0)
        ms = _mm(o * o, ones) * (1.0 / GD_HEAD)
        on = o * lax.rsqrt(ms + NORM_EPS) * onw_ref[...]
        y_ref[h] = (on * _silu(z_ref[h].astype(F32))).astype(y_ref.dtype)


def _gdn_branch(pqkv, zgd, ba, ltri, conv_w, alog, dtb, onw, batch, seq_len):
    tt = REC_TT
    nt = seq_len // tt
    bt = batch * seq_len
    tok = lambda n: pl.BlockSpec((n, tt, LANES), lambda b, t: (0, b * nt + t, 0))
    par = pl.BlockSpec((GD_HEADS, 1, LANES), lambda b, t: (0, 0, 0))
    in_specs = [
        tok(3 * GD_HEADS),
        tok(GD_HEADS),
        pl.BlockSpec((tt, LANES), lambda b, t: (b * nt + t, 0)),
        pl.BlockSpec((tt, tt), lambda b, t: (0, 0)),
        pl.BlockSpec((3 * GD_HEADS, GD_CONV, LANES), lambda b, t: (0, 0, 0)),
        par, par,
        pl.BlockSpec((1, LANES), lambda b, t: (0, 0)),
    ]
    return pl.pallas_call(
        _gdn_kernel,
        grid=(batch, nt),
        in_specs=in_specs,
        out_specs=tok(GD_HEADS),
        out_shape=jax.ShapeDtypeStruct((GD_HEADS, bt, LANES), BF16),
        scratch_shapes=[pltpu.VMEM((GD_HEADS, LANES, LANES), F32),
                        pltpu.VMEM((3 * GD_HEADS, CARRY_ROWS, LANES), F32)],
        compiler_params=pltpu.CompilerParams(
            dimension_semantics=("arbitrary", "arbitrary"), vmem_limit_bytes=VMEM_LIMIT),
        name="gated_delta_chunked",
    )(pqkv, zgd, ba, ltri, conv_w, alog, dtb, onw)


def _merge_kernel(x_ref, ya_ref, yb_ref, gates_ref, wa_ref, wb_ref, wo_ref, g_ref, o_ref):
    ya = jnp.concatenate([ya_ref[j] for j in range(N_PAIRS)], axis=1)
    yb = jnp.concatenate([yb_ref[j] for j in range(GD_HEADS)], axis=1)
    ma = jnp.dot(ya, wa_ref[...], preferred_element_type=F32)
    mb = jnp.dot(yb, wb_ref[...], preferred_element_type=F32)
    gates = gates_ref[...].astype(F32)
    merged = _sigmoid(gates[:, :D_MODEL]) * ma + _sigmoid(gates[:, D_MODEL:]) * mb
    res = x_ref[...] + jnp.dot(merged.astype(BF16), wo_ref[...], preferred_element_type=F32)
    ms = jnp.mean(res * res, axis=-1, keepdims=True)
    o_ref[...] = res * lax.rsqrt(ms + NORM_EPS) * g_ref[...]


def _merge(x2, ya, yb, gates, w_a, w_b, w_o, norm_w):
    bt = x2.shape[0]
    tm = MERGE_TM
    const = lambda i: (0, 0)
    in_specs = [
        pl.BlockSpec((tm, D_MODEL), lambda i: (i, 0)),
        pl.BlockSpec((N_PAIRS, tm, LANES), lambda i: (0, i, 0)),
        pl.BlockSpec((GD_HEADS, tm, LANES), lambda i: (0, i, 0)),
        pl.BlockSpec((tm, 2 * D_MODEL), lambda i: (i, 0)),
        pl.BlockSpec((RW_WIDTH, D_MODEL), const),
        pl.BlockSpec((GD_WIDTH, D_MODEL), const),
        pl.BlockSpec((D_MODEL, D_MODEL), const),
        pl.BlockSpec((1, D_MODEL), const),
    ]
    return pl.pallas_call(
        _merge_kernel,
        grid=(bt // tm,),
        in_specs=in_specs,
        out_specs=pl.BlockSpec((tm, D_MODEL), lambda i: (i, 0)),
        out_shape=jax.ShapeDtypeStruct((bt, D_MODEL), F32),
        compiler_params=pltpu.CompilerParams(
            dimension_semantics=("arbitrary",), vmem_limit_bytes=VMEM_LIMIT),
        name="merge_out_proj",
    )(x2, ya, yb, gates, w_a, w_b, w_o, norm_w)


def _block_lower_tri(n):
    i = jnp.arange(n)
    return (((i[:, None] // CHUNK) == (i[None, :] // CHUNK)) & (i[None, :] <= i[:, None])).astype(BF16)


def _layer(x2, batch, seq_len, norm_in_w, w_in, rw_mu, rw_w0, rw_w2, rw_a0, rw_a2, rw_k_k,
           rw_k_a, rw_r_k, rw_gn_w, rw_gn_b, gd_conv_w, gd_A_log, gd_dt_bias, gd_o_norm_w,
           w_branch_a, w_branch_b, w_out):
    c_beta = C_ZGD + GD_WIDTH
    c_gates = c_beta + 2 * GD_HEADS
    w_perm = jnp.concatenate(
        [w_in[:, :c_beta], w_in[:, c_gates:], w_in[:, c_beta:c_gates],
         jnp.zeros((D_MODEL, LANES - 2 * GD_HEADS), w_in.dtype)], axis=1)
    w_perm = (norm_in_w[:, None] * w_perm).astype(BF16)

    prw, zrw, pqkv, zgd, gates, ba = _projection(x2, w_perm)

    ltri = _block_lower_tri(REC_TT)
    pair = lambda p: p.reshape(N_PAIRS, 1, LANES)
    zero = jnp.zeros((RW_RANK, RW_WIDTH), F32)
    wda = jnp.concatenate([jnp.concatenate([rw_w2, zero], axis=1),
                           jnp.concatenate([zero, rw_a2], axis=1)], axis=0).astype(BF16)
    mu = rw_mu.reshape(RW_SHIFT_COLS // LANES, 1, LANES)
    ya = _rwkv_branch(prw, zrw, ltri, wda, mu, pair(rw_w0), pair(rw_a0), pair(rw_k_k),
                      pair(rw_k_a), pair(rw_r_k.reshape(RW_WIDTH)), pair(rw_gn_w),
                      pair(rw_gn_b), batch, seq_len)

    head = lambda p: jnp.broadcast_to(p.reshape(GD_HEADS, 1, 1), (GD_HEADS, 1, LANES))
    conv_w = jnp.swapaxes(gd_conv_w.reshape(GD_CONV, 3 * GD_HEADS, LANES), 0, 1)
    yb = _gdn_branch(pqkv, zgd, ba, ltri, conv_w, head(gd_A_log), head(gd_dt_bias),
                     gd_o_norm_w.reshape(1, GD_HEAD), batch, seq_len)
    return gates, ya, yb


def kernel(x, norm_in_w, w_in, rw_mu, rw_w0, rw_w2, rw_a0, rw_a2, rw_k_k, rw_k_a, rw_r_k,
           rw_gn_w, rw_gn_b, gd_conv_w, gd_A_log, gd_dt_bias, gd_o_norm_w,
           w_branch_a, w_branch_b, w_out, norm_out_w):
    batch, seq_len, d = x.shape
    assert d == D_MODEL and norm_in_w.shape[0] == 1
    assert seq_len % max(PROJ_TM, REC_TT) == 0 and (batch * seq_len) % MERGE_TM == 0
    x2 = x.reshape(batch * seq_len, d)
    gates, ya, yb = _layer(
        x2, batch, seq_len, norm_in_w[0], w_in[0], rw_mu[0], rw_w0[0], rw_w2[0], rw_a0[0],
        rw_a2[0], rw_k_k[0], rw_k_a[0], rw_r_k[0], rw_gn_w[0], rw_gn_b[0], gd_conv_w[0],
        gd_A_log[0], gd_dt_bias[0], gd_o_norm_w[0], w_branch_a[0], w_branch_b[0], w_out[0])
    out = _merge(x2, ya, yb, gates, w_branch_a[0].astype(BF16), w_branch_b[0].astype(BF16),
                 w_out[0].astype(BF16), norm_out_w.reshape(1, D_MODEL))
    return out.reshape(batch, seq_len, d)
```

```python
import jax
import jax.numpy as jnp
from jax import lax
from jax.experimental import pallas as pl
from jax.experimental.pallas import tpu as pltpu

F32 = jnp.float32
BF16 = jnp.bfloat16

D_MODEL = 1024
RW_WIDTH = 512
RW_HEAD = 64
RW_RANK = 64
RW_GN_EPS = RW_HEAD * 1e-5
GD_WIDTH = 512
GD_HEAD = 128
GD_HEADS = GD_WIDTH // GD_HEAD
GD_CONV = 4
NORM_EPS = 1e-6
L2_EPS = 1e-12

LANES = 128
CHUNK = 64
N_PAIRS = RW_WIDTH // LANES
RW_SHIFT_COLS = 3 * RW_WIDTH + 2 * RW_RANK
C_ZRW = RW_SHIFT_COLS
C_QKV = C_ZRW + RW_WIDTH
C_ZGD = C_QKV + 3 * GD_WIDTH
C_GATES = C_ZGD + GD_WIDTH
C_BA = C_GATES + 2 * D_MODEL
PROJ_COLS = C_BA + LANES

PROJ_TM = 512
CARRY_ROWS = 8
REC_TT = 512
CUM_ROWS = 256
MERGE_TM = 512
VMEM_LIMIT = 56 * 1024 * 1024


def _mm(a, b):
    return jnp.dot(a.astype(BF16), b.astype(BF16), preferred_element_type=F32)


def _mm_nt(a, b):
    return lax.dot_general(a.astype(BF16), b.astype(BF16), (((1,), (1,)), ((), ())),
                           preferred_element_type=F32)


def _mm_tn(a, b):
    return lax.dot_general(a.astype(BF16), b.astype(BF16), (((0,), (0,)), ((), ())),
                           preferred_element_type=F32)


def _sigmoid(x):
    return 1.0 / (1.0 + jnp.exp(-x))


def _silu(x):
    return x * _sigmoid(x)


def _softplus(x):
    return jnp.maximum(x, 0.0) + jnp.log(1.0 + jnp.exp(-jnp.abs(x)))


def _split3(x):
    hi = x.astype(BF16)
    r1 = x - hi.astype(F32)
    mid = r1.astype(BF16)
    lo = (r1 - mid.astype(F32)).astype(BF16)
    return hi, mid, lo


def _blk(idx, size):
    return lax.shift_right_logical(idx, size.bit_length() - 1)


def _tri_masks(n):
    ri = lax.broadcasted_iota(jnp.int32, (n, n), 0)
    ci = lax.broadcasted_iota(jnp.int32, (n, n), 1)
    same = _blk(ri, CHUNK) == _blk(ci, CHUNK)
    return ri, ci, same


def _unit_lower_inverses(lmats, ri, ci):
    eye = (ri == ci).astype(F32)
    blk8 = _blk(ri, 8) == _blk(ci, 8)
    n = lmats[0].shape[0]
    ds = [jnp.where(blk8, l, 0.0) for l in lmats]
    d2s = [_mm(d, d) for d in ds]
    ts = [eye + d for d in ds]
    xs = [_mm(d2, jnp.concatenate([t, d2], axis=1)) for t, d2 in zip(ts, d2s)]
    ts = [t + x[:, :n] for t, x in zip(ts, xs)]
    ts = [t + _mm(x[:, n:], t) for t, x in zip(ts, xs)]
    s = 8
    while s < CHUNK:
        groups = range(n // (2 * s))
        lower = lambda x: jnp.concatenate(
            [x[g * 2 * s + s:(g + 1) * 2 * s] for g in groups], axis=0)
        zeros = jnp.zeros((s, n), F32)
        rc = lax.broadcasted_iota(jnp.int32, (n // 2, n), 0)
        cl = lax.broadcasted_iota(jnp.int32, (n // 2, n), 1)
        rl = _blk(rc, s) * (2 * s) + s + (rc & (s - 1))
        off = (_blk(rl, 2 * s) == _blk(cl, 2 * s)) & (_blk(rl, s) != _blk(cl, s))
        xs = [_mm(jnp.where(off, lower(l), 0.0), t) for l, t in zip(lmats, ts)]
        xs = [jnp.concatenate([p for g in groups for p in (zeros, x[g * s:(g + 1) * s])], axis=0)
              for x in xs]
        ys = [_mm(lower(t), x) for t, x in zip(ts, xs)]
        ts = [jnp.concatenate([p for g in groups
                               for p in (t[g * 2 * s:g * 2 * s + s],
                                         t[g * 2 * s + s:(g + 1) * 2 * s] + y[g * s:(g + 1) * s])],
                              axis=0)
              for t, y in zip(ts, ys)]
        s *= 2
    return ts


def _chunk_cumsum(ltri, x):
    return jnp.concatenate([_mm(ltri, x[r0:r0 + CUM_ROWS]) for r0 in range(0, x.shape[0], CUM_ROWS)],
                           axis=0)


def _shifted_rows(carry, cur, s):
    return pltpu.roll(jnp.concatenate([carry, cur], axis=0), s, 0)[CARRY_ROWS:]


def _proj_kernel(x_ref, g_ref, w_ref, prw_ref, zrw_ref, pqkv_ref, zgd_ref, gates_ref, ba_ref):
    xc = x_ref[...]
    inv = lax.rsqrt(jnp.mean(xc * xc, axis=-1, keepdims=True) + NORM_EPS)
    xb = (xc * g_ref[...]).astype(BF16)

    def proj(c0, c1):
        return jnp.dot(xb, w_ref[:, c0:c1], preferred_element_type=F32) * inv

    def lane_groups(c0, n_groups, out_ref):
        for g0 in range(0, n_groups, 4):
            ng = min(4, n_groups - g0)
            pe = proj(c0 + g0 * LANES, c0 + (g0 + ng) * LANES)
            for j in range(ng):
                out_ref[g0 + j] = pe[:, j * LANES:(j + 1) * LANES].astype(BF16)

    lane_groups(0, RW_SHIFT_COLS // LANES, prw_ref)
    lane_groups(C_ZRW, N_PAIRS, zrw_ref)
    lane_groups(C_QKV, 3 * GD_HEADS, pqkv_ref)
    lane_groups(C_ZGD, GD_HEADS, zgd_ref)
    for c0 in range(0, 2 * D_MODEL, 512):
        gates_ref[:, c0:c0 + 512] = proj(C_GATES + c0, C_GATES + c0 + 512).astype(BF16)
    ba_ref[...] = proj(C_BA, PROJ_COLS)


def _projection(x2, norm_w, w_perm):
    bt = x2.shape[0]
    tm = PROJ_TM
    const = lambda i: (0, 0)
    slabs = lambda n: pl.BlockSpec((n, tm, LANES), lambda i: (0, i, 0))
    n_rw = RW_SHIFT_COLS // LANES
    out_shapes = (
        jax.ShapeDtypeStruct((n_rw, bt, LANES), BF16),
        jax.ShapeDtypeStruct((N_PAIRS, bt, LANES), BF16),
        jax.ShapeDtypeStruct((3 * GD_HEADS, bt, LANES), BF16),
        jax.ShapeDtypeStruct((GD_HEADS, bt, LANES), BF16),
        jax.ShapeDtypeStruct((bt, 2 * D_MODEL), BF16),
        jax.ShapeDtypeStruct((bt, LANES), F32),
    )
    out_specs = (
        slabs(n_rw), slabs(N_PAIRS), slabs(3 * GD_HEADS), slabs(GD_HEADS),
        pl.BlockSpec((tm, 2 * D_MODEL), lambda i: (i, 0)),
        pl.BlockSpec((tm, LANES), lambda i: (i, 0)),
    )
    in_specs = [
        pl.BlockSpec((tm, D_MODEL), lambda i: (i, 0)),
        pl.BlockSpec((1, D_MODEL), const),
        pl.BlockSpec((D_MODEL, PROJ_COLS), const, pipeline_mode=pl.Buffered(1)),
    ]
    return pl.pallas_call(
        _proj_kernel,
        grid=(bt // tm,),
        in_specs=in_specs,
        out_specs=out_specs,
        out_shape=out_shapes,
        compiler_params=pltpu.CompilerParams(
            dimension_semantics=("arbitrary",), vmem_limit_bytes=VMEM_LIMIT),
        name="rmsnorm_in_proj",
    )(x2, norm_w, w_perm)


def _rwkv_kernel(prw_ref, z_ref, ltri_ref, wda_ref, mu_ref, w0_ref, a0_ref,
                 kk_ref, ka_ref, rk_ref, gnw_ref, gnb_ref, y_ref, s_scr, carry_scr):
    t_idx = pl.program_id(1)

    @pl.when(t_idx == 0)
    def _():
        s_scr[...] = jnp.zeros_like(s_scr)
        carry_scr[...] = jnp.zeros_like(carry_scr)

    tt = z_ref.shape[1]
    n_chunks = tt // CHUNK

    def token_shift(g):
        cur = prw_ref[g].astype(F32)
        prev = _shifted_rows(carry_scr[g], cur, 1)
        carry_scr[g] = cur[tt - CARRY_ROWS:]
        return cur + (prev - cur) * mu_ref[g]

    lane = lax.broadcasted_iota(jnp.int32, (1, LANES), 1)
    head_a = lane < RW_HEAD
    ri, ci, same = _tri_masks(LANES)
    strict = same & (ci < ri)
    causal = same & (ci <= ri)

    def head_sum(x):
        sa = jnp.sum(jnp.where(head_a, x, 0.0), axis=-1, keepdims=True)
        sb = jnp.sum(jnp.where(head_a, 0.0, x), axis=-1, keepdims=True)
        return jnp.where(head_a, jnp.broadcast_to(sa, x.shape), jnp.broadcast_to(sb, x.shape))
    ltri = ltri_ref[...]

    def pair_rows(x):
        return jnp.concatenate([jnp.where(head_a, x, 0.0), jnp.where(head_a, 0.0, x)], axis=0)

    wa = token_shift(3 * N_PAIRS)
    da = jnp.where(lane < RW_RANK, jnp.tanh(wa), wa)
    pre = _mm(da, wda_ref[...])

    bonus, p_end, work = [], [], []
    for j in range(N_PAIRS):
        r = token_shift(j)
        k = token_shift(N_PAIRS + j)
        v = token_shift(2 * N_PAIRS + j)
        cols = slice(j * LANES, (j + 1) * LANES)
        log_w = -_softplus(-(w0_ref[j] + pre[:, cols])) - 0.5
        lw = -jnp.exp(log_w)
        lr = _sigmoid(a0_ref[j] + pre[:, RW_WIDTH + j * LANES:RW_WIDTH + (j + 1) * LANES])
        kkr = k * kk_ref[j]
        kk = kkr * lax.rsqrt(head_sum(kkr * kkr) + L2_EPS)
        k2 = k * (1.0 + (lr - 1.0) * ka_ref[j])
        b = kk * lr
        bonus.append(head_sum(r * k2 * rk_ref[j]) * v)
        lw_hi = lw.astype(BF16)
        lw_lo = (lw - lw_hi.astype(F32)).astype(BF16)
        cl2 = _chunk_cumsum(ltri, jnp.concatenate([lw_hi, lw_lo], axis=1))
        clw = cl2[:, :LANES] + cl2[:, LANES:]
        e_neg = jnp.exp(-clw)
        rt = r * jnp.exp(clw)
        at = -kk * jnp.exp(clw - lw)
        bt = b * e_neg
        kt = k2 * e_neg
        for c in range(n_chunks):
            sl = slice(c * CHUNK, (c + 1) * CHUNK)
            cl_last = clw[(c + 1) * CHUNK - 1:(c + 1) * CHUNK, :]
            e_end = jnp.exp(cl_last - clw[sl])
            p_end.append(jnp.sum(jnp.where(ri == ci, jnp.exp(cl_last), 0.0), axis=-1,
                                 keepdims=True))
            work.append(dict(
                ar=jnp.concatenate([pair_rows(at[sl]), pair_rows(rt[sl])], axis=0),
                bk=jnp.concatenate([pair_rows(bt[sl]), pair_rows(kt[sl])], axis=0),
                v=pair_rows(v[sl]),
                bk_end=jnp.concatenate([pair_rows(b[sl] * e_end), pair_rows(k2[sl] * e_end)],
                                       axis=0)))

    n = len(work)
    strict2 = jnp.concatenate([strict, strict], axis=1)
    causal2 = jnp.concatenate([causal, causal], axis=1)
    zeros = jnp.zeros((LANES, LANES), F32)
    ms = [_mm_nt(w["ar"], w["bk"]) for w in work]
    lks = [jnp.where(strict2, m[:LANES], 0.0) for m in ms]
    t_invs = _unit_lower_inverses([lk[:, :LANES] for lk in lks], ri, ci)
    akvs = [_mm(lk[:, LANES:], w["v"]) for lk, w in zip(lks, work)]
    tws = [_mm(t, jnp.concatenate([w["ar"][:LANES], akv], axis=1))
           for t, w, akv in zip(t_invs, work, akvs)]
    xs = [jnp.concatenate([tw, jnp.concatenate([zeros, w["v"]], axis=1)], axis=0)
          for tw, w in zip(tws, work)]
    qys = [_mm(jnp.where(causal2, m[LANES:], 0.0), x) for m, x in zip(ms, xs)]
    zs = [_mm_tn(w["bk_end"], x) for w, x in zip(work, xs)]

    states = [None] * n
    for c in range(n_chunks):
        for j in range(N_PAIRS):
            i = j * n_chunks + c
            if c == 0:
                states[i] = s_scr[j]
            h0 = states[i]
            h1 = p_end[i] * h0 + _mm(zs[i][:, :LANES], h0) + zs[i][:, LANES:]
            if c == n_chunks - 1:
                s_scr[j] = h1
            else:
                states[i + 1] = h1

    for j in range(N_PAIRS):
        ys = []
        for c in range(n_chunks):
            i = j * n_chunks + c
            qc = work[i]["ar"][LANES:] + qys[i][:, :LANES]
            y_p = _mm(qc, states[i]) + qys[i][:, LANES:]
            ys.append(y_p[:CHUNK] + y_p[CHUNK:])
        y = jnp.concatenate(ys, axis=0)
        inv_n = 1.0 / RW_HEAD
        yc = y - head_sum(y) * inv_n
        var = head_sum(yc * yc) * inv_n
        yn = yc * lax.rsqrt(var + RW_GN_EPS) * gnw_ref[j] + gnb_ref[j]
        y_ref[j] = ((yn + bonus[j]) * _silu(z_ref[j].astype(F32))).astype(y_ref.dtype)


def _rwkv_branch(prw, zrw, ltri, wda, mu, w0, a0, k_k, k_a, r_k, gn_w, gn_b, batch, seq_len):
    tt = REC_TT
    nt = seq_len // tt
    bt = batch * seq_len
    tok = lambda n: pl.BlockSpec((n, tt, LANES), lambda b, t: (0, b * nt + t, 0))
    par = pl.BlockSpec((N_PAIRS, 1, LANES), lambda b, t: (0, 0, 0))
    n_rw = RW_SHIFT_COLS // LANES
    in_specs = [
        tok(n_rw),
        tok(N_PAIRS),
        pl.BlockSpec((CUM_ROWS, CUM_ROWS), lambda b, t: (0, 0)),
        pl.BlockSpec((LANES, 2 * RW_WIDTH), lambda b, t: (0, 0)),
        pl.BlockSpec((n_rw, 1, LANES), lambda b, t: (0, 0, 0)),
        par, par, par, par, par, par, par,
    ]
    return pl.pallas_call(
        _rwkv_kernel,
        grid=(batch, nt),
        in_specs=in_specs,
        out_specs=tok(N_PAIRS),
        out_shape=jax.ShapeDtypeStruct((N_PAIRS, bt, LANES), BF16),
        scratch_shapes=[pltpu.VMEM((N_PAIRS, LANES, LANES), F32),
                        pltpu.VMEM((n_rw, CARRY_ROWS, LANES), F32)],
        compiler_params=pltpu.CompilerParams(
            dimension_semantics=("arbitrary", "arbitrary"), vmem_limit_bytes=VMEM_LIMIT),
        name="rwkv7_chunked",
    )(prw, zrw, ltri, wda, mu, w0, a0, k_k, k_a, r_k, gn_w, gn_b)


def _gdn_kernel(pqkv_ref, z_ref, ba_ref, ltri_ref, cw_ref, alog_ref, dtb_ref, onw_ref,
                y_ref, s_scr, carry_scr):
    t_idx = pl.program_id(1)

    @pl.when(t_idx == 0)
    def _():
        s_scr[...] = jnp.zeros_like(s_scr)
        carry_scr[...] = jnp.zeros_like(carry_scr)

    tt = ba_ref.shape[0]

    def conv_silu(g):
        cur = pqkv_ref[g].astype(F32)
        carry = carry_scr[g]
        taps = cw_ref[g]
        acc = cur * taps[GD_CONV - 1:GD_CONV]
        for s in range(1, GD_CONV):
            acc = acc + _shifted_rows(carry, cur, s) * taps[GD_CONV - 1 - s:GD_CONV - s]
        carry_scr[g] = cur[tt - CARRY_ROWS:]
        return _silu(acc)

    n_super = tt // LANES
    per_super = LANES // CHUNK
    ri, ci, same = _tri_masks(LANES)
    strict = same & (ci < ri)
    causal = same & (ci <= ri)
    ba = ba_ref[...]
    g_all = -jnp.exp(alog_ref[...]) * _softplus(ba + dtb_ref[...])
    gc3 = _chunk_cumsum(ltri_ref[...], jnp.concatenate(_split3(g_all), axis=1))
    gc_all = gc3[:, :LANES] + gc3[:, LANES:2 * LANES] + gc3[:, 2 * LANES:]
    gc_all_t = gc_all.T

    work = []
    for h in range(GD_HEADS):
        q = conv_silu(h)
        k = conv_silu(GD_HEADS + h)
        v = conv_silu(2 * GD_HEADS + h)
        qn = q * (lax.rsqrt(jnp.sum(q * q, axis=-1, keepdims=True) + L2_EPS) * (GD_HEAD ** -0.5))
        kn = k * lax.rsqrt(jnp.sum(k * k, axis=-1, keepdims=True) + L2_EPS)
        beta = jnp.broadcast_to(_sigmoid(ba[:, h:h + 1]), (tt, LANES))
        gc_col = jnp.broadcast_to(gc_all[:, GD_HEADS + h:GD_HEADS + h + 1], (tt, LANES))
        gc_row = gc_all_t[GD_HEADS + h:GD_HEADS + h + 1, :]
        e_col = jnp.exp(gc_col)
        kb = kn * beta
        vb = v * beta
        qd = qn * e_col
        kbe = kb * e_col
        for sc in range(n_super):
            sl = slice(sc * LANES, (sc + 1) * LANES)
            diff = gc_col[sl] - gc_row[:, sl]
            chunks = []
            for cc in range(per_super):
                rows = slice(sc * LANES + cc * CHUNK, sc * LANES + (cc + 1) * CHUNK)
                g_last = gc_col[rows.stop - 1:rows.stop, :]
                chunks.append(dict(kte=kn[rows] * jnp.exp(g_last - gc_col[rows]),
                                   decay=jnp.exp(g_last)))
            work.append(dict(h=h, dec=jnp.exp(jnp.where(causal, diff, -1e30)),
                             kq=jnp.concatenate([kb[sl], qn[sl]], axis=0), kn=kn[sl],
                             rhs=jnp.concatenate([vb[sl], kbe[sl]], axis=1), qd=qd[sl],
                             chunks=chunks))

    for w in work:
        w["m"] = _mm_nt(w["kq"], w["kn"])
    t_invs = _unit_lower_inverses(
        [-jnp.where(strict, w["m"][:LANES] * w["dec"], 0.0) for w in work], ri, ci)
    uws = [_mm(t, w["rhs"]) for t, w in zip(t_invs, work)]
    qws = [_mm(w["m"][LANES:] * w["dec"], uw) for w, uw in zip(work, uws)]
    for w, uw, qw in zip(work, uws, qws):
        w["qc"] = w["qd"] - qw[:, LANES:]
        w["o0"] = qw[:, :LANES]
        for cc, ch in enumerate(w["chunks"]):
            ch["uw"] = uw[cc * CHUNK:(cc + 1) * CHUNK]
    for w in work:
        for ch in w["chunks"]:
            ch["tn"] = _mm_tn(ch["kte"], ch["uw"])

    state = [s_scr[h] for h in range(GD_HEADS)]
    for sc in range(n_super):
        for cc in range(per_super):
            for h in range(GD_HEADS):
                ch = work[h * n_super + sc]["chunks"][cc]
                s0 = state[h]
                ch["s0"] = s0
                state[h] = s0 * ch["decay"] - _mm(ch["tn"][:, LANES:], s0) + ch["tn"][:, :LANES]
    for h in range(GD_HEADS):
        s_scr[h] = state[h]

    for h in range(GD_HEADS):
        os = []
        for sc in range(n_super):
            w = work[h * n_super + sc]
            for cc, ch in enumerate(w["chunks"]):
                rows = slice(cc * CHUNK, (cc + 1) * CHUNK)
                os.append(_mm(w["qc"][rows], ch["s0"]) + w["o0"][rows])
        o = jnp.concatenate(os, axis=0)
        ms = jnp.mean(o * o, axis=-1, keepdims=True)
        on = o * lax.rsqrt(ms + NORM_EPS) * onw_ref[...]
        y_ref[h] = (on * _silu(z_ref[h].astype(F32))).astype(y_ref.dtype)


def _gdn_branch(pqkv, zgd, ba, ltri, conv_w, alog, dtb, onw, batch, seq_len):
    tt = REC_TT
    nt = seq_len // tt
    bt = batch * seq_len
    tok = lambda n: pl.BlockSpec((n, tt, LANES), lambda b, t: (0, b * nt + t, 0))
    par = pl.BlockSpec((1, LANES), lambda b, t: (0, 0))
    in_specs = [
        tok(3 * GD_HEADS),
        tok(GD_HEADS),
        pl.BlockSpec((tt, LANES), lambda b, t: (b * nt + t, 0)),
        pl.BlockSpec((CUM_ROWS, CUM_ROWS), lambda b, t: (0, 0)),
        pl.BlockSpec((3 * GD_HEADS, GD_CONV, LANES), lambda b, t: (0, 0, 0)),
        par, par,
        pl.BlockSpec((1, LANES), lambda b, t: (0, 0)),
    ]
    return pl.pallas_call(
        _gdn_kernel,
        grid=(batch, nt),
        in_specs=in_specs,
        out_specs=tok(GD_HEADS),
        out_shape=jax.ShapeDtypeStruct((GD_HEADS, bt, LANES), BF16),
        scratch_shapes=[pltpu.VMEM((GD_HEADS, LANES, LANES), F32),
                        pltpu.VMEM((3 * GD_HEADS, CARRY_ROWS, LANES), F32)],
        compiler_params=pltpu.CompilerParams(
            dimension_semantics=("arbitrary", "arbitrary"), vmem_limit_bytes=VMEM_LIMIT),
        name="gated_delta_chunked",
    )(pqkv, zgd, ba, ltri, conv_w, alog, dtb, onw)


def _merge_kernel(x_ref, ya_ref, yb_ref, gates_ref, wa_ref, wb_ref, wo_ref, g_ref, o_ref):
    ya = jnp.concatenate([ya_ref[j] for j in range(N_PAIRS)], axis=1)
    yb = jnp.concatenate([yb_ref[j] for j in range(GD_HEADS)], axis=1)
    ma = jnp.dot(ya, wa_ref[...], preferred_element_type=F32)
    mb = jnp.dot(yb, wb_ref[...], preferred_element_type=F32)
    gates = gates_ref[...].astype(F32)
    merged = _sigmoid(gates[:, :D_MODEL]) * ma + _sigmoid(gates[:, D_MODEL:]) * mb
    res = x_ref[...] + jnp.dot(merged.astype(BF16), wo_ref[...], preferred_element_type=F32)
    ms = jnp.mean(res * res, axis=-1, keepdims=True)
    o_ref[...] = res * lax.rsqrt(ms + NORM_EPS) * g_ref[...]


def _merge(x2, ya, yb, gates, w_a, w_b, w_o, norm_w):
    bt = x2.shape[0]
    tm = MERGE_TM
    const = lambda i: (0, 0)
    in_specs = [
        pl.BlockSpec((tm, D_MODEL), lambda i: (i, 0)),
        pl.BlockSpec((N_PAIRS, tm, LANES), lambda i: (0, i, 0)),
        pl.BlockSpec((GD_HEADS, tm, LANES), lambda i: (0, i, 0)),
        pl.BlockSpec((tm, 2 * D_MODEL), lambda i: (i, 0)),
        pl.BlockSpec((RW_WIDTH, D_MODEL), const),
        pl.BlockSpec((GD_WIDTH, D_MODEL), const),
        pl.BlockSpec((D_MODEL, D_MODEL), const),
        pl.BlockSpec((1, D_MODEL), const),
    ]
    return pl.pallas_call(
        _merge_kernel,
        grid=(bt // tm,),
        in_specs=in_specs,
        out_specs=pl.BlockSpec((tm, D_MODEL), lambda i: (i, 0)),
        out_shape=jax.ShapeDtypeStruct((bt, D_MODEL), F32),
        compiler_params=pltpu.CompilerParams(
            dimension_semantics=("arbitrary",), vmem_limit_bytes=VMEM_LIMIT),
        name="merge_out_proj",
    )(x2, ya, yb, gates, w_a, w_b, w_o, norm_w)


def _block_lower_tri(n):
    i = jnp.arange(n)
    return (((i[:, None] // CHUNK) == (i[None, :] // CHUNK)) & (i[None, :] <= i[:, None])).astype(BF16)


def _layer(x2, batch, seq_len, norm_in_w, w_in, rw_mu, rw_w0, rw_w2, rw_a0, rw_a2, rw_k_k,
           rw_k_a, rw_r_k, rw_gn_w, rw_gn_b, gd_conv_w, gd_A_log, gd_dt_bias, gd_o_norm_w,
           w_branch_a, w_branch_b, w_out):
    c_beta = C_ZGD + GD_WIDTH
    c_gates = c_beta + 2 * GD_HEADS
    w_bf = w_in.astype(BF16)
    w_perm = jnp.concatenate(
        [w_bf[:, :c_beta], w_bf[:, c_gates:], w_bf[:, c_beta:c_gates],
         jnp.zeros((D_MODEL, LANES - 2 * GD_HEADS), BF16)], axis=1)

    prw, zrw, pqkv, zgd, gates, ba = _projection(x2, norm_in_w.reshape(1, D_MODEL), w_perm)

    ltri = _block_lower_tri(CUM_ROWS)
    pair = lambda p: p.reshape(N_PAIRS, 1, LANES)
    zero = jnp.zeros((RW_RANK, RW_WIDTH), F32)
    wda = jnp.concatenate([jnp.concatenate([rw_w2, zero], axis=1),
                           jnp.concatenate([zero, rw_a2], axis=1)], axis=0).astype(BF16)
    mu = rw_mu.reshape(RW_SHIFT_COLS // LANES, 1, LANES)
    ya = _rwkv_branch(prw, zrw, ltri, wda, mu, pair(rw_w0), pair(rw_a0), pair(rw_k_k),
                      pair(rw_k_a), pair(rw_r_k.reshape(RW_WIDTH)), pair(rw_gn_w),
                      pair(rw_gn_b), batch, seq_len)

    head = lambda p: jnp.pad(p.astype(F32), (GD_HEADS, LANES - 2 * GD_HEADS)).reshape(1, LANES)
    conv_w = jnp.swapaxes(gd_conv_w.reshape(GD_CONV, 3 * GD_HEADS, LANES), 0, 1)
    yb = _gdn_branch(pqkv, zgd, ba, ltri, conv_w, head(gd_A_log), head(gd_dt_bias),
                     gd_o_norm_w.reshape(1, GD_HEAD), batch, seq_len)
    return gates, ya, yb


def kernel(x, norm_in_w, w_in, rw_mu, rw_w0, rw_w2, rw_a0, rw_a2, rw_k_k, rw_k_a, rw_r_k,
           rw_gn_w, rw_gn_b, gd_conv_w, gd_A_log, gd_dt_bias, gd_o_norm_w,
           w_branch_a, w_branch_b, w_out, norm_out_w):
    batch, seq_len, d = x.shape
    assert d == D_MODEL and norm_in_w.shape[0] == 1
    assert seq_len % max(PROJ_TM, REC_TT) == 0 and (batch * seq_len) % MERGE_TM == 0
    x2 = x.reshape(batch * seq_len, d)
    gates, ya, yb = _layer(
        x2, batch, seq_len, norm_in_w[0], w_in[0], rw_mu[0], rw_w0[0], rw_w2[0], rw_a0[0],
        rw_a2[0], rw_k_k[0], rw_k_a[0], rw_r_k[0], rw_gn_w[0], rw_gn_b[0], gd_conv_w[0],
        gd_A_log[0], gd_dt_bias[0], gd_o_norm_w[0], w_branch_a[0], w_branch_b[0], w_out[0])
    out = _merge(x2, ya, yb, gates, w_branch_a[0].astype(BF16), w_branch_b[0].astype(BF16),
                 w_out[0].astype(BF16), norm_out_w.reshape(1, D_MODEL))
    return out.reshape(batch, seq_len, d)
```

```python
import jax
import jax.numpy as jnp
from jax import lax
from jax.experimental import pallas as pl
from jax.experimental.pallas import tpu as pltpu

F32 = jnp.float32
BF16 = jnp.bfloat16

D_MODEL = 1024
RW_WIDTH = 512
RW_HEAD = 64
RW_RANK = 64
RW_GN_EPS = RW_HEAD * 1e-5
DECAY_SCALE = -0.6065306597126334
GD_WIDTH = 512
GD_HEAD = 128
GD_HEADS = GD_WIDTH // GD_HEAD
GD_CONV = 4
NORM_EPS = 1e-6
L2_EPS = 1e-12

LANES = 128
CHUNK = 64
N_PAIRS = RW_WIDTH // LANES
RW_SHIFT_COLS = 3 * RW_WIDTH + 2 * RW_RANK
C_ZRW = RW_SHIFT_COLS
C_QKV = C_ZRW + RW_WIDTH
C_ZGD = C_QKV + 3 * GD_WIDTH
C_GATES = C_ZGD + GD_WIDTH

PROJ_TM = 512
CARRY_ROWS = 8
REC_TT = 512
CUM_ROWS = 256
MERGE_TM = 512
VMEM_LIMIT = 56 * 1024 * 1024


def _mm(a, b):
    return jnp.dot(a.astype(BF16), b.astype(BF16), preferred_element_type=F32)


def _mm_nt(a, b):
    return lax.dot_general(a.astype(BF16), b.astype(BF16), (((1,), (1,)), ((), ())),
                           preferred_element_type=F32)


def _mm_tn(a, b):
    return lax.dot_general(a.astype(BF16), b.astype(BF16), (((0,), (0,)), ((), ())),
                           preferred_element_type=F32)


def _sigmoid(x):
    return 0.5 * jnp.tanh(0.5 * x) + 0.5


def _silu(x):
    h = 0.5 * x
    return h * jnp.tanh(h) + h


def _softplus(x):
    return jnp.maximum(x, 0.0) + jnp.log(1.0 + jnp.exp(-jnp.abs(x)))


def _split3(x):
    hi = x.astype(BF16)
    r1 = x - hi.astype(F32)
    mid = r1.astype(BF16)
    lo = (r1 - mid.astype(F32)).astype(BF16)
    return hi, mid, lo


def _blk(idx, size):
    return lax.shift_right_logical(idx, size.bit_length() - 1)


def _tri_masks(n):
    ri = lax.broadcasted_iota(jnp.int32, (n, n), 0)
    ci = lax.broadcasted_iota(jnp.int32, (n, n), 1)
    same = _blk(ri, CHUNK) == _blk(ci, CHUNK)
    return ri, ci, same


def _unit_lower_inverses(lmats, ri, ci):
    eye = (ri == ci).astype(F32)
    blk8 = _blk(ri, 8) == _blk(ci, 8)
    n = lmats[0].shape[0]
    ds = [jnp.where(blk8, l, 0.0) for l in lmats]
    dbs = [d.astype(BF16) for d in ds]
    d2s = [_mm(d, d).astype(BF16) for d in dbs]
    ts = [eye + d for d in ds]
    xs = [_mm(d2, jnp.concatenate([t.astype(BF16), d2], axis=1)) for t, d2 in zip(ts, d2s)]
    ts = [t + x[:, :n] for t, x in zip(ts, xs)]
    ts = [t + _mm(x[:, n:], t) for t, x in zip(ts, xs)]
    s = 8
    while s < CHUNK:
        groups = range(n // (2 * s))
        lower = lambda x: jnp.concatenate(
            [x[g * 2 * s + s:(g + 1) * 2 * s] for g in groups], axis=0)
        zeros = jnp.zeros((s, n), F32)
        rc = lax.broadcasted_iota(jnp.int32, (n // 2, n), 0)
        cl = lax.broadcasted_iota(jnp.int32, (n // 2, n), 1)
        rl = _blk(rc, s) * (2 * s) + s + (rc & (s - 1))
        off = (_blk(rl, 2 * s) == _blk(cl, 2 * s)) & (_blk(rl, s) != _blk(cl, s))
        tbs = [t.astype(BF16) for t in ts]
        xs = [_mm(jnp.where(off, lower(l), 0.0), t) for l, t in zip(lmats, tbs)]
        xs = [jnp.concatenate([p for g in groups for p in (zeros, x[g * s:(g + 1) * s])], axis=0)
              for x in xs]
        ys = [_mm(lower(t), x) for t, x in zip(ts, xs)]
        ts = [jnp.concatenate([p for g in groups
                               for p in (t[g * 2 * s:g * 2 * s + s],
                                         t[g * 2 * s + s:(g + 1) * 2 * s] + y[g * s:(g + 1) * s])],
                              axis=0)
              for t, y in zip(ts, ys)]
        s *= 2
    return [t.astype(BF16) for t in ts]


def _chunk_cumsum(ltri, x):
    return jnp.concatenate([_mm(ltri, x[r0:r0 + CUM_ROWS]) for r0 in range(0, x.shape[0], CUM_ROWS)],
                           axis=0)


def _proj_kernel(x_ref, g_ref, w_ref, wg_ref, wba_ref,
                 prw_ref, zrw_ref, pqkv_ref, zgd_ref, gates_ref, ba_ref):
    xc = x_ref[...]
    inv = lax.rsqrt(jnp.mean(xc * xc, axis=-1, keepdims=True) + NORM_EPS)
    xb = (xc * g_ref[...]).astype(BF16)

    def proj(w):
        return jnp.dot(xb, w, preferred_element_type=F32) * inv

    def lane_groups(c0, n_groups, out_ref):
        for g0 in range(0, n_groups, 4):
            ng = min(4, n_groups - g0)
            pe = proj(w_ref[:, c0 + g0 * LANES:c0 + (g0 + ng) * LANES])
            for j in range(ng):
                out_ref[g0 + j] = pe[:, j * LANES:(j + 1) * LANES].astype(BF16)

    lane_groups(0, RW_SHIFT_COLS // LANES, prw_ref)
    lane_groups(C_ZRW, N_PAIRS, zrw_ref)
    lane_groups(C_QKV, 3 * GD_HEADS, pqkv_ref)
    lane_groups(C_ZGD, GD_HEADS, zgd_ref)
    for c0 in range(0, 2 * D_MODEL, 512):
        gates_ref[:, c0:c0 + 512] = proj(wg_ref[:, c0:c0 + 512]).astype(BF16)
    ba_ref[...] = proj(wba_ref[...])


def _projection(x2, norm_w, w_main, w_gates, w_ba):
    bt = x2.shape[0]
    tm = PROJ_TM
    const = lambda i: (0, 0)
    slabs = lambda n: pl.BlockSpec((n, tm, LANES), lambda i: (0, i, 0))
    n_rw = RW_SHIFT_COLS // LANES
    out_shapes = (
        jax.ShapeDtypeStruct((n_rw, bt, LANES), BF16),
        jax.ShapeDtypeStruct((N_PAIRS, bt, LANES), BF16),
        jax.ShapeDtypeStruct((3 * GD_HEADS, bt, LANES), BF16),
        jax.ShapeDtypeStruct((GD_HEADS, bt, LANES), BF16),
        jax.ShapeDtypeStruct((bt, 2 * D_MODEL), BF16),
        jax.ShapeDtypeStruct((bt, LANES), F32),
    )
    out_specs = (
        slabs(n_rw), slabs(N_PAIRS), slabs(3 * GD_HEADS), slabs(GD_HEADS),
        pl.BlockSpec((tm, 2 * D_MODEL), lambda i: (i, 0)),
        pl.BlockSpec((tm, LANES), lambda i: (i, 0)),
    )
    in_specs = [
        pl.BlockSpec((tm, D_MODEL), lambda i: (i, 0)),
        pl.BlockSpec((1, D_MODEL), const),
        pl.BlockSpec((D_MODEL, C_GATES), const, pipeline_mode=pl.Buffered(1)),
        pl.BlockSpec((D_MODEL, 2 * D_MODEL), const, pipeline_mode=pl.Buffered(1)),
        pl.BlockSpec((D_MODEL, LANES), const),
    ]
    return pl.pallas_call(
        _proj_kernel,
        grid=(bt // tm,),
        in_specs=in_specs,
        out_specs=out_specs,
        out_shape=out_shapes,
        compiler_params=pltpu.CompilerParams(
            dimension_semantics=("arbitrary",), vmem_limit_bytes=VMEM_LIMIT),
        name="rmsnorm_in_proj",
    )(x2, norm_w, w_main, w_gates, w_ba)


def _rwkv_kernel(prw_ref, z_ref, ltri_ref, wda_ref, mu_ref, w0_ref, a0_ref,
                 kk_ref, ka_ref, rk_ref, gnw_ref, gnb_ref, y_ref, s_scr, carry_scr):
    t_idx = pl.program_id(1)

    tt = z_ref.shape[1]
    n_chunks = tt // CHUNK

    @pl.when(t_idx == 0)
    def _():
        s_scr[...] = jnp.zeros_like(s_scr)
        carry_scr[:, :CARRY_ROWS, :] = jnp.zeros((carry_scr.shape[0], CARRY_ROWS, LANES), F32)

    def token_shift(g):
        cur = prw_ref[g].astype(F32)
        carry_scr[g, CARRY_ROWS:, :] = cur
        prev = carry_scr[g, pl.ds(CARRY_ROWS - 1, tt), :]
        carry_scr[g, :CARRY_ROWS, :] = cur[tt - CARRY_ROWS:]
        return cur + (prev - cur) * mu_ref[g]

    lane = lax.broadcasted_iota(jnp.int32, (1, LANES), 1)
    head_a = lane < RW_HEAD
    ri, ci, same = _tri_masks(LANES)
    strict = same & (ci < ri)
    causal = same & (ci <= ri)

    def head_sum(x):
        sa = jnp.sum(jnp.where(head_a, x, 0.0), axis=-1, keepdims=True)
        sb = jnp.sum(jnp.where(head_a, 0.0, x), axis=-1, keepdims=True)
        return jnp.where(head_a, jnp.broadcast_to(sa, x.shape), jnp.broadcast_to(sb, x.shape))
    ltri = ltri_ref[...]

    def pair_rows(x):
        return jnp.concatenate([jnp.where(head_a, x, 0.0), jnp.where(head_a, 0.0, x)], axis=0)

    wa = token_shift(3 * N_PAIRS)
    da = jnp.where(lane < RW_RANK, jnp.tanh(wa), wa)
    pre = _mm(da, wda_ref[...])

    bonus, p_end, work = [], [], []
    for j in range(N_PAIRS):
        r = token_shift(j)
        k = token_shift(N_PAIRS + j)
        v = token_shift(2 * N_PAIRS + j)
        cols = slice(j * LANES, (j + 1) * LANES)
        lw = DECAY_SCALE * _sigmoid(w0_ref[j] + pre[:, cols])
        lr = _sigmoid(a0_ref[j] + pre[:, RW_WIDTH + j * LANES:RW_WIDTH + (j + 1) * LANES])
        kkr = k * kk_ref[j]
        kk = kkr * lax.rsqrt(head_sum(kkr * kkr) + L2_EPS)
        k2 = k * (1.0 + (lr - 1.0) * ka_ref[j])
        b = kk * lr
        bonus.append(head_sum(r * k2 * rk_ref[j]) * v)
        lw_hi = lw.astype(BF16)
        lw_lo = (lw - lw_hi.astype(F32)).astype(BF16)
        cl2 = _chunk_cumsum(ltri, jnp.concatenate([lw_hi, lw_lo], axis=1))
        clw = cl2[:, :LANES] + cl2[:, LANES:]
        e_neg = jnp.exp(-clw)
        rt = r * jnp.exp(clw)
        at = -kk * jnp.exp(clw - lw)
        bt = b * e_neg
        kt = k2 * e_neg
        for c in range(n_chunks):
            sl = slice(c * CHUNK, (c + 1) * CHUNK)
            p_c = jnp.exp(clw[(c + 1) * CHUNK - 1:(c + 1) * CHUNK, :])
            p_end.append(jnp.sum(jnp.where(ri == ci, p_c, 0.0), axis=-1, keepdims=True))
            work.append(dict(
                ar=jnp.concatenate([pair_rows(at[sl]), pair_rows(rt[sl])], axis=0).astype(BF16),
                bk=jnp.concatenate([bt[sl], bt[sl], kt[sl], kt[sl]], axis=0).astype(BF16),
                v=pair_rows(v[sl]).astype(BF16),
                bk_end=jnp.concatenate([pair_rows(bt[sl] * p_c), pair_rows(kt[sl] * p_c)],
                                       axis=0).astype(BF16)))

    n = len(work)
    causal2 = jnp.concatenate([causal, causal], axis=1)
    zeros = jnp.zeros((LANES, LANES), BF16)
    ms = [_mm_nt(w["ar"], w["bk"]) for w in work]
    lbs = [jnp.where(strict, m[:LANES, :LANES], 0.0) for m in ms]
    aks = [jnp.where(strict, m[:LANES, LANES:], 0.0).astype(BF16) for m in ms]
    rbks = [jnp.where(causal2, m[LANES:], 0.0).astype(BF16) for m in ms]
    t_invs = _unit_lower_inverses(lbs, ri, ci)
    akvs = [_mm(ak, w["v"]).astype(BF16) for ak, w in zip(aks, work)]
    tws = [_mm(t, jnp.concatenate([w["ar"][:LANES], akv], axis=1)).astype(BF16)
           for t, w, akv in zip(t_invs, work, akvs)]
    xs = [jnp.concatenate([tw, jnp.concatenate([zeros, w["v"]], axis=1)], axis=0)
          for tw, w in zip(tws, work)]
    qys = [_mm(rbk, x) for rbk, x in zip(rbks, xs)]
    zs = [_mm_tn(w["bk_end"], x) for w, x in zip(work, xs)]
    zas = [z[:, :LANES].astype(BF16) for z in zs]

    states = [None] * n
    for c in range(n_chunks):
        for j in range(N_PAIRS):
            i = j * n_chunks + c
            if c == 0:
                states[i] = s_scr[j]
            h0 = states[i]
            h1 = p_end[i] * h0 + _mm(zas[i], h0) + zs[i][:, LANES:]
            if c == n_chunks - 1:
                s_scr[j] = h1
            else:
                states[i + 1] = h1

    for j in range(N_PAIRS):
        ys = []
        for c in range(n_chunks):
            i = j * n_chunks + c
            qc = work[i]["ar"][LANES:] + qys[i][:, :LANES]
            y_p = _mm(qc, states[i]) + qys[i][:, LANES:]
            ys.append(y_p[:CHUNK] + y_p[CHUNK:])
        y = jnp.concatenate(ys, axis=0)
        inv_n = 1.0 / RW_HEAD
        yc = y - head_sum(y) * inv_n
        var = head_sum(yc * yc) * inv_n
        yn = yc * lax.rsqrt(var + RW_GN_EPS) * gnw_ref[j] + gnb_ref[j]
        y_ref[j] = ((yn + bonus[j]) * _silu(z_ref[j].astype(F32))).astype(y_ref.dtype)


def _rwkv_branch(prw, zrw, ltri, wda, mu, w0, a0, k_k, k_a, r_k, gn_w, gn_b, batch, seq_len):
    tt = REC_TT
    nt = seq_len // tt
    bt = batch * seq_len
    tok = lambda n: pl.BlockSpec((n, tt, LANES), lambda b, t: (0, b * nt + t, 0))
    par = pl.BlockSpec((N_PAIRS, 1, LANES), lambda b, t: (0, 0, 0))
    n_rw = RW_SHIFT_COLS // LANES
    in_specs = [
        tok(n_rw),
        tok(N_PAIRS),
        pl.BlockSpec((CUM_ROWS, CUM_ROWS), lambda b, t: (0, 0)),
        pl.BlockSpec((LANES, 2 * RW_WIDTH), lambda b, t: (0, 0)),
        pl.BlockSpec((n_rw, 1, LANES), lambda b, t: (0, 0, 0)),
        par, par, par, par, par, par, par,
    ]
    return pl.pallas_call(
        _rwkv_kernel,
        grid=(batch, nt),
        in_specs=in_specs,
        out_specs=tok(N_PAIRS),
        out_shape=jax.ShapeDtypeStruct((N_PAIRS, bt, LANES), BF16),
        scratch_shapes=[pltpu.VMEM((N_PAIRS, LANES, LANES), F32),
                        pltpu.VMEM((n_rw, CARRY_ROWS + tt, LANES), F32)],
        compiler_params=pltpu.CompilerParams(
            dimension_semantics=("arbitrary", "arbitrary"), vmem_limit_bytes=VMEM_LIMIT),
        name="rwkv7_chunked",
    )(prw, zrw, ltri, wda, mu, w0, a0, k_k, k_a, r_k, gn_w, gn_b)


def _gdn_kernel(pqkv_ref, z_ref, ba_ref, ltri_ref, cw_ref, alog_ref, dtb_ref, onw_ref,
                y_ref, s_scr, carry_scr):
    t_idx = pl.program_id(1)

    tt = ba_ref.shape[0]

    @pl.when(t_idx == 0)
    def _():
        s_scr[...] = jnp.zeros_like(s_scr)
        carry_scr[:, :CARRY_ROWS, :] = jnp.zeros((3 * GD_HEADS, CARRY_ROWS, LANES), F32)

    def conv_silu(g):
        cur = pqkv_ref[g].astype(F32)
        carry_scr[g, CARRY_ROWS:, :] = cur
        taps = cw_ref[g]
        acc = cur * taps[GD_CONV - 1:GD_CONV]
        for s in range(1, GD_CONV):
            acc = acc + carry_scr[g, pl.ds(CARRY_ROWS - s, tt), :] * taps[GD_CONV - 1 - s:GD_CONV - s]
        carry_scr[g, :CARRY_ROWS, :] = cur[tt - CARRY_ROWS:]
        return _silu(acc)

    n_super = tt // LANES
    per_super = LANES // CHUNK
    ri, ci, same = _tri_masks(LANES)
    strict = same & (ci < ri)
    causal = same & (ci <= ri)
    ba = ba_ref[...]
    g_all = -jnp.exp(alog_ref[...]) * _softplus(ba + dtb_ref[...])
    gc3 = _chunk_cumsum(ltri_ref[...], jnp.concatenate(_split3(g_all), axis=1))
    gc_all = gc3[:, :LANES] + gc3[:, LANES:2 * LANES] + gc3[:, 2 * LANES:]
    gc_all_t = gc_all.T

    work = []
    for h in range(GD_HEADS):
        q = conv_silu(h)
        k = conv_silu(GD_HEADS + h)
        v = conv_silu(2 * GD_HEADS + h)
        qn = q * (lax.rsqrt(jnp.sum(q * q, axis=-1, keepdims=True) + L2_EPS) * (GD_HEAD ** -0.5))
        kn = k * lax.rsqrt(jnp.sum(k * k, axis=-1, keepdims=True) + L2_EPS)
        beta = jnp.broadcast_to(_sigmoid(ba[:, h:h + 1]), (tt, LANES))
        gc_col = jnp.broadcast_to(gc_all[:, GD_HEADS + h:GD_HEADS + h + 1], (tt, LANES))
        gc_row = gc_all_t[GD_HEADS + h:GD_HEADS + h + 1, :]
        e_col = jnp.exp(gc_col)
        kb = kn * beta
        vb = v * beta
        qd = qn * e_col
        kbe = kb * e_col
        for sc in range(n_super):
            sl = slice(sc * LANES, (sc + 1) * LANES)
            diff = gc_col[sl] - gc_row[:, sl]
            chunks = []
            for cc in range(per_super):
                rows = slice(sc * LANES + cc * CHUNK, sc * LANES + (cc + 1) * CHUNK)
                g_last = gc_col[rows.stop - 1:rows.stop, :]
                chunks.append(dict(kte=(kn[rows] * jnp.exp(g_last - gc_col[rows])).astype(BF16),
                                   decay=jnp.exp(g_last)))
            work.append(dict(dec=jnp.exp(jnp.where(causal, diff, -1e30)),
                             kq=jnp.concatenate([kb[sl], qn[sl]], axis=0).astype(BF16),
                             kn=kn[sl].astype(BF16),
                             rhs=jnp.concatenate([vb[sl], kbe[sl]], axis=1).astype(BF16),
                             qd=qd[sl], chunks=chunks))

    ms = [_mm_nt(w["kq"], w["kn"]) for w in work]
    lmats = [-jnp.where(strict, m[:LANES] * w["dec"], 0.0) for m, w in zip(ms, work)]
    qks = [(m[LANES:] * w["dec"]).astype(BF16) for m, w in zip(ms, work)]
    t_invs = _unit_lower_inverses(lmats, ri, ci)
    uws = [_mm(t, w["rhs"]).astype(BF16) for t, w in zip(t_invs, work)]
    qws = [_mm(qk, uw) for qk, uw in zip(qks, uws)]
    for w, uw, qw in zip(work, uws, qws):
        w["qc"] = (w["qd"] - qw[:, LANES:]).astype(BF16)
        w["o0"] = qw[:, :LANES]
        for cc, ch in enumerate(w["chunks"]):
            tn = _mm_tn(ch["kte"], uw[cc * CHUNK:(cc + 1) * CHUNK])
            ch["tn_u"] = tn[:, :LANES]
            ch["tn_w"] = tn[:, LANES:].astype(BF16)

    state = [s_scr[h] for h in range(GD_HEADS)]
    for sc in range(n_super):
        for cc in range(per_super):
            for h in range(GD_HEADS):
                ch = work[h * n_super + sc]["chunks"][cc]
                s0 = state[h]
                ch["s0"] = s0
                state[h] = s0 * ch["decay"] - _mm(ch["tn_w"], s0) + ch["tn_u"]
    for h in range(GD_HEADS):
        s_scr[h] = state[h]

    for h in range(GD_HEADS):
        os = []
        for sc in range(n_super):
            w = work[h * n_super + sc]
            for cc, ch in enumerate(w["chunks"]):
                rows = slice(cc * CHUNK, (cc + 1) * CHUNK)
                os.append(_mm(w["qc"][rows], ch["s0"]) + w["o0"][rows])
        o = jnp.concatenate(os, axis=0)
        ms = jnp.mean(o * o, axis=-1, keepdims=True)
        on = o * lax.rsqrt(ms + NORM_EPS) * onw_ref[...]
        y_ref[h] = (on * _silu(z_ref[h].astype(F32))).astype(y_ref.dtype)


def _gdn_branch(pqkv, zgd, ba, ltri, conv_w, alog, dtb, onw, batch, seq_len):
    tt = REC_TT
    nt = seq_len // tt
    bt = batch * seq_len
    tok = lambda n: pl.BlockSpec((n, tt, LANES), lambda b, t: (0, b * nt + t, 0))
    par = pl.BlockSpec((1, LANES), lambda b, t: (0, 0))
    in_specs = [
        tok(3 * GD_HEADS),
        tok(GD_HEADS),
        pl.BlockSpec((tt, LANES), lambda b, t: (b * nt + t, 0)),
        pl.BlockSpec((CUM_ROWS, CUM_ROWS), lambda b, t: (0, 0)),
        pl.BlockSpec((3 * GD_HEADS, GD_CONV, LANES), lambda b, t: (0, 0, 0)),
        par, par,
        pl.BlockSpec((1, LANES), lambda b, t: (0, 0)),
    ]
    return pl.pallas_call(
        _gdn_kernel,
        grid=(batch, nt),
        in_specs=in_specs,
        out_specs=tok(GD_HEADS),
        out_shape=jax.ShapeDtypeStruct((GD_HEADS, bt, LANES), BF16),
        scratch_shapes=[pltpu.VMEM((GD_HEADS, LANES, LANES), F32),
                        pltpu.VMEM((3 * GD_HEADS, CARRY_ROWS + tt, LANES), F32)],
        compiler_params=pltpu.CompilerParams(
            dimension_semantics=("arbitrary", "arbitrary"), vmem_limit_bytes=VMEM_LIMIT),
        name="gated_delta_chunked",
    )(pqkv, zgd, ba, ltri, conv_w, alog, dtb, onw)


def _merge_kernel(x_ref, ya_ref, yb_ref, gates_ref, wa_ref, wb_ref, wo_ref, g_ref, o_ref):
    ya = jnp.concatenate([ya_ref[j] for j in range(N_PAIRS)], axis=1)
    yb = jnp.concatenate([yb_ref[j] for j in range(GD_HEADS)], axis=1)
    ma = jnp.dot(ya, wa_ref[...], preferred_element_type=F32)
    mb = jnp.dot(yb, wb_ref[...], preferred_element_type=F32)
    gates = gates_ref[...].astype(F32)
    merged = _sigmoid(gates[:, :D_MODEL]) * ma + _sigmoid(gates[:, D_MODEL:]) * mb
    res = x_ref[...] + jnp.dot(merged.astype(BF16), wo_ref[...], preferred_element_type=F32)
    ms = jnp.mean(res * res, axis=-1, keepdims=True)
    o_ref[...] = res * lax.rsqrt(ms + NORM_EPS) * g_ref[...]


def _merge(x2, ya, yb, gates, w_a, w_b, w_o, norm_w):
    bt = x2.shape[0]
    tm = MERGE_TM
    const = lambda i: (0, 0)
    in_specs = [
        pl.BlockSpec((tm, D_MODEL), lambda i: (i, 0)),
        pl.BlockSpec((N_PAIRS, tm, LANES), lambda i: (0, i, 0)),
        pl.BlockSpec((GD_HEADS, tm, LANES), lambda i: (0, i, 0)),
        pl.BlockSpec((tm, 2 * D_MODEL), lambda i: (i, 0)),
        pl.BlockSpec((RW_WIDTH, D_MODEL), const),
        pl.BlockSpec((GD_WIDTH, D_MODEL), const),
        pl.BlockSpec((D_MODEL, D_MODEL), const),
        pl.BlockSpec((1, D_MODEL), const),
    ]
    return pl.pallas_call(
        _merge_kernel,
        grid=(bt // tm,),
        in_specs=in_specs,
        out_specs=pl.BlockSpec((tm, D_MODEL), lambda i: (i, 0)),
        out_shape=jax.ShapeDtypeStruct((bt, D_MODEL), F32),
        compiler_params=pltpu.CompilerParams(
            dimension_semantics=("arbitrary",), vmem_limit_bytes=VMEM_LIMIT),
        name="merge_out_proj",
    )(x2, ya, yb, gates, w_a, w_b, w_o, norm_w)


def _block_lower_tri(n):
    i = jnp.arange(n)
    return (((i[:, None] // CHUNK) == (i[None, :] // CHUNK)) & (i[None, :] <= i[:, None])).astype(BF16)


def _layer(x2, batch, seq_len, norm_in_w, w_in, rw_mu, rw_w0, rw_w2, rw_a0, rw_a2, rw_k_k,
           rw_k_a, rw_r_k, rw_gn_w, rw_gn_b, gd_conv_w, gd_A_log, gd_dt_bias, gd_o_norm_w,
           w_branch_a, w_branch_b, w_out):
    c_gates = C_GATES + 2 * GD_HEADS
    w_main = w_in[:, :C_GATES].astype(BF16)
    w_gates = w_in[:, c_gates:].astype(BF16)
    w_ba = jnp.pad(w_in[:, C_GATES:c_gates], ((0, 0), (0, LANES - 2 * GD_HEADS))).astype(BF16)

    prw, zrw, pqkv, zgd, gates, ba = _projection(
        x2, norm_in_w.reshape(1, D_MODEL), w_main, w_gates, w_ba)

    ltri = _block_lower_tri(CUM_ROWS)
    pair = lambda p: p.reshape(N_PAIRS, 1, LANES)
    zero = jnp.zeros((RW_RANK, RW_WIDTH), F32)
    wda = jnp.concatenate([jnp.concatenate([rw_w2, zero], axis=1),
                           jnp.concatenate([zero, rw_a2], axis=1)], axis=0).astype(BF16)
    mu = rw_mu.reshape(RW_SHIFT_COLS // LANES, 1, LANES)
    ya = _rwkv_branch(prw, zrw, ltri, wda, mu, pair(rw_w0), pair(rw_a0), pair(rw_k_k),
                      pair(rw_k_a), pair(rw_r_k.reshape(RW_WIDTH)), pair(rw_gn_w),
                      pair(rw_gn_b), batch, seq_len)

    head = lambda p: jnp.pad(p.astype(F32), (GD_HEADS, LANES - 2 * GD_HEADS)).reshape(1, LANES)
    conv_w = jnp.swapaxes(gd_conv_w.reshape(GD_CONV, 3 * GD_HEADS, LANES), 0, 1)
    yb = _gdn_branch(pqkv, zgd, ba, ltri, conv_w, head(gd_A_log), head(gd_dt_bias),
                     gd_o_norm_w.reshape(1, GD_HEAD), batch, seq_len)
    return gates, ya, yb


def kernel(x, norm_in_w, w_in, rw_mu, rw_w0, rw_w2, rw_a0, rw_a2, rw_k_k, rw_k_a, rw_r_k,
           rw_gn_w, rw_gn_b, gd_conv_w, gd_A_log, gd_dt_bias, gd_o_norm_w,
           w_branch_a, w_branch_b, w_out, norm_out_w):
    batch, seq_len, d = x.shape
    assert d == D_MODEL and norm_in_w.shape[0] == 1
    assert seq_len % max(PROJ_TM, REC_TT) == 0 and (batch * seq_len) % MERGE_TM == 0
    x2 = x.reshape(batch * seq_len, d)
    gates, ya, yb = _layer(
        x2, batch, seq_len, norm_in_w[0], w_in[0], rw_mu[0], rw_w0[0], rw_w2[0], rw_a0[0],
        rw_a2[0], rw_k_k[0], rw_k_a[0], rw_r_k[0], rw_gn_w[0], rw_gn_b[0], gd_conv_w[0],
        gd_A_log[0], gd_dt_bias[0], gd_o_norm_w[0], w_branch_a[0], w_branch_b[0], w_out[0])
    out = _merge(x2, ya, yb, gates, w_branch_a[0].astype(BF16), w_branch_b[0].astype(BF16),
                 w_out[0].astype(BF16), norm_out_w.reshape(1, D_MODEL))
    return out.reshape(batch, seq_len, d)
```

```python
import jax
import jax.numpy as jnp
from jax import lax
from jax.experimental import pallas as pl
from jax.experimental.pallas import tpu as pltpu

F32 = jnp.float32
BF16 = jnp.bfloat16

D_MODEL = 1024
RW_WIDTH = 512
RW_HEAD = 64
RW_RANK = 64
RW_GN_EPS = RW_HEAD * 1e-5
DECAY_SCALE = -0.6065306597126334
GD_WIDTH = 512
GD_HEAD = 128
GD_HEADS = GD_WIDTH // GD_HEAD
GD_CONV = 4
NORM_EPS = 1e-6
L2_EPS = 1e-12

LANES = 128
CHUNK = 64
N_PAIRS = RW_WIDTH // LANES
RW_SHIFT_COLS = 3 * RW_WIDTH + 2 * RW_RANK
C_ZRW = RW_SHIFT_COLS
C_QKV = C_ZRW + RW_WIDTH
C_ZGD = C_QKV + 3 * GD_WIDTH
C_GATES = C_ZGD + GD_WIDTH

PROJ_TM = 512
CARRY_ROWS = 8
REC_TT = 512
CUM_ROWS = 256
MERGE_TM = 512
VMEM_LIMIT = 56 * 1024 * 1024


def _mm(a, b):
    return jnp.dot(a.astype(BF16), b.astype(BF16), preferred_element_type=F32)


def _mm_nt(a, b):
    return lax.dot_general(a.astype(BF16), b.astype(BF16), (((1,), (1,)), ((), ())),
                           preferred_element_type=F32)


def _mm_tn(a, b):
    return lax.dot_general(a.astype(BF16), b.astype(BF16), (((0,), (0,)), ((), ())),
                           preferred_element_type=F32)


def _sigmoid(x):
    return 0.5 * jnp.tanh(0.5 * x) + 0.5


def _silu(x):
    h = 0.5 * x
    return h * jnp.tanh(h) + h


def _softplus(x):
    return jnp.maximum(x, 0.0) + jnp.log(1.0 + jnp.exp(-jnp.abs(x)))


def _split3(x):
    hi = x.astype(BF16)
    r1 = x - hi.astype(F32)
    mid = r1.astype(BF16)
    lo = (r1 - mid.astype(F32)).astype(BF16)
    return hi, mid, lo


def _blk(idx, size):
    return lax.shift_right_logical(idx, size.bit_length() - 1)


def _tri_masks(n):
    ri = lax.broadcasted_iota(jnp.int32, (n, n), 0)
    ci = lax.broadcasted_iota(jnp.int32, (n, n), 1)
    same = _blk(ri, CHUNK) == _blk(ci, CHUNK)
    return ri, ci, same


def _unit_lower_inverses(lmats, ri, ci):
    eye = (ri == ci).astype(F32)
    blk8 = _blk(ri, 8) == _blk(ci, 8)
    n = lmats[0].shape[0]
    ds = [jnp.where(blk8, l, 0.0) for l in lmats]
    dbs = [d.astype(BF16) for d in ds]
    d2s = [_mm(d, d).astype(BF16) for d in dbs]
    ts = [eye + d for d in ds]
    xs = [_mm(d2, jnp.concatenate([t.astype(BF16), d2], axis=1)) for t, d2 in zip(ts, d2s)]
    ts = [t + x[:, :n] for t, x in zip(ts, xs)]
    ts = [t + _mm(x[:, n:], t) for t, x in zip(ts, xs)]
    s = 8
    while s < CHUNK:
        groups = range(n // (2 * s))
        lower = lambda x: jnp.concatenate(
            [x[g * 2 * s + s:(g + 1) * 2 * s] for g in groups], axis=0)
        zeros = jnp.zeros((s, n), F32)
        rc = lax.broadcasted_iota(jnp.int32, (n // 2, n), 0)
        cl = lax.broadcasted_iota(jnp.int32, (n // 2, n), 1)
        rl = _blk(rc, s) * (2 * s) + s + (rc & (s - 1))
        off = (_blk(rl, 2 * s) == _blk(cl, 2 * s)) & (_blk(rl, s) != _blk(cl, s))
        tbs = [t.astype(BF16) for t in ts]
        xs = [_mm(jnp.where(off, lower(l), 0.0), t) for l, t in zip(lmats, tbs)]
        xs = [jnp.concatenate([p for g in groups for p in (zeros, x[g * s:(g + 1) * s])], axis=0)
              for x in xs]
        ys = [_mm(lower(t), x) for t, x in zip(ts, xs)]
        ts = [jnp.concatenate([p for g in groups
                               for p in (t[g * 2 * s:g * 2 * s + s],
                                         t[g * 2 * s + s:(g + 1) * 2 * s] + y[g * s:(g + 1) * s])],
                              axis=0)
              for t, y in zip(ts, ys)]
        s *= 2
    return [t.astype(BF16) for t in ts]


def _chunk_cumsum(ltri, x):
    return jnp.concatenate([_mm(ltri, x[r0:r0 + CUM_ROWS]) for r0 in range(0, x.shape[0], CUM_ROWS)],
                           axis=0)


def _proj_kernel(x_ref, g_ref, w_ref, wg_ref, wba_ref,
                 prw_ref, zrw_ref, pqkv_ref, zgd_ref, gates_ref, ba_ref):
    xc = x_ref[...]
    inv = lax.rsqrt(jnp.mean(xc * xc, axis=-1, keepdims=True) + NORM_EPS)
    xb = (xc * g_ref[...]).astype(BF16)

    def proj(w):
        return jnp.dot(xb, w, preferred_element_type=F32) * inv

    def lane_groups(c0, n_groups, out_ref):
        for g0 in range(0, n_groups, 4):
            ng = min(4, n_groups - g0)
            pe = proj(w_ref[:, c0 + g0 * LANES:c0 + (g0 + ng) * LANES])
            for j in range(ng):
                out_ref[g0 + j] = pe[:, j * LANES:(j + 1) * LANES].astype(BF16)

    lane_groups(0, RW_SHIFT_COLS // LANES, prw_ref)
    lane_groups(C_ZRW, N_PAIRS, zrw_ref)
    lane_groups(C_QKV, 3 * GD_HEADS, pqkv_ref)
    lane_groups(C_ZGD, GD_HEADS, zgd_ref)
    for c0 in range(0, 2 * D_MODEL, 512):
        gates_ref[:, c0:c0 + 512] = proj(wg_ref[:, c0:c0 + 512]).astype(BF16)
    ba_ref[...] = proj(wba_ref[...])


def _projection(x2, norm_w, w_main, w_gates, w_ba):
    bt = x2.shape[0]
    tm = PROJ_TM
    const = lambda i: (0, 0)
    slabs = lambda n: pl.BlockSpec((n, tm, LANES), lambda i: (0, i, 0))
    n_rw = RW_SHIFT_COLS // LANES
    out_shapes = (
        jax.ShapeDtypeStruct((n_rw, bt, LANES), BF16),
        jax.ShapeDtypeStruct((N_PAIRS, bt, LANES), BF16),
        jax.ShapeDtypeStruct((3 * GD_HEADS, bt, LANES), BF16),
        jax.ShapeDtypeStruct((GD_HEADS, bt, LANES), BF16),
        jax.ShapeDtypeStruct((bt, 2 * D_MODEL), BF16),
        jax.ShapeDtypeStruct((bt, LANES), F32),
    )
    out_specs = (
        slabs(n_rw), slabs(N_PAIRS), slabs(3 * GD_HEADS), slabs(GD_HEADS),
        pl.BlockSpec((tm, 2 * D_MODEL), lambda i: (i, 0)),
        pl.BlockSpec((tm, LANES), lambda i: (i, 0)),
    )
    in_specs = [
        pl.BlockSpec((tm, D_MODEL), lambda i: (i, 0)),
        pl.BlockSpec((1, D_MODEL), const),
        pl.BlockSpec((D_MODEL, C_GATES), const, pipeline_mode=pl.Buffered(1)),
        pl.BlockSpec((D_MODEL, 2 * D_MODEL), const, pipeline_mode=pl.Buffered(1)),
        pl.BlockSpec((D_MODEL, LANES), const),
    ]
    return pl.pallas_call(
        _proj_kernel,
        grid=(bt // tm,),
        in_specs=in_specs,
        out_specs=out_specs,
        out_shape=out_shapes,
        compiler_params=pltpu.CompilerParams(
            dimension_semantics=("arbitrary",), vmem_limit_bytes=VMEM_LIMIT),
        name="rmsnorm_in_proj",
    )(x2, norm_w, w_main, w_gates, w_ba)


def _rwkv_kernel(prw_ref, z_ref, ltri_ref, wda_ref, mu_ref, w0_ref, a0_ref,
                 kk_ref, ka_ref, rk_ref, gnw_ref, gnb_ref, y_ref, s_scr, carry_scr):
    t_idx = pl.program_id(1)

    tt = z_ref.shape[1]
    n_chunks = tt // CHUNK

    @pl.when(t_idx == 0)
    def _():
        s_scr[...] = jnp.zeros_like(s_scr)
        carry_scr[...] = jnp.zeros_like(carry_scr)

    def token_shift(g):
        cur = prw_ref[g].astype(F32)
        prev = pltpu.roll(jnp.concatenate([carry_scr[g], cur], axis=0), 1, 0)[CARRY_ROWS:]
        carry_scr[g] = cur[tt - CARRY_ROWS:]
        return cur + (prev - cur) * mu_ref[g]

    lane = lax.broadcasted_iota(jnp.int32, (1, LANES), 1)
    head_a = lane < RW_HEAD
    ri, ci, same = _tri_masks(LANES)
    strict = same & (ci < ri)
    causal = same & (ci <= ri)

    def head_sum(x):
        sa = jnp.sum(jnp.where(head_a, x, 0.0), axis=-1, keepdims=True)
        sb = jnp.sum(jnp.where(head_a, 0.0, x), axis=-1, keepdims=True)
        return jnp.where(head_a, jnp.broadcast_to(sa, x.shape), jnp.broadcast_to(sb, x.shape))
    ltri = ltri_ref[...]

    def pair_rows(x):
        return jnp.concatenate([jnp.where(head_a, x, 0.0), jnp.where(head_a, 0.0, x)], axis=0)

    wa = token_shift(3 * N_PAIRS)
    da = jnp.where(lane < RW_RANK, jnp.tanh(wa), wa)
    pre = _mm(da, wda_ref[...])

    bonus, p_end, work = [], [], []
    for j in range(N_PAIRS):
        r = token_shift(j)
        k = token_shift(N_PAIRS + j)
        v = token_shift(2 * N_PAIRS + j)
        cols = slice(j * LANES, (j + 1) * LANES)
        lw = DECAY_SCALE * _sigmoid(w0_ref[j] + pre[:, cols])
        lr = _sigmoid(a0_ref[j] + pre[:, RW_WIDTH + j * LANES:RW_WIDTH + (j + 1) * LANES])
        kkr = k * kk_ref[j]
        kk = kkr * lax.rsqrt(head_sum(kkr * kkr) + L2_EPS)
        k2 = k * (1.0 + (lr - 1.0) * ka_ref[j])
        b = kk * lr
        bonus.append(head_sum(r * k2 * rk_ref[j]) * v)
        lw_hi = lw.astype(BF16)
        lw_lo = (lw - lw_hi.astype(F32)).astype(BF16)
        cl2 = _chunk_cumsum(ltri, jnp.concatenate([lw_hi, lw_lo], axis=1))
        clw = cl2[:, :LANES] + cl2[:, LANES:]
        e_neg = jnp.exp(-clw)
        rt = r * jnp.exp(clw)
        at = -kk * jnp.exp(clw - lw)
        bt = b * e_neg
        kt = k2 * e_neg
        for c in range(n_chunks):
            sl = slice(c * CHUNK, (c + 1) * CHUNK)
            p_c = jnp.exp(clw[(c + 1) * CHUNK - 1:(c + 1) * CHUNK, :])
            p_end.append(jnp.sum(jnp.where(ri == ci, p_c, 0.0), axis=-1, keepdims=True))
            work.append(dict(
                ar=jnp.concatenate([pair_rows(at[sl]), pair_rows(rt[sl])], axis=0).astype(BF16),
                bk=jnp.concatenate([bt[sl], bt[sl], kt[sl], kt[sl]], axis=0).astype(BF16),
                v=pair_rows(v[sl]).astype(BF16),
                bk_end=jnp.concatenate([pair_rows(bt[sl] * p_c), pair_rows(kt[sl] * p_c)],
                                       axis=0).astype(BF16)))

    n = len(work)
    causal2 = jnp.concatenate([causal, causal], axis=1)
    zeros = jnp.zeros((LANES, LANES), BF16)
    ms = [_mm_nt(w["ar"], w["bk"]) for w in work]
    lbs = [jnp.where(strict, m[:LANES, :LANES], 0.0) for m in ms]
    aks = [jnp.where(strict, m[:LANES, LANES:], 0.0).astype(BF16) for m in ms]
    rbks = [jnp.where(causal2, m[LANES:], 0.0).astype(BF16) for m in ms]
    t_invs = _unit_lower_inverses(lbs, ri, ci)
    akvs = [_mm(ak, w["v"]).astype(BF16) for ak, w in zip(aks, work)]
    tws = [_mm(t, jnp.concatenate([w["ar"][:LANES], akv], axis=1)).astype(BF16)
           for t, w, akv in zip(t_invs, work, akvs)]
    xs = [jnp.concatenate([tw, jnp.concatenate([zeros, w["v"]], axis=1)], axis=0)
          for tw, w in zip(tws, work)]
    qys = [_mm(rbk, x) for rbk, x in zip(rbks, xs)]
    zs = [_mm_tn(w["bk_end"], x) for w, x in zip(work, xs)]
    zas = [z[:, :LANES].astype(BF16) for z in zs]

    states = [None] * n
    for c in range(n_chunks):
        for j in range(N_PAIRS):
            i = j * n_chunks + c
            if c == 0:
                states[i] = s_scr[j]
            h0 = states[i]
            h1 = p_end[i] * h0 + _mm(zas[i], h0) + zs[i][:, LANES:]
            if c == n_chunks - 1:
                s_scr[j] = h1
            else:
                states[i + 1] = h1

    for j in range(N_PAIRS):
        ys = []
        for c in range(n_chunks):
            i = j * n_chunks + c
            qc = work[i]["ar"][LANES:] + qys[i][:, :LANES]
            y_p = _mm(qc, states[i]) + qys[i][:, LANES:]
            ys.append(y_p[:CHUNK] + y_p[CHUNK:])
        y = jnp.concatenate(ys, axis=0)
        inv_n = 1.0 / RW_HEAD
        yc = y - head_sum(y) * inv_n
        var = head_sum(yc * yc) * inv_n
        yn = yc * lax.rsqrt(var + RW_GN_EPS) * gnw_ref[j] + gnb_ref[j]
        y_ref[j] = ((yn + bonus[j]) * _silu(z_ref[j].astype(F32))).astype(y_ref.dtype)


def _rwkv_branch(prw, zrw, ltri, wda, mu, w0, a0, k_k, k_a, r_k, gn_w, gn_b, batch, seq_len):
    tt = REC_TT
    nt = seq_len // tt
    bt = batch * seq_len
    tok = lambda n: pl.BlockSpec((n, tt, LANES), lambda b, t: (0, b * nt + t, 0))
    par = pl.BlockSpec((N_PAIRS, 1, LANES), lambda b, t: (0, 0, 0))
    n_rw = RW_SHIFT_COLS // LANES
    in_specs = [
        tok(n_rw),
        tok(N_PAIRS),
        pl.BlockSpec((CUM_ROWS, CUM_ROWS), lambda b, t: (0, 0)),
        pl.BlockSpec((LANES, 2 * RW_WIDTH), lambda b, t: (0, 0)),
        pl.BlockSpec((n_rw, 1, LANES), lambda b, t: (0, 0, 0)),
        par, par, par, par, par, par, par,
    ]
    return pl.pallas_call(
        _rwkv_kernel,
        grid=(batch, nt),
        in_specs=in_specs,
        out_specs=tok(N_PAIRS),
        out_shape=jax.ShapeDtypeStruct((N_PAIRS, bt, LANES), BF16),
        scratch_shapes=[pltpu.VMEM((N_PAIRS, LANES, LANES), F32),
                        pltpu.VMEM((n_rw, CARRY_ROWS, LANES), F32)],
        compiler_params=pltpu.CompilerParams(
            dimension_semantics=("arbitrary", "arbitrary"), vmem_limit_bytes=VMEM_LIMIT),
        name="rwkv7_chunked",
    )(prw, zrw, ltri, wda, mu, w0, a0, k_k, k_a, r_k, gn_w, gn_b)


def _gdn_kernel(pqkv_ref, z_ref, ba_ref, ltri_ref, cw_ref, alog_ref, dtb_ref, onw_ref,
                y_ref, s_scr, carry_scr):
    t_idx = pl.program_id(1)

    tt = ba_ref.shape[0]

    @pl.when(t_idx == 0)
    def _():
        s_scr[...] = jnp.zeros_like(s_scr)
        carry_scr[:, :CARRY_ROWS, :] = jnp.zeros((3 * GD_HEADS, CARRY_ROWS, LANES), F32)

    for g in range(3 * GD_HEADS):
        carry_scr[g, CARRY_ROWS:, :] = pqkv_ref[g].astype(F32)

    def conv_silu(g):
        cur = carry_scr[g, CARRY_ROWS:, :]
        taps = cw_ref[g]
        acc = cur * taps[GD_CONV - 1:GD_CONV]
        for s in range(1, GD_CONV):
            acc = acc + carry_scr[g, pl.ds(CARRY_ROWS - s, tt), :] * taps[GD_CONV - 1 - s:GD_CONV - s]
        carry_scr[g, :CARRY_ROWS, :] = cur[tt - CARRY_ROWS:]
        return _silu(acc)

    n_super = tt // LANES
    per_super = LANES // CHUNK
    ri, ci, same = _tri_masks(LANES)
    strict = same & (ci < ri)
    causal = same & (ci <= ri)
    ba = ba_ref[...]
    g_all = -jnp.exp(alog_ref[...]) * _softplus(ba + dtb_ref[...])
    gc3 = _chunk_cumsum(ltri_ref[...], jnp.concatenate(_split3(g_all), axis=1))
    gc_all = gc3[:, :LANES] + gc3[:, LANES:2 * LANES] + gc3[:, 2 * LANES:]
    gc_all_t = gc_all.T

    work = []
    for h in range(GD_HEADS):
        q = conv_silu(h)
        k = conv_silu(GD_HEADS + h)
        v = conv_silu(2 * GD_HEADS + h)
        qn = q * (lax.rsqrt(jnp.sum(q * q, axis=-1, keepdims=True) + L2_EPS) * (GD_HEAD ** -0.5))
        kn = k * lax.rsqrt(jnp.sum(k * k, axis=-1, keepdims=True) + L2_EPS)
        beta = jnp.broadcast_to(_sigmoid(ba[:, h:h + 1]), (tt, LANES))
        gc_col = jnp.broadcast_to(gc_all[:, GD_HEADS + h:GD_HEADS + h + 1], (tt, LANES))
        gc_row = gc_all_t[GD_HEADS + h:GD_HEADS + h + 1, :]
        e_col = jnp.exp(gc_col)
        kb = kn * beta
        vb = v * beta
        qd = qn * e_col
        kbe = kb * e_col
        for sc in range(n_super):
            sl = slice(sc * LANES, (sc + 1) * LANES)
            diff = gc_col[sl] - gc_row[:, sl]
            chunks = []
            for cc in range(per_super):
                rows = slice(sc * LANES + cc * CHUNK, sc * LANES + (cc + 1) * CHUNK)
                g_last = gc_col[rows.stop - 1:rows.stop, :]
                chunks.append(dict(kte=(kn[rows] * jnp.exp(g_last - gc_col[rows])).astype(BF16),
                                   decay=jnp.exp(g_last)))
            work.append(dict(dec=jnp.exp(jnp.where(causal, diff, -1e30)),
                             kq=jnp.concatenate([kb[sl], qn[sl]], axis=0).astype(BF16),
                             kn=kn[sl].astype(BF16),
                             rhs=jnp.concatenate([vb[sl], kbe[sl]], axis=1).astype(BF16),
                             qd=qd[sl], chunks=chunks))

    ms = [_mm_nt(w["kq"], w["kn"]) for w in work]
    lmats = [-jnp.where(strict, m[:LANES] * w["dec"], 0.0) for m, w in zip(ms, work)]
    qks = [(m[LANES:] * w["dec"]).astype(BF16) for m, w in zip(ms, work)]
    t_invs = _unit_lower_inverses(lmats, ri, ci)
    uws = [_mm(t, w["rhs"]).astype(BF16) for t, w in zip(t_invs, work)]
    qws = [_mm(qk, uw) for qk, uw in zip(qks, uws)]
    for w, uw, qw in zip(work, uws, qws):
        w["qc"] = (w["qd"] - qw[:, LANES:]).astype(BF16)
        w["o0"] = qw[:, :LANES]
        for cc, ch in enumerate(w["chunks"]):
            tn = _mm_tn(ch["kte"], uw[cc * CHUNK:(cc + 1) * CHUNK])
            ch["tn_u"] = tn[:, :LANES]
            ch["tn_w"] = tn[:, LANES:].astype(BF16)

    state = [s_scr[h] for h in range(GD_HEADS)]
    for sc in range(n_super):
        for cc in range(per_super):
            for h in range(GD_HEADS):
                ch = work[h * n_super + sc]["chunks"][cc]
                s0 = state[h]
                ch["s0"] = s0
                state[h] = s0 * ch["decay"] - _mm(ch["tn_w"], s0) + ch["tn_u"]
    for h in range(GD_HEADS):
        s_scr[h] = state[h]

    for h in range(GD_HEADS):
        os = []
        for sc in range(n_super):
            w = work[h * n_super + sc]
            for cc, ch in enumerate(w["chunks"]):
                rows = slice(cc * CHUNK, (cc + 1) * CHUNK)
                os.append(_mm(w["qc"][rows], ch["s0"]) + w["o0"][rows])
        o = jnp.concatenate(os, axis=0)
        ms = jnp.mean(o * o, axis=-1, keepdims=True)
        on = o * lax.rsqrt(ms + NORM_EPS) * onw_ref[...]
        y_ref[h] = (on * _silu(z_ref[h].astype(F32))).astype(y_ref.dtype)


def _gdn_branch(pqkv, zgd, ba, ltri, conv_w, alog, dtb, onw, batch, seq_len):
    tt = REC_TT
    nt = seq_len // tt
    bt = batch * seq_len
    tok = lambda n: pl.BlockSpec((n, tt, LANES), lambda b, t: (0, b * nt + t, 0))
    par = pl.BlockSpec((1, LANES), lambda b, t: (0, 0))
    in_specs = [
        tok(3 * GD_HEADS),
        tok(GD_HEADS),
        pl.BlockSpec((tt, LANES), lambda b, t: (b * nt + t, 0)),
        pl.BlockSpec((CUM_ROWS, CUM_ROWS), lambda b, t: (0, 0)),
        pl.BlockSpec((3 * GD_HEADS, GD_CONV, LANES), lambda b, t: (0, 0, 0)),
        par, par,
        pl.BlockSpec((1, LANES), lambda b, t: (0, 0)),
    ]
    return pl.pallas_call(
        _gdn_kernel,
        grid=(batch, nt),
        in_specs=in_specs,
        out_specs=tok(GD_HEADS),
        out_shape=jax.ShapeDtypeStruct((GD_HEADS, bt, LANES), BF16),
        scratch_shapes=[pltpu.VMEM((GD_HEADS, LANES, LANES), F32),
                        pltpu.VMEM((3 * GD_HEADS, CARRY_ROWS + tt, LANES), F32)],
        compiler_params=pltpu.CompilerParams(
            dimension_semantics=("arbitrary", "arbitrary"), vmem_limit_bytes=VMEM_LIMIT),
        name="gated_delta_chunked",
    )(pqkv, zgd, ba, ltri, conv_w, alog, dtb, onw)


def _merge_kernel(x_ref, ya_ref, yb_ref, gates_ref, wa_ref, wb_ref, wo_ref, g_ref, o_ref):
    ya = jnp.concatenate([ya_ref[j] for j in range(N_PAIRS)], axis=1)
    yb = jnp.concatenate([yb_ref[j] for j in range(GD_HEADS)], axis=1)
    ma = jnp.dot(ya, wa_ref[...], preferred_element_type=F32)
    mb = jnp.dot(yb, wb_ref[...], preferred_element_type=F32)
    gates = gates_ref[...].astype(F32)
    merged = _sigmoid(gates[:, :D_MODEL]) * ma + _sigmoid(gates[:, D_MODEL:]) * mb
    res = x_ref[...] + jnp.dot(merged.astype(BF16), wo_ref[...], preferred_element_type=F32)
    ms = jnp.mean(res * res, axis=-1, keepdims=True)
    o_ref[...] = res * lax.rsqrt(ms + NORM_EPS) * g_ref[...]


def _merge(x2, ya, yb, gates, w_a, w_b, w_o, norm_w):
    bt = x2.shape[0]
    tm = MERGE_TM
    const = lambda i: (0, 0)
    in_specs = [
        pl.BlockSpec((tm, D_MODEL), lambda i: (i, 0)),
        pl.BlockSpec((N_PAIRS, tm, LANES), lambda i: (0, i, 0)),
        pl.BlockSpec((GD_HEADS, tm, LANES), lambda i: (0, i, 0)),
        pl.BlockSpec((tm, 2 * D_MODEL), lambda i: (i, 0)),
        pl.BlockSpec((RW_WIDTH, D_MODEL), const),
        pl.BlockSpec((GD_WIDTH, D_MODEL), const),
        pl.BlockSpec((D_MODEL, D_MODEL), const),
        pl.BlockSpec((1, D_MODEL), const),
    ]
    return pl.pallas_call(
        _merge_kernel,
        grid=(bt // tm,),
        in_specs=in_specs,
        out_specs=pl.BlockSpec((tm, D_MODEL), lambda i: (i, 0)),
        out_shape=jax.ShapeDtypeStruct((bt, D_MODEL), F32),
        compiler_params=pltpu.CompilerParams(
            dimension_semantics=("arbitrary",), vmem_limit_bytes=VMEM_LIMIT),
        name="merge_out_proj",
    )(x2, ya, yb, gates, w_a, w_b, w_o, norm_w)


def _block_lower_tri(n):
    i = jnp.arange(n)
    return (((i[:, None] // CHUNK) == (i[None, :] // CHUNK)) & (i[None, :] <= i[:, None])).astype(BF16)


def _layer(x2, batch, seq_len, norm_in_w, w_in, rw_mu, rw_w0, rw_w2, rw_a0, rw_a2, rw_k_k,
           rw_k_a, rw_r_k, rw_gn_w, rw_gn_b, gd_conv_w, gd_A_log, gd_dt_bias, gd_o_norm_w,
           w_branch_a, w_branch_b, w_out):
    c_gates = C_GATES + 2 * GD_HEADS
    w_main = w_in[:, :C_GATES].astype(BF16)
    w_gates = w_in[:, c_gates:].astype(BF16)
    w_ba = jnp.pad(w_in[:, C_GATES:c_gates], ((0, 0), (0, LANES - 2 * GD_HEADS))).astype(BF16)

    prw, zrw, pqkv, zgd, gates, ba = _projection(
        x2, norm_in_w.reshape(1, D_MODEL), w_main, w_gates, w_ba)

    ltri = _block_lower_tri(CUM_ROWS)
    pair = lambda p: p.reshape(N_PAIRS, 1, LANES)
    zero = jnp.zeros((RW_RANK, RW_WIDTH), F32)
    wda = jnp.concatenate([jnp.concatenate([rw_w2, zero], axis=1),
                           jnp.concatenate([zero, rw_a2], axis=1)], axis=0).astype(BF16)
    mu = rw_mu.reshape(RW_SHIFT_COLS // LANES, 1, LANES)
    ya = _rwkv_branch(prw, zrw, ltri, wda, mu, pair(rw_w0), pair(rw_a0), pair(rw_k_k),
                      pair(rw_k_a), pair(rw_r_k.reshape(RW_WIDTH)), pair(rw_gn_w),
                      pair(rw_gn_b), batch, seq_len)

    head = lambda p: jnp.pad(p.astype(F32), (GD_HEADS, LANES - 2 * GD_HEADS)).reshape(1, LANES)
    conv_w = jnp.swapaxes(gd_conv_w.reshape(GD_CONV, 3 * GD_HEADS, LANES), 0, 1)
    yb = _gdn_branch(pqkv, zgd, ba, ltri, conv_w, head(gd_A_log), head(gd_dt_bias),
                     gd_o_norm_w.reshape(1, GD_HEAD), batch, seq_len)
    return gates, ya, yb


def kernel(x, norm_in_w, w_in, rw_mu, rw_w0, rw_w2, rw_a0, rw_a2, rw_k_k, rw_k_a, rw_r_k,
           rw_gn_w, rw_gn_b, gd_conv_w, gd_A_log, gd_dt_bias, gd_o_norm_w,
           w_branch_a, w_branch_b, w_out, norm_out_w):
    batch, seq_len, d = x.shape
    assert d == D_MODEL and norm_in_w.shape[0] == 1
    assert seq_len % max(PROJ_TM, REC_TT) == 0 and (batch * seq_len) % MERGE_TM == 0
    x2 = x.reshape(batch * seq_len, d)
    gates, ya, yb = _layer(
        x2, batch, seq_len, norm_in_w[0], w_in[0], rw_mu[0], rw_w0[0], rw_w2[0], rw_a0[0],
        rw_a2[0], rw_k_k[0], rw_k_a[0], rw_r_k[0], rw_gn_w[0], rw_gn_b[0], gd_conv_w[0],
        gd_A_log[0], gd_dt_bias[0], gd_o_norm_w[0], w_branch_a[0], w_branch_b[0], w_out[0])
    out = _merge(x2, ya, yb, gates, w_branch_a[0].astype(BF16), w_branch_b[0].astype(BF16),
                 w_out[0].astype(BF16), norm_out_w.reshape(1, D_MODEL))
    return out.reshape(batch, seq_len, d)
```

```python
import jax
import jax.numpy as jnp
from jax import lax
from jax.experimental import pallas as pl
from jax.experimental.pallas import tpu as pltpu

F32 = jnp.float32
BF16 = jnp.bfloat16

D_MODEL = 1024
RW_WIDTH = 512
RW_HEAD = 64
RW_RANK = 64
RW_GN_EPS = RW_HEAD * 1e-5
DECAY_SCALE = -0.6065306597126334
GD_WIDTH = 512
GD_HEAD = 128
GD_HEADS = GD_WIDTH // GD_HEAD
GD_CONV = 4
NORM_EPS = 1e-6
L2_EPS = 1e-12

LANES = 128
CHUNK = 64
N_PAIRS = RW_WIDTH // LANES
RW_SHIFT_COLS = 3 * RW_WIDTH + 2 * RW_RANK
C_ZRW = RW_SHIFT_COLS
C_QKV = C_ZRW + RW_WIDTH
C_ZGD = C_QKV + 3 * GD_WIDTH
C_GATES = C_ZGD + GD_WIDTH

PROJ_TM = 512
CARRY_ROWS = 8
REC_TT = 512
CUM_ROWS = 256
MERGE_TM = 512
MERGE_COLS = 512
VMEM_LIMIT = 56 * 1024 * 1024


def _mm(a, b):
    return jnp.dot(a.astype(BF16), b.astype(BF16), preferred_element_type=F32)


def _mm_nt(a, b):
    return lax.dot_general(a.astype(BF16), b.astype(BF16), (((1,), (1,)), ((), ())),
                           preferred_element_type=F32)


def _mm_tn(a, b):
    return lax.dot_general(a.astype(BF16), b.astype(BF16), (((0,), (0,)), ((), ())),
                           preferred_element_type=F32)


def _sigmoid(x):
    return 0.5 * jnp.tanh(0.5 * x) + 0.5


def _silu(x):
    h = 0.5 * x
    return h * jnp.tanh(h) + h


def _softplus(x):
    return jnp.maximum(x, 0.0) + jnp.log(1.0 + jnp.exp(-jnp.abs(x)))


def _split3(x):
    hi = x.astype(BF16)
    r1 = x - hi.astype(F32)
    mid = r1.astype(BF16)
    lo = (r1 - mid.astype(F32)).astype(BF16)
    return hi, mid, lo


def _blk(idx, size):
    return lax.shift_right_logical(idx, size.bit_length() - 1)


def _tri_masks(n):
    ri = lax.broadcasted_iota(jnp.int32, (n, n), 0)
    ci = lax.broadcasted_iota(jnp.int32, (n, n), 1)
    same = _blk(ri, CHUNK) == _blk(ci, CHUNK)
    return ri, ci, same


def _unit_lower_inverses(lmats, ri, ci):
    eye = (ri == ci).astype(F32)
    blk8 = _blk(ri, 8) == _blk(ci, 8)
    n = lmats[0].shape[0]
    ds = [jnp.where(blk8, l, 0.0) for l in lmats]
    dbs = [d.astype(BF16) for d in ds]
    d2s = [_mm(d, d).astype(BF16) for d in dbs]
    ts = [eye + d for d in ds]
    xs = [_mm(d2, jnp.concatenate([t.astype(BF16), d2], axis=1)) for t, d2 in zip(ts, d2s)]
    ts = [t + x[:, :n] for t, x in zip(ts, xs)]
    ts = [t + _mm(x[:, n:], t) for t, x in zip(ts, xs)]
    s = 8
    while s < CHUNK:
        groups = range(n // (2 * s))
        lower = lambda x: jnp.concatenate(
            [x[g * 2 * s + s:(g + 1) * 2 * s] for g in groups], axis=0)
        zeros = jnp.zeros((s, n), F32)
        rc = lax.broadcasted_iota(jnp.int32, (n // 2, n), 0)
        cl = lax.broadcasted_iota(jnp.int32, (n // 2, n), 1)
        rl = _blk(rc, s) * (2 * s) + s + (rc & (s - 1))
        off = (_blk(rl, 2 * s) == _blk(cl, 2 * s)) & (_blk(rl, s) != _blk(cl, s))
        tbs = [t.astype(BF16) for t in ts]
        xs = [_mm(jnp.where(off, lower(l), 0.0), t) for l, t in zip(lmats, tbs)]
        xs = [jnp.concatenate([p for g in groups for p in (zeros, x[g * s:(g + 1) * s])], axis=0)
              for x in xs]
        ys = [_mm(lower(t), x) for t, x in zip(ts, xs)]
        ts = [jnp.concatenate([p for g in groups
                               for p in (t[g * 2 * s:g * 2 * s + s],
                                         t[g * 2 * s + s:(g + 1) * 2 * s] + y[g * s:(g + 1) * s])],
                              axis=0)
              for t, y in zip(ts, ys)]
        s *= 2
    return [t.astype(BF16) for t in ts]


def _chunk_cumsum(ltri, x):
    return jnp.concatenate([_mm(ltri, x[r0:r0 + CUM_ROWS]) for r0 in range(0, x.shape[0], CUM_ROWS)],
                           axis=0)


def _proj_kernel(x_ref, g_ref, w_ref, wba_ref, prw_ref, zrw_ref, pqkv_ref, zgd_ref, ba_ref):
    xc = x_ref[...]
    inv = lax.rsqrt(jnp.mean(xc * xc, axis=-1, keepdims=True) + NORM_EPS)
    xb = (xc * g_ref[...]).astype(BF16)

    def proj(w):
        return jnp.dot(xb, w, preferred_element_type=F32) * inv

    def lane_groups(c0, n_groups, out_ref):
        for g0 in range(0, n_groups, 4):
            ng = min(4, n_groups - g0)
            pe = proj(w_ref[:, c0 + g0 * LANES:c0 + (g0 + ng) * LANES])
            for j in range(ng):
                out_ref[g0 + j] = pe[:, j * LANES:(j + 1) * LANES].astype(BF16)

    lane_groups(0, RW_SHIFT_COLS // LANES, prw_ref)
    lane_groups(C_ZRW, N_PAIRS, zrw_ref)
    lane_groups(C_QKV, 3 * GD_HEADS, pqkv_ref)
    lane_groups(C_ZGD, GD_HEADS, zgd_ref)
    ba_ref[...] = proj(wba_ref[...])


def _projection(x2, norm_w, w_main, w_ba):
    bt = x2.shape[0]
    tm = PROJ_TM
    const = lambda i: (0, 0)
    slabs = lambda n: pl.BlockSpec((n, tm, LANES), lambda i: (0, i, 0))
    n_rw = RW_SHIFT_COLS // LANES
    out_shapes = (
        jax.ShapeDtypeStruct((n_rw, bt, LANES), BF16),
        jax.ShapeDtypeStruct((N_PAIRS, bt, LANES), BF16),
        jax.ShapeDtypeStruct((3 * GD_HEADS, bt, LANES), BF16),
        jax.ShapeDtypeStruct((GD_HEADS, bt, LANES), BF16),
        jax.ShapeDtypeStruct((bt, LANES), F32),
    )
    out_specs = (
        slabs(n_rw), slabs(N_PAIRS), slabs(3 * GD_HEADS), slabs(GD_HEADS),
        pl.BlockSpec((tm, LANES), lambda i: (i, 0)),
    )
    in_specs = [
        pl.BlockSpec((tm, D_MODEL), lambda i: (i, 0)),
        pl.BlockSpec((1, D_MODEL), const),
        pl.BlockSpec((D_MODEL, C_GATES), const, pipeline_mode=pl.Buffered(1)),
        pl.BlockSpec((D_MODEL, LANES), const),
    ]
    return pl.pallas_call(
        _proj_kernel,
        grid=(bt // tm,),
        in_specs=in_specs,
        out_specs=out_specs,
        out_shape=out_shapes,
        compiler_params=pltpu.CompilerParams(
            dimension_semantics=("arbitrary",), vmem_limit_bytes=VMEM_LIMIT),
        name="rmsnorm_in_proj",
    )(x2, norm_w, w_main, w_ba)


def _rwkv_kernel(prw_ref, z_ref, ltri_ref, wda_ref, mu_ref, w0_ref, a0_ref,
                 kk_ref, ka_ref, rk_ref, gnw_ref, gnb_ref, y_ref, s_scr, carry_scr):
    t_idx = pl.program_id(1)

    tt = z_ref.shape[1]
    n_chunks = tt // CHUNK

    @pl.when(t_idx == 0)
    def _():
        s_scr[...] = jnp.zeros_like(s_scr)
        carry_scr[...] = jnp.zeros_like(carry_scr)

    def token_shift(g):
        cur = prw_ref[g].astype(F32)
        prev = pltpu.roll(jnp.concatenate([carry_scr[g], cur], axis=0), 1, 0)[CARRY_ROWS:]
        carry_scr[g] = cur[tt - CARRY_ROWS:]
        return cur + (prev - cur) * mu_ref[g]

    lane = lax.broadcasted_iota(jnp.int32, (1, LANES), 1)
    head_a = lane < RW_HEAD
    ri, ci, same = _tri_masks(LANES)
    strict = same & (ci < ri)
    causal = same & (ci <= ri)

    def head_sum(x):
        sa = jnp.sum(jnp.where(head_a, x, 0.0), axis=-1, keepdims=True)
        sb = jnp.sum(jnp.where(head_a, 0.0, x), axis=-1, keepdims=True)
        return jnp.where(head_a, jnp.broadcast_to(sa, x.shape), jnp.broadcast_to(sb, x.shape))
    ltri = ltri_ref[...]

    def pair_rows(x):
        return jnp.concatenate([jnp.where(head_a, x, 0.0), jnp.where(head_a, 0.0, x)], axis=0)

    wa = token_shift(3 * N_PAIRS)
    da = jnp.where(lane < RW_RANK, jnp.tanh(wa), wa)
    pre = _mm(da, wda_ref[...])

    bonus, p_end, work = [], [], []
    for j in range(N_PAIRS):
        r = token_shift(j)
        k = token_shift(N_PAIRS + j)
        v = token_shift(2 * N_PAIRS + j)
        cols = slice(j * LANES, (j + 1) * LANES)
        lw = DECAY_SCALE * _sigmoid(w0_ref[j] + pre[:, cols])
        lr = _sigmoid(a0_ref[j] + pre[:, RW_WIDTH + j * LANES:RW_WIDTH + (j + 1) * LANES])
        kkr = k * kk_ref[j]
        kk = kkr * lax.rsqrt(head_sum(kkr * kkr) + L2_EPS)
        k2 = k * (1.0 + (lr - 1.0) * ka_ref[j])
        b = kk * lr
        bonus.append(head_sum(r * k2 * rk_ref[j]) * v)
        lw_hi = lw.astype(BF16)
        lw_lo = (lw - lw_hi.astype(F32)).astype(BF16)
        cl2 = _chunk_cumsum(ltri, jnp.concatenate([lw_hi, lw_lo], axis=1))
        clw = cl2[:, :LANES] + cl2[:, LANES:]
        e_neg = jnp.exp(-clw)
        rt = r * jnp.exp(clw)
        at = -kk * jnp.exp(clw - lw)
        bt = b * e_neg
        kt = k2 * e_neg
        for c in range(n_chunks):
            sl = slice(c * CHUNK, (c + 1) * CHUNK)
            p_c = jnp.exp(clw[(c + 1) * CHUNK - 1:(c + 1) * CHUNK, :])
            p_end.append(jnp.sum(jnp.where(ri == ci, p_c, 0.0), axis=-1, keepdims=True))
            work.append(dict(
                ar=jnp.concatenate([pair_rows(at[sl]), pair_rows(rt[sl])], axis=0).astype(BF16),
                bk=jnp.concatenate([bt[sl], bt[sl], kt[sl], kt[sl]], axis=0).astype(BF16),
                v=pair_rows(v[sl]).astype(BF16),
                bk_end=jnp.concatenate([pair_rows(bt[sl] * p_c), pair_rows(kt[sl] * p_c)],
                                       axis=0).astype(BF16)))

    n = len(work)
    causal2 = jnp.concatenate([causal, causal], axis=1)
    zeros = jnp.zeros((LANES, LANES), BF16)
    ms = [_mm_nt(w["ar"], w["bk"]) for w in work]
    lbs = [jnp.where(strict, m[:LANES, :LANES], 0.0) for m in ms]
    aks = [jnp.where(strict, m[:LANES, LANES:], 0.0).astype(BF16) for m in ms]
    rbks = [jnp.where(causal2, m[LANES:], 0.0).astype(BF16) for m in ms]
    t_invs = _unit_lower_inverses(lbs, ri, ci)
    akvs = [_mm(ak, w["v"]).astype(BF16) for ak, w in zip(aks, work)]
    tws = [_mm(t, jnp.concatenate([w["ar"][:LANES], akv], axis=1)).astype(BF16)
           for t, w, akv in zip(t_invs, work, akvs)]
    xs = [jnp.concatenate([tw, jnp.concatenate([zeros, w["v"]], axis=1)], axis=0)
          for tw, w in zip(tws, work)]
    qys = [_mm(rbk, x) for rbk, x in zip(rbks, xs)]
    zs = [_mm_tn(w["bk_end"], x) for w, x in zip(work, xs)]
    zas = [z[:, :LANES].astype(BF16) for z in zs]

    states = [None] * n
    for c in range(n_chunks):
        for j in range(N_PAIRS):
            i = j * n_chunks + c
            if c == 0:
                states[i] = s_scr[j]
            h0 = states[i]
            h1 = p_end[i] * h0 + _mm(zas[i], h0) + zs[i][:, LANES:]
            if c == n_chunks - 1:
                s_scr[j] = h1
            else:
                states[i + 1] = h1

    for j in range(N_PAIRS):
        ys = []
        for c in range(n_chunks):
            i = j * n_chunks + c
            qc = work[i]["ar"][LANES:] + qys[i][:, :LANES]
            y_p = _mm(qc, states[i]) + qys[i][:, LANES:]
            ys.append(y_p[:CHUNK] + y_p[CHUNK:])
        y = jnp.concatenate(ys, axis=0)
        inv_n = 1.0 / RW_HEAD
        yc = y - head_sum(y) * inv_n
        var = head_sum(yc * yc) * inv_n
        yn = yc * lax.rsqrt(var + RW_GN_EPS) * gnw_ref[j] + gnb_ref[j]
        y_ref[j] = ((yn + bonus[j]) * _silu(z_ref[j].astype(F32))).astype(y_ref.dtype)


def _rwkv_branch(prw, zrw, ltri, wda, mu, w0, a0, k_k, k_a, r_k, gn_w, gn_b, batch, seq_len):
    tt = REC_TT
    nt = seq_len // tt
    bt = batch * seq_len
    tok = lambda n: pl.BlockSpec((n, tt, LANES), lambda b, t: (0, b * nt + t, 0))
    par = pl.BlockSpec((N_PAIRS, 1, LANES), lambda b, t: (0, 0, 0))
    n_rw = RW_SHIFT_COLS // LANES
    in_specs = [
        tok(n_rw),
        tok(N_PAIRS),
        pl.BlockSpec((CUM_ROWS, CUM_ROWS), lambda b, t: (0, 0)),
        pl.BlockSpec((LANES, 2 * RW_WIDTH), lambda b, t: (0, 0)),
        pl.BlockSpec((n_rw, 1, LANES), lambda b, t: (0, 0, 0)),
        par, par, par, par, par, par, par,
    ]
    return pl.pallas_call(
        _rwkv_kernel,
        grid=(batch, nt),
        in_specs=in_specs,
        out_specs=tok(N_PAIRS),
        out_shape=jax.ShapeDtypeStruct((N_PAIRS, bt, LANES), BF16),
        scratch_shapes=[pltpu.VMEM((N_PAIRS, LANES, LANES), F32),
                        pltpu.VMEM((n_rw, CARRY_ROWS, LANES), F32)],
        compiler_params=pltpu.CompilerParams(
            dimension_semantics=("arbitrary", "arbitrary"), vmem_limit_bytes=VMEM_LIMIT),
        name="rwkv7_chunked",
    )(prw, zrw, ltri, wda, mu, w0, a0, k_k, k_a, r_k, gn_w, gn_b)


def _gdn_kernel(pqkv_ref, z_ref, ba_ref, ltri_ref, cw_ref, alog_ref, dtb_ref, onw_ref,
                y_ref, s_scr, carry_scr):
    t_idx = pl.program_id(1)

    tt = ba_ref.shape[0]

    @pl.when(t_idx == 0)
    def _():
        s_scr[...] = jnp.zeros_like(s_scr)
        carry_scr[:, :CARRY_ROWS, :] = jnp.zeros((3 * GD_HEADS, CARRY_ROWS, LANES), F32)

    for g in range(3 * GD_HEADS):
        carry_scr[g, CARRY_ROWS:, :] = pqkv_ref[g].astype(F32)

    def conv_silu(g):
        cur = carry_scr[g, CARRY_ROWS:, :]
        taps = cw_ref[g]
        acc = cur * taps[GD_CONV - 1:GD_CONV]
        for s in range(1, GD_CONV):
            acc = acc + carry_scr[g, pl.ds(CARRY_ROWS - s, tt), :] * taps[GD_CONV - 1 - s:GD_CONV - s]
        carry_scr[g, :CARRY_ROWS, :] = cur[tt - CARRY_ROWS:]
        return _silu(acc)

    n_super = tt // LANES
    per_super = LANES // CHUNK
    ri, ci, same = _tri_masks(LANES)
    strict = same & (ci < ri)
    causal = same & (ci <= ri)
    ba = ba_ref[...]
    g_all = -jnp.exp(alog_ref[...]) * _softplus(ba + dtb_ref[...])
    gc3 = _chunk_cumsum(ltri_ref[...], jnp.concatenate(_split3(g_all), axis=1))
    gc_all = gc3[:, :LANES] + gc3[:, LANES:2 * LANES] + gc3[:, 2 * LANES:]
    gc_all_t = gc_all.T

    work = []
    for h in range(GD_HEADS):
        q = conv_silu(h)
        k = conv_silu(GD_HEADS + h)
        v = conv_silu(2 * GD_HEADS + h)
        qn = q * (lax.rsqrt(jnp.sum(q * q, axis=-1, keepdims=True) + L2_EPS) * (GD_HEAD ** -0.5))
        kn = k * lax.rsqrt(jnp.sum(k * k, axis=-1, keepdims=True) + L2_EPS)
        beta = jnp.broadcast_to(_sigmoid(ba[:, h:h + 1]), (tt, LANES))
        gc_col = jnp.broadcast_to(gc_all[:, GD_HEADS + h:GD_HEADS + h + 1], (tt, LANES))
        gc_row = gc_all_t[GD_HEADS + h:GD_HEADS + h + 1, :]
        e_col = jnp.exp(gc_col)
        kb = kn * beta
        vb = v * beta
        qd = qn * e_col
        kbe = kb * e_col
        for sc in range(n_super):
            sl = slice(sc * LANES, (sc + 1) * LANES)
            diff = gc_col[sl] - gc_row[:, sl]
            chunks = []
            for cc in range(per_super):
                rows = slice(sc * LANES + cc * CHUNK, sc * LANES + (cc + 1) * CHUNK)
                g_last = gc_col[rows.stop - 1:rows.stop, :]
                chunks.append(dict(kte=(kn[rows] * jnp.exp(g_last - gc_col[rows])).astype(BF16),
                                   decay=jnp.exp(g_last)))
            work.append(dict(dec=jnp.exp(jnp.where(causal, diff, -1e30)),
                             kq=jnp.concatenate([kb[sl], qn[sl]], axis=0).astype(BF16),
                             kn=kn[sl].astype(BF16),
                             rhs=jnp.concatenate([vb[sl], kbe[sl]], axis=1).astype(BF16),
                             qd=qd[sl], chunks=chunks))

    ms = [_mm_nt(w["kq"], w["kn"]) for w in work]
    lmats = [-jnp.where(strict, m[:LANES] * w["dec"], 0.0) for m, w in zip(ms, work)]
    qks = [(m[LANES:] * w["dec"]).astype(BF16) for m, w in zip(ms, work)]
    t_invs = _unit_lower_inverses(lmats, ri, ci)
    uws = [_mm(t, w["rhs"]).astype(BF16) for t, w in zip(t_invs, work)]
    qws = [_mm(qk, uw) for qk, uw in zip(qks, uws)]
    for w, uw, qw in zip(work, uws, qws):
        w["qc"] = (w["qd"] - qw[:, LANES:]).astype(BF16)
        w["o0"] = qw[:, :LANES]
        for cc, ch in enumerate(w["chunks"]):
            tn = _mm_tn(ch["kte"], uw[cc * CHUNK:(cc + 1) * CHUNK])
            ch["tn_u"] = tn[:, :LANES]
            ch["tn_w"] = tn[:, LANES:].astype(BF16)

    state = [s_scr[h] for h in range(GD_HEADS)]
    for sc in range(n_super):
        for cc in range(per_super):
            for h in range(GD_HEADS):
                ch = work[h * n_super + sc]["chunks"][cc]
                s0 = state[h]
                ch["s0"] = s0
                state[h] = s0 * ch["decay"] - _mm(ch["tn_w"], s0) + ch["tn_u"]
    for h in range(GD_HEADS):
        s_scr[h] = state[h]

    for h in range(GD_HEADS):
        os = []
        for sc in range(n_super):
            w = work[h * n_super + sc]
            for cc, ch in enumerate(w["chunks"]):
                rows = slice(cc * CHUNK, (cc + 1) * CHUNK)
                os.append(_mm(w["qc"][rows], ch["s0"]) + w["o0"][rows])
        o = jnp.concatenate(os, axis=0)
        ms = jnp.mean(o * o, axis=-1, keepdims=True)
        on = o * lax.rsqrt(ms + NORM_EPS) * onw_ref[...]
        y_ref[h] = (on * _silu(z_ref[h].astype(F32))).astype(y_ref.dtype)


def _gdn_branch(pqkv, zgd, ba, ltri, conv_w, alog, dtb, onw, batch, seq_len):
    tt = REC_TT
    nt = seq_len // tt
    bt = batch * seq_len
    tok = lambda n: pl.BlockSpec((n, tt, LANES), lambda b, t: (0, b * nt + t, 0))
    par = pl.BlockSpec((1, LANES), lambda b, t: (0, 0))
    in_specs = [
        tok(3 * GD_HEADS),
        tok(GD_HEADS),
        pl.BlockSpec((tt, LANES), lambda b, t: (b * nt + t, 0)),
        pl.BlockSpec((CUM_ROWS, CUM_ROWS), lambda b, t: (0, 0)),
        pl.BlockSpec((3 * GD_HEADS, GD_CONV, LANES), lambda b, t: (0, 0, 0)),
        par, par,
        pl.BlockSpec((1, LANES), lambda b, t: (0, 0)),
    ]
    return pl.pallas_call(
        _gdn_kernel,
        grid=(batch, nt),
        in_specs=in_specs,
        out_specs=tok(GD_HEADS),
        out_shape=jax.ShapeDtypeStruct((GD_HEADS, bt, LANES), BF16),
        scratch_shapes=[pltpu.VMEM((GD_HEADS, LANES, LANES), F32),
                        pltpu.VMEM((3 * GD_HEADS, CARRY_ROWS + tt, LANES), F32)],
        compiler_params=pltpu.CompilerParams(
            dimension_semantics=("arbitrary", "arbitrary"), vmem_limit_bytes=VMEM_LIMIT),
        name="gated_delta_chunked",
    )(pqkv, zgd, ba, ltri, conv_w, alog, dtb, onw)


def _merge_kernel(x_ref, ya_ref, yb_ref, gin_ref, wg_ref, wa_ref, wb_ref, wo_ref, g_ref, o_ref):
    xc = x_ref[...]
    inv = lax.rsqrt(jnp.mean(xc * xc, axis=-1, keepdims=True) + NORM_EPS)
    xb = (xc * gin_ref[...]).astype(BF16)
    ya = jnp.concatenate([ya_ref[j] for j in range(N_PAIRS)], axis=1)
    yb = jnp.concatenate([yb_ref[j] for j in range(GD_HEADS)], axis=1)
    blocks = []
    for c0 in range(0, D_MODEL, MERGE_COLS):
        cols = slice(c0, c0 + MERGE_COLS)
        gate_a = _sigmoid(jnp.dot(xb, wg_ref[:, cols], preferred_element_type=F32) * inv)
        gate_b = _sigmoid(jnp.dot(xb, wg_ref[:, D_MODEL + c0:D_MODEL + c0 + MERGE_COLS],
                                  preferred_element_type=F32) * inv)
        ma = jnp.dot(ya, wa_ref[:, cols], preferred_element_type=F32)
        mb = jnp.dot(yb, wb_ref[:, cols], preferred_element_type=F32)
        blocks.append((gate_a * ma + gate_b * mb).astype(BF16))
    merged = jnp.concatenate(blocks, axis=1)
    res = xc + jnp.dot(merged, wo_ref[...], preferred_element_type=F32)
    ms = jnp.mean(res * res, axis=-1, keepdims=True)
    o_ref[...] = res * lax.rsqrt(ms + NORM_EPS) * g_ref[...]


def _merge(x2, ya, yb, norm_in, w_gates, w_a, w_b, w_o, norm_w):
    bt = x2.shape[0]
    tm = MERGE_TM
    const = lambda i: (0, 0)
    in_specs = [
        pl.BlockSpec((tm, D_MODEL), lambda i: (i, 0)),
        pl.BlockSpec((N_PAIRS, tm, LANES), lambda i: (0, i, 0)),
        pl.BlockSpec((GD_HEADS, tm, LANES), lambda i: (0, i, 0)),
        pl.BlockSpec((1, D_MODEL), const),
        pl.BlockSpec((D_MODEL, 2 * D_MODEL), const),
        pl.BlockSpec((RW_WIDTH, D_MODEL), const),
        pl.BlockSpec((GD_WIDTH, D_MODEL), const),
        pl.BlockSpec((D_MODEL, D_MODEL), const),
        pl.BlockSpec((1, D_MODEL), const),
    ]
    return pl.pallas_call(
        _merge_kernel,
        grid=(bt // tm,),
        in_specs=in_specs,
        out_specs=pl.BlockSpec((tm, D_MODEL), lambda i: (i, 0)),
        out_shape=jax.ShapeDtypeStruct((bt, D_MODEL), F32),
        compiler_params=pltpu.CompilerParams(
            dimension_semantics=("arbitrary",), vmem_limit_bytes=VMEM_LIMIT),
        name="merge_out_proj",
    )(x2, ya, yb, norm_in, w_gates, w_a, w_b, w_o, norm_w)


def _block_lower_tri(n):
    i = jnp.arange(n)
    return (((i[:, None] // CHUNK) == (i[None, :] // CHUNK)) & (i[None, :] <= i[:, None])).astype(BF16)


def _layer(x2, batch, seq_len, norm_in_w, w_in, rw_mu, rw_w0, rw_w2, rw_a0, rw_a2, rw_k_k,
           rw_k_a, rw_r_k, rw_gn_w, rw_gn_b, gd_conv_w, gd_A_log, gd_dt_bias, gd_o_norm_w,
           w_branch_a, w_branch_b, w_out):
    c_gates = C_GATES + 2 * GD_HEADS
    w_main = w_in[:, :C_GATES].astype(BF16)
    w_gates = w_in[:, c_gates:].astype(BF16)
    w_ba = jnp.pad(w_in[:, C_GATES:c_gates], ((0, 0), (0, LANES - 2 * GD_HEADS))).astype(BF16)

    prw, zrw, pqkv, zgd, ba = _projection(x2, norm_in_w.reshape(1, D_MODEL), w_main, w_ba)

    ltri = _block_lower_tri(CUM_ROWS)
    pair = lambda p: p.reshape(N_PAIRS, 1, LANES)
    zero = jnp.zeros((RW_RANK, RW_WIDTH), F32)
    wda = jnp.concatenate([jnp.concatenate([rw_w2, zero], axis=1),
                           jnp.concatenate([zero, rw_a2], axis=1)], axis=0).astype(BF16)
    mu = rw_mu.reshape(RW_SHIFT_COLS // LANES, 1, LANES)
    ya = _rwkv_branch(prw, zrw, ltri, wda, mu, pair(rw_w0), pair(rw_a0), pair(rw_k_k),
                      pair(rw_k_a), pair(rw_r_k.reshape(RW_WIDTH)), pair(rw_gn_w),
                      pair(rw_gn_b), batch, seq_len)

    head = lambda p: jnp.pad(p.astype(F32), (GD_HEADS, LANES - 2 * GD_HEADS)).reshape(1, LANES)
    conv_w = jnp.swapaxes(gd_conv_w.reshape(GD_CONV, 3 * GD_HEADS, LANES), 0, 1)
    yb = _gdn_branch(pqkv, zgd, ba, ltri, conv_w, head(gd_A_log), head(gd_dt_bias),
                     gd_o_norm_w.reshape(1, GD_HEAD), batch, seq_len)
    return w_gates, ya, yb


def kernel(x, norm_in_w, w_in, rw_mu, rw_w0, rw_w2, rw_a0, rw_a2, rw_k_k, rw_k_a, rw_r_k,
           rw_gn_w, rw_gn_b, gd_conv_w, gd_A_log, gd_dt_bias, gd_o_norm_w,
           w_branch_a, w_branch_b, w_out, norm_out_w):
    batch, seq_len, d = x.shape
    assert d == D_MODEL and norm_in_w.shape[0] == 1
    assert seq_len % max(PROJ_TM, REC_TT) == 0 and (batch * seq_len) % MERGE_TM == 0
    x2 = x.reshape(batch * seq_len, d)
    w_gates, ya, yb = _layer(
        x2, batch, seq_len, norm_in_w[0], w_in[0], rw_mu[0], rw_w0[0], rw_w2[0], rw_a0[0],
        rw_a2[0], rw_k_k[0], rw_k_a[0], rw_r_k[0], rw_gn_w[0], rw_gn_b[0], gd_conv_w[0],
        gd_A_log[0], gd_dt_bias[0], gd_o_norm_w[0], w_branch_a[0], w_branch_b[0], w_out[0])
    out = _merge(x2, ya, yb, norm_in_w.reshape(1, D_MODEL), w_gates,
                 w_branch_a[0].astype(BF16), w_branch_b[0].astype(BF16), w_out[0].astype(BF16),
                 norm_out_w.reshape(1, D_MODEL))
    return out.reshape(batch, seq_len, d)
```

```python
import jax
import jax.numpy as jnp
from jax import lax
from jax.experimental import pallas as pl
from jax.experimental.pallas import tpu as pltpu

F32 = jnp.float32
BF16 = jnp.bfloat16

D_MODEL = 1024
RW_WIDTH = 512
RW_HEAD = 64
RW_RANK = 64
RW_GN_EPS = RW_HEAD * 1e-5
DECAY_SCALE = -0.6065306597126334
GD_WIDTH = 512
GD_HEAD = 128
GD_HEADS = GD_WIDTH // GD_HEAD
GD_CONV = 4
NORM_EPS = 1e-6
L2_EPS = 1e-12

LANES = 128
RW_CHUNK = 64
GD_CHUNK = 128
N_PAIRS = RW_WIDTH // LANES
RW_SHIFT_COLS = 3 * RW_WIDTH + 2 * RW_RANK
C_ZRW = RW_SHIFT_COLS
C_QKV = C_ZRW + RW_WIDTH
C_ZGD = C_QKV + 3 * GD_WIDTH
C_GATES = C_ZGD + GD_WIDTH

PROJ_TM = 512
CARRY_ROWS = 8
REC_TT = 512
CUM_ROWS = 256
MERGE_TM = 512
MERGE_COLS = 512
VMEM_LIMIT = 56 * 1024 * 1024


def _mm(a, b):
    return jnp.dot(a.astype(BF16), b.astype(BF16), preferred_element_type=F32)


def _mm_nt(a, b):
    return lax.dot_general(a.astype(BF16), b.astype(BF16), (((1,), (1,)), ((), ())),
                           preferred_element_type=F32)


def _mm_tn(a, b):
    return lax.dot_general(a.astype(BF16), b.astype(BF16), (((0,), (0,)), ((), ())),
                           preferred_element_type=F32)


def _sigmoid(x):
    return 0.5 * jnp.tanh(0.5 * x) + 0.5


def _silu(x):
    h = 0.5 * x
    return h * jnp.tanh(h) + h


def _softplus(x):
    return jnp.maximum(x, 0.0) + jnp.log(1.0 + jnp.exp(-jnp.abs(x)))


def _split3(x):
    hi = x.astype(BF16)
    r1 = x - hi.astype(F32)
    mid = r1.astype(BF16)
    lo = (r1 - mid.astype(F32)).astype(BF16)
    return hi, mid, lo


def _blk(idx, size):
    return lax.shift_right_logical(idx, size.bit_length() - 1)


def _tri_masks(n, chunk):
    ri = lax.broadcasted_iota(jnp.int32, (n, n), 0)
    ci = lax.broadcasted_iota(jnp.int32, (n, n), 1)
    same = _blk(ri, chunk) == _blk(ci, chunk)
    return ri, ci, same


def _unit_lower_inverses(lmats, ri, ci, chunk):
    eye = (ri == ci).astype(F32)
    blk8 = _blk(ri, 8) == _blk(ci, 8)
    n = lmats[0].shape[0]
    ds = [jnp.where(blk8, l, 0.0) for l in lmats]
    dbs = [d.astype(BF16) for d in ds]
    d2s = [_mm(d, d).astype(BF16) for d in dbs]
    ts = [eye + d for d in ds]
    xs = [_mm(d2, jnp.concatenate([t.astype(BF16), d2], axis=1)) for t, d2 in zip(ts, d2s)]
    ts = [t + x[:, :n] for t, x in zip(ts, xs)]
    ts = [t + _mm(x[:, n:], t) for t, x in zip(ts, xs)]
    s = 8
    while s < chunk:
        groups = range(n // (2 * s))
        lower = lambda x: jnp.concatenate(
            [x[g * 2 * s + s:(g + 1) * 2 * s] for g in groups], axis=0)
        zeros = jnp.zeros((s, n), F32)
        rc = lax.broadcasted_iota(jnp.int32, (n // 2, n), 0)
        cl = lax.broadcasted_iota(jnp.int32, (n // 2, n), 1)
        rl = _blk(rc, s) * (2 * s) + s + (rc & (s - 1))
        off = (_blk(rl, 2 * s) == _blk(cl, 2 * s)) & (_blk(rl, s) != _blk(cl, s))
        tbs = [t.astype(BF16) for t in ts]
        xs = [_mm(jnp.where(off, lower(l), 0.0), t) for l, t in zip(lmats, tbs)]
        xs = [jnp.concatenate([p for g in groups for p in (zeros, x[g * s:(g + 1) * s])], axis=0)
              for x in xs]
        ys = [_mm(lower(t), x) for t, x in zip(ts, xs)]
        ts = [jnp.concatenate([p for g in groups
                               for p in (t[g * 2 * s:g * 2 * s + s],
                                         t[g * 2 * s + s:(g + 1) * 2 * s] + y[g * s:(g + 1) * s])],
                              axis=0)
              for t, y in zip(ts, ys)]
        s *= 2
    return [t.astype(BF16) for t in ts]


def _chunk_cumsum(ltri, x):
    return jnp.concatenate([_mm(ltri, x[r0:r0 + CUM_ROWS]) for r0 in range(0, x.shape[0], CUM_ROWS)],
                           axis=0)


def _proj_kernel(x_ref, g_ref, w_ref, wba_ref, prw_ref, zrw_ref, pqkv_ref, zgd_ref, ba_ref):
    xc = x_ref[...]
    inv = lax.rsqrt(jnp.mean(xc * xc, axis=-1, keepdims=True) + NORM_EPS)
    xb = (xc * g_ref[...]).astype(BF16)

    def proj(w):
        return jnp.dot(xb, w, preferred_element_type=F32) * inv

    def lane_groups(c0, n_groups, out_ref):
        for g0 in range(0, n_groups, 4):
            ng = min(4, n_groups - g0)
            pe = proj(w_ref[:, c0 + g0 * LANES:c0 + (g0 + ng) * LANES])
            for j in range(ng):
                out_ref[g0 + j] = pe[:, j * LANES:(j + 1) * LANES].astype(BF16)

    lane_groups(0, RW_SHIFT_COLS // LANES, prw_ref)
    lane_groups(C_ZRW, N_PAIRS, zrw_ref)
    lane_groups(C_QKV, 3 * GD_HEADS, pqkv_ref)
    lane_groups(C_ZGD, GD_HEADS, zgd_ref)
    ba_ref[...] = proj(wba_ref[...])


def _projection(x2, norm_w, w_main, w_ba):
    bt = x2.shape[0]
    tm = PROJ_TM
    const = lambda i: (0, 0)
    slabs = lambda n: pl.BlockSpec((n, tm, LANES), lambda i: (0, i, 0))
    n_rw = RW_SHIFT_COLS // LANES
    out_shapes = (
        jax.ShapeDtypeStruct((n_rw, bt, LANES), BF16),
        jax.ShapeDtypeStruct((N_PAIRS, bt, LANES), BF16),
        jax.ShapeDtypeStruct((3 * GD_HEADS, bt, LANES), BF16),
        jax.ShapeDtypeStruct((GD_HEADS, bt, LANES), BF16),
        jax.ShapeDtypeStruct((bt, LANES), F32),
    )
    out_specs = (
        slabs(n_rw), slabs(N_PAIRS), slabs(3 * GD_HEADS), slabs(GD_HEADS),
        pl.BlockSpec((tm, LANES), lambda i: (i, 0)),
    )
    in_specs = [
        pl.BlockSpec((tm, D_MODEL), lambda i: (i, 0)),
        pl.BlockSpec((1, D_MODEL), const),
        pl.BlockSpec((D_MODEL, C_GATES), const, pipeline_mode=pl.Buffered(1)),
        pl.BlockSpec((D_MODEL, LANES), const),
    ]
    return pl.pallas_call(
        _proj_kernel,
        grid=(bt // tm,),
        in_specs=in_specs,
        out_specs=out_specs,
        out_shape=out_shapes,
        compiler_params=pltpu.CompilerParams(
            dimension_semantics=("arbitrary",), vmem_limit_bytes=VMEM_LIMIT),
        name="rmsnorm_in_proj",
    )(x2, norm_w, w_main, w_ba)


def _rwkv_kernel(prw_ref, z_ref, ltri_ref, wda_ref, mu_ref, w0_ref, a0_ref,
                 kk_ref, ka_ref, rk_ref, gnw_ref, gnb_ref, y_ref, s_scr, carry_scr):
    t_idx = pl.program_id(1)

    tt = z_ref.shape[1]
    n_chunks = tt // RW_CHUNK

    @pl.when(t_idx == 0)
    def _():
        s_scr[...] = jnp.zeros_like(s_scr)
        carry_scr[...] = jnp.zeros_like(carry_scr)

    def token_shift(g):
        cur = prw_ref[g].astype(F32)
        prev = pltpu.roll(jnp.concatenate([carry_scr[g], cur], axis=0), 1, 0)[CARRY_ROWS:]
        carry_scr[g] = cur[tt - CARRY_ROWS:]
        return cur + (prev - cur) * mu_ref[g]

    lane = lax.broadcasted_iota(jnp.int32, (1, LANES), 1)
    head_a = lane < RW_HEAD
    ri, ci, same = _tri_masks(LANES, RW_CHUNK)
    strict = same & (ci < ri)
    causal = same & (ci <= ri)

    def head_sum(x):
        sa = jnp.sum(jnp.where(head_a, x, 0.0), axis=-1, keepdims=True)
        sb = jnp.sum(jnp.where(head_a, 0.0, x), axis=-1, keepdims=True)
        return jnp.where(head_a, jnp.broadcast_to(sa, x.shape), jnp.broadcast_to(sb, x.shape))
    ltri = ltri_ref[...]

    def pair_rows(x):
        return jnp.concatenate([jnp.where(head_a, x, 0.0), jnp.where(head_a, 0.0, x)], axis=0)

    wa = token_shift(3 * N_PAIRS)
    da = jnp.where(lane < RW_RANK, jnp.tanh(wa), wa)
    pre = _mm(da, wda_ref[...])

    bonus, p_end, work = [], [], []
    for j in range(N_PAIRS):
        r = token_shift(j)
        k = token_shift(N_PAIRS + j)
        v = token_shift(2 * N_PAIRS + j)
        cols = slice(j * LANES, (j + 1) * LANES)
        lw = DECAY_SCALE * _sigmoid(w0_ref[j] + pre[:, cols])
        lr = _sigmoid(a0_ref[j] + pre[:, RW_WIDTH + j * LANES:RW_WIDTH + (j + 1) * LANES])
        kkr = k * kk_ref[j]
        kk = kkr * lax.rsqrt(head_sum(kkr * kkr) + L2_EPS)
        k2 = k * (1.0 + (lr - 1.0) * ka_ref[j])
        b = kk * lr
        bonus.append(head_sum(r * k2 * rk_ref[j]) * v)
        lw_hi = lw.astype(BF16)
        lw_lo = (lw - lw_hi.astype(F32)).astype(BF16)
        cl2 = _chunk_cumsum(ltri, jnp.concatenate([lw_hi, lw_lo], axis=1))
        clw = cl2[:, :LANES] + cl2[:, LANES:]
        e_neg = jnp.exp(-clw)
        rt = r * jnp.exp(clw)
        at = -kk * jnp.exp(clw - lw)
        bt = b * e_neg
        kt = k2 * e_neg
        for c in range(n_chunks):
            sl = slice(c * RW_CHUNK, (c + 1) * RW_CHUNK)
            p_c = jnp.exp(clw[(c + 1) * RW_CHUNK - 1:(c + 1) * RW_CHUNK, :])
            p_end.append(jnp.sum(jnp.where(ri == ci, p_c, 0.0), axis=-1, keepdims=True))
            work.append(dict(
                ar=jnp.concatenate([pair_rows(at[sl]), pair_rows(rt[sl])], axis=0).astype(BF16),
                bk=jnp.concatenate([bt[sl], bt[sl], kt[sl], kt[sl]], axis=0).astype(BF16),
                v=pair_rows(v[sl]).astype(BF16),
                bk_end=jnp.concatenate([pair_rows(bt[sl] * p_c), pair_rows(kt[sl] * p_c)],
                                       axis=0).astype(BF16)))

    n = len(work)
    causal2 = jnp.concatenate([causal, causal], axis=1)
    zeros = jnp.zeros((LANES, LANES), BF16)
    ms = [_mm_nt(w["ar"], w["bk"]) for w in work]
    lbs = [jnp.where(strict, m[:LANES, :LANES], 0.0) for m in ms]
    aks = [jnp.where(strict, m[:LANES, LANES:], 0.0).astype(BF16) for m in ms]
    rbks = [jnp.where(causal2, m[LANES:], 0.0).astype(BF16) for m in ms]
    t_invs = _unit_lower_inverses(lbs, ri, ci, RW_CHUNK)
    akvs = [_mm(ak, w["v"]).astype(BF16) for ak, w in zip(aks, work)]
    tws = [_mm(t, jnp.concatenate([w["ar"][:LANES], akv], axis=1)).astype(BF16)
           for t, w, akv in zip(t_invs, work, akvs)]
    xs = [jnp.concatenate([tw, jnp.concatenate([zeros, w["v"]], axis=1)], axis=0)
          for tw, w in zip(tws, work)]
    qys = [_mm(rbk, x) for rbk, x in zip(rbks, xs)]
    zs = [_mm_tn(w["bk_end"], x) for w, x in zip(work, xs)]
    zas = [z[:, :LANES].astype(BF16) for z in zs]

    states = [None] * n
    for c in range(n_chunks):
        for j in range(N_PAIRS):
            i = j * n_chunks + c
            if c == 0:
                states[i] = s_scr[j]
            h0 = states[i]
            h1 = p_end[i] * h0 + _mm(zas[i], h0) + zs[i][:, LANES:]
            if c == n_chunks - 1:
                s_scr[j] = h1
            else:
                states[i + 1] = h1

    for j in range(N_PAIRS):
        ys = []
        for c in range(n_chunks):
            i = j * n_chunks + c
            qc = work[i]["ar"][LANES:] + qys[i][:, :LANES]
            y_p = _mm(qc, states[i]) + qys[i][:, LANES:]
            ys.append(y_p[:RW_CHUNK] + y_p[RW_CHUNK:])
        y = jnp.concatenate(ys, axis=0)
        inv_n = 1.0 / RW_HEAD
        yc = y - head_sum(y) * inv_n
        var = head_sum(yc * yc) * inv_n
        yn = yc * lax.rsqrt(var + RW_GN_EPS) * gnw_ref[j] + gnb_ref[j]
        y_ref[j] = ((yn + bonus[j]) * _silu(z_ref[j].astype(F32))).astype(y_ref.dtype)


def _rwkv_branch(prw, zrw, ltri, wda, mu, w0, a0, k_k, k_a, r_k, gn_w, gn_b, batch, seq_len):
    tt = REC_TT
    nt = seq_len // tt
    bt = batch * seq_len
    tok = lambda n: pl.BlockSpec((n, tt, LANES), lambda b, t: (0, b * nt + t, 0))
    par = pl.BlockSpec((N_PAIRS, 1, LANES), lambda b, t: (0, 0, 0))
    n_rw = RW_SHIFT_COLS // LANES
    in_specs = [
        tok(n_rw),
        tok(N_PAIRS),
        pl.BlockSpec((CUM_ROWS, CUM_ROWS), lambda b, t: (0, 0)),
        pl.BlockSpec((LANES, 2 * RW_WIDTH), lambda b, t: (0, 0)),
        pl.BlockSpec((n_rw, 1, LANES), lambda b, t: (0, 0, 0)),
        par, par, par, par, par, par, par,
    ]
    return pl.pallas_call(
        _rwkv_kernel,
        grid=(batch, nt),
        in_specs=in_specs,
        out_specs=tok(N_PAIRS),
        out_shape=jax.ShapeDtypeStruct((N_PAIRS, bt, LANES), BF16),
        scratch_shapes=[pltpu.VMEM((N_PAIRS, LANES, LANES), F32),
                        pltpu.VMEM((n_rw, CARRY_ROWS, LANES), F32)],
        compiler_params=pltpu.CompilerParams(
            dimension_semantics=("arbitrary", "arbitrary"), vmem_limit_bytes=VMEM_LIMIT),
        name="rwkv7_chunked",
    )(prw, zrw, ltri, wda, mu, w0, a0, k_k, k_a, r_k, gn_w, gn_b)


def _gdn_kernel(pqkv_ref, z_ref, ba_ref, ltri_ref, cw_ref, alog_ref, dtb_ref, onw_ref,
                y_ref, s_scr, carry_scr):
    t_idx = pl.program_id(1)

    tt = ba_ref.shape[0]

    @pl.when(t_idx == 0)
    def _():
        s_scr[...] = jnp.zeros_like(s_scr)
        carry_scr[:, :CARRY_ROWS, :] = jnp.zeros((3 * GD_HEADS, CARRY_ROWS, LANES), F32)

    for g in range(3 * GD_HEADS):
        carry_scr[g, CARRY_ROWS:, :] = pqkv_ref[g].astype(F32)

    def conv_silu(g):
        cur = carry_scr[g, CARRY_ROWS:, :]
        taps = cw_ref[g]
        acc = cur * taps[GD_CONV - 1:GD_CONV]
        for s in range(1, GD_CONV):
            acc = acc + carry_scr[g, pl.ds(CARRY_ROWS - s, tt), :] * taps[GD_CONV - 1 - s:GD_CONV - s]
        carry_scr[g, :CARRY_ROWS, :] = cur[tt - CARRY_ROWS:]
        return _silu(acc)

    n_super = tt // LANES
    per_super = LANES // GD_CHUNK
    ri, ci, same = _tri_masks(LANES, GD_CHUNK)
    strict = same & (ci < ri)
    causal = same & (ci <= ri)
    ba = ba_ref[...]
    g_all = -jnp.exp(alog_ref[...]) * _softplus(ba + dtb_ref[...])
    gc3 = _chunk_cumsum(ltri_ref[...], jnp.concatenate(_split3(g_all), axis=1))
    gc_all = gc3[:, :LANES] + gc3[:, LANES:2 * LANES] + gc3[:, 2 * LANES:]
    gc_all_t = gc_all.T

    work = []
    for h in range(GD_HEADS):
        q = conv_silu(h)
        k = conv_silu(GD_HEADS + h)
        v = conv_silu(2 * GD_HEADS + h)
        qn = q * (lax.rsqrt(jnp.sum(q * q, axis=-1, keepdims=True) + L2_EPS) * (GD_HEAD ** -0.5))
        kn = k * lax.rsqrt(jnp.sum(k * k, axis=-1, keepdims=True) + L2_EPS)
        beta = jnp.broadcast_to(_sigmoid(ba[:, h:h + 1]), (tt, LANES))
        gc_col = jnp.broadcast_to(gc_all[:, GD_HEADS + h:GD_HEADS + h + 1], (tt, LANES))
        gc_row = gc_all_t[GD_HEADS + h:GD_HEADS + h + 1, :]
        e_col = jnp.exp(gc_col)
        kb = kn * beta
        vb = v * beta
        qd = qn * e_col
        kbe = kb * e_col
        for sc in range(n_super):
            sl = slice(sc * LANES, (sc + 1) * LANES)
            diff = gc_col[sl] - gc_row[:, sl]
            chunks = []
            for cc in range(per_super):
                rows = slice(sc * LANES + cc * GD_CHUNK, sc * LANES + (cc + 1) * GD_CHUNK)
                g_last = gc_col[rows.stop - 1:rows.stop, :]
                chunks.append(dict(kte=(kn[rows] * jnp.exp(g_last - gc_col[rows])).astype(BF16),
                                   decay=jnp.exp(g_last)))
            work.append(dict(dec=jnp.exp(jnp.where(causal, diff, -1e30)),
                             kq=jnp.concatenate([kb[sl], qn[sl]], axis=0).astype(BF16),
                             kn=kn[sl].astype(BF16),
                             rhs=jnp.concatenate([vb[sl], kbe[sl]], axis=1).astype(BF16),
                             qd=qd[sl], chunks=chunks))

    ms = [_mm_nt(w["kq"], w["kn"]) for w in work]
    lmats = [-jnp.where(strict, m[:LANES] * w["dec"], 0.0) for m, w in zip(ms, work)]
    qks = [(m[LANES:] * w["dec"]).astype(BF16) for m, w in zip(ms, work)]
    t_invs = _unit_lower_inverses(lmats, ri, ci, GD_CHUNK)
    uws = [_mm(t, w["rhs"]).astype(BF16) for t, w in zip(t_invs, work)]
    qws = [_mm(qk, uw) for qk, uw in zip(qks, uws)]
    for w, uw, qw in zip(work, uws, qws):
        w["qc"] = (w["qd"] - qw[:, LANES:]).astype(BF16)
        w["o0"] = qw[:, :LANES]
        for cc, ch in enumerate(w["chunks"]):
            tn = _mm_tn(ch["kte"], uw[cc * GD_CHUNK:(cc + 1) * GD_CHUNK])
            ch["tn_u"] = tn[:, :LANES]
            ch["tn_w"] = tn[:, LANES:].astype(BF16)

    state = [s_scr[h] for h in range(GD_HEADS)]
    for sc in range(n_super):
        for cc in range(per_super):
            for h in range(GD_HEADS):
                ch = work[h * n_super + sc]["chunks"][cc]
                s0 = state[h]
                ch["s0"] = s0
                state[h] = s0 * ch["decay"] - _mm(ch["tn_w"], s0) + ch["tn_u"]
    for h in range(GD_HEADS):
        s_scr[h] = state[h]

    for h in range(GD_HEADS):
        os = []
        for sc in range(n_super):
            w = work[h * n_super + sc]
            for cc, ch in enumerate(w["chunks"]):
                rows = slice(cc * GD_CHUNK, (cc + 1) * GD_CHUNK)
                os.append(_mm(w["qc"][rows], ch["s0"]) + w["o0"][rows])
        o = jnp.concatenate(os, axis=0)
        ms = jnp.mean(o * o, axis=-1, keepdims=True)
        on = o * lax.rsqrt(ms + NORM_EPS) * onw_ref[...]
        y_ref[h] = (on * _silu(z_ref[h].astype(F32))).astype(y_ref.dtype)


def _gdn_branch(pqkv, zgd, ba, ltri, conv_w, alog, dtb, onw, batch, seq_len):
    tt = REC_TT
    nt = seq_len // tt
    bt = batch * seq_len
    tok = lambda n: pl.BlockSpec((n, tt, LANES), lambda b, t: (0, b * nt + t, 0))
    par = pl.BlockSpec((1, LANES), lambda b, t: (0, 0))
    in_specs = [
        tok(3 * GD_HEADS),
        tok(GD_HEADS),
        pl.BlockSpec((tt, LANES), lambda b, t: (b * nt + t, 0)),
        pl.BlockSpec((CUM_ROWS, CUM_ROWS), lambda b, t: (0, 0)),
        pl.BlockSpec((3 * GD_HEADS, GD_CONV, LANES), lambda b, t: (0, 0, 0)),
        par, par,
        pl.BlockSpec((1, LANES), lambda b, t: (0, 0)),
    ]
    return pl.pallas_call(
        _gdn_kernel,
        grid=(batch, nt),
        in_specs=in_specs,
        out_specs=tok(GD_HEADS),
        out_shape=jax.ShapeDtypeStruct((GD_HEADS, bt, LANES), BF16),
        scratch_shapes=[pltpu.VMEM((GD_HEADS, LANES, LANES), F32),
                        pltpu.VMEM((3 * GD_HEADS, CARRY_ROWS + tt, LANES), F32)],
        compiler_params=pltpu.CompilerParams(
            dimension_semantics=("arbitrary", "arbitrary"), vmem_limit_bytes=VMEM_LIMIT),
        name="gated_delta_chunked",
    )(pqkv, zgd, ba, ltri, conv_w, alog, dtb, onw)


def _merge_kernel(x_ref, ya_ref, yb_ref, gin_ref, wg_ref, wa_ref, wb_ref, wo_ref, g_ref, o_ref):
    xc = x_ref[...]
    inv = lax.rsqrt(jnp.mean(xc * xc, axis=-1, keepdims=True) + NORM_EPS)
    xb = (xc * gin_ref[...]).astype(BF16)
    ya = jnp.concatenate([ya_ref[j] for j in range(N_PAIRS)], axis=1)
    yb = jnp.concatenate([yb_ref[j] for j in range(GD_HEADS)], axis=1)
    blocks = []
    for c0 in range(0, D_MODEL, MERGE_COLS):
        cols = slice(c0, c0 + MERGE_COLS)
        gate_a = _sigmoid(jnp.dot(xb, wg_ref[:, cols], preferred_element_type=F32) * inv)
        gate_b = _sigmoid(jnp.dot(xb, wg_ref[:, D_MODEL + c0:D_MODEL + c0 + MERGE_COLS],
                                  preferred_element_type=F32) * inv)
        ma = jnp.dot(ya, wa_ref[:, cols], preferred_element_type=F32)
        mb = jnp.dot(yb, wb_ref[:, cols], preferred_element_type=F32)
        blocks.append((gate_a * ma + gate_b * mb).astype(BF16))
    merged = jnp.concatenate(blocks, axis=1)
    res = xc + jnp.dot(merged, wo_ref[...], preferred_element_type=F32)
    ms = jnp.mean(res * res, axis=-1, keepdims=True)
    o_ref[...] = res * lax.rsqrt(ms + NORM_EPS) * g_ref[...]


def _merge(x2, ya, yb, norm_in, w_gates, w_a, w_b, w_o, norm_w):
    bt = x2.shape[0]
    tm = MERGE_TM
    const = lambda i: (0, 0)
    in_specs = [
        pl.BlockSpec((tm, D_MODEL), lambda i: (i, 0)),
        pl.BlockSpec((N_PAIRS, tm, LANES), lambda i: (0, i, 0)),
        pl.BlockSpec((GD_HEADS, tm, LANES), lambda i: (0, i, 0)),
        pl.BlockSpec((1, D_MODEL), const),
        pl.BlockSpec((D_MODEL, 2 * D_MODEL), const),
        pl.BlockSpec((RW_WIDTH, D_MODEL), const),
        pl.BlockSpec((GD_WIDTH, D_MODEL), const),
        pl.BlockSpec((D_MODEL, D_MODEL), const),
        pl.BlockSpec((1, D_MODEL), const),
    ]
    return pl.pallas_call(
        _merge_kernel,
        grid=(bt // tm,),
        in_specs=in_specs,
        out_specs=pl.BlockSpec((tm, D_MODEL), lambda i: (i, 0)),
        out_shape=jax.ShapeDtypeStruct((bt, D_MODEL), F32),
        compiler_params=pltpu.CompilerParams(
            dimension_semantics=("arbitrary",), vmem_limit_bytes=VMEM_LIMIT),
        name="merge_out_proj",
    )(x2, ya, yb, norm_in, w_gates, w_a, w_b, w_o, norm_w)


def _block_lower_tri(n, chunk):
    i = jnp.arange(n)
    return (((i[:, None] // chunk) == (i[None, :] // chunk)) & (i[None, :] <= i[:, None])).astype(BF16)


def _layer(x2, batch, seq_len, norm_in_w, w_in, rw_mu, rw_w0, rw_w2, rw_a0, rw_a2, rw_k_k,
           rw_k_a, rw_r_k, rw_gn_w, rw_gn_b, gd_conv_w, gd_A_log, gd_dt_bias, gd_o_norm_w,
           w_branch_a, w_branch_b, w_out):
    c_gates = C_GATES + 2 * GD_HEADS
    w_main = w_in[:, :C_GATES].astype(BF16)
    w_gates = w_in[:, c_gates:].astype(BF16)
    w_ba = jnp.pad(w_in[:, C_GATES:c_gates], ((0, 0), (0, LANES - 2 * GD_HEADS))).astype(BF16)

    prw, zrw, pqkv, zgd, ba = _projection(x2, norm_in_w.reshape(1, D_MODEL), w_main, w_ba)

    ltri = _block_lower_tri(CUM_ROWS, RW_CHUNK)
    pair = lambda p: p.reshape(N_PAIRS, 1, LANES)
    zero = jnp.zeros((RW_RANK, RW_WIDTH), F32)
    wda = jnp.concatenate([jnp.concatenate([rw_w2, zero], axis=1),
                           jnp.concatenate([zero, rw_a2], axis=1)], axis=0).astype(BF16)
    mu = rw_mu.reshape(RW_SHIFT_COLS // LANES, 1, LANES)
    ya = _rwkv_branch(prw, zrw, ltri, wda, mu, pair(rw_w0), pair(rw_a0), pair(rw_k_k),
                      pair(rw_k_a), pair(rw_r_k.reshape(RW_WIDTH)), pair(rw_gn_w),
                      pair(rw_gn_b), batch, seq_len)

    head = lambda p: jnp.pad(p.astype(F32), (GD_HEADS, LANES - 2 * GD_HEADS)).reshape(1, LANES)
    conv_w = jnp.swapaxes(gd_conv_w.reshape(GD_CONV, 3 * GD_HEADS, LANES), 0, 1)
    yb = _gdn_branch(pqkv, zgd, ba, _block_lower_tri(CUM_ROWS, GD_CHUNK), conv_w,
                     head(gd_A_log), head(gd_dt_bias),
                     gd_o_norm_w.reshape(1, GD_HEAD), batch, seq_len)
    return w_gates, ya, yb


def kernel(x, norm_in_w, w_in, rw_mu, rw_w0, rw_w2, rw_a0, rw_a2, rw_k_k, rw_k_a, rw_r_k,
           rw_gn_w, rw_gn_b, gd_conv_w, gd_A_log, gd_dt_bias, gd_o_norm_w,
           w_branch_a, w_branch_b, w_out, norm_out_w):
    batch, seq_len, d = x.shape
    assert d == D_MODEL and norm_in_w.shape[0] == 1
    assert seq_len % max(PROJ_TM, REC_TT) == 0 and (batch * seq_len) % MERGE_TM == 0
    x2 = x.reshape(batch * seq_len, d)
    w_gates, ya, yb = _layer(
        x2, batch, seq_len, norm_in_w[0], w_in[0], rw_mu[0], rw_w0[0], rw_w2[0], rw_a0[0],
        rw_a2[0], rw_k_k[0], rw_k_a[0], rw_r_k[0], rw_gn_w[0], rw_gn_b[0], gd_conv_w[0],
        gd_A_log[0], gd_dt_bias[0], gd_o_norm_w[0], w_branch_a[0], w_branch_b[0], w_out[0])
    out = _merge(x2, ya, yb, norm_in_w.reshape(1, D_MODEL), w_gates,
                 w_branch_a[0].astype(BF16), w_branch_b[0].astype(BF16), w_out[0].astype(BF16),
                 norm_out_w.reshape(1, D_MODEL))
    return out.reshape(batch, seq_len, d)
```

```python
import jax
import jax.numpy as jnp
from jax import lax
from jax.experimental import pallas as pl
from jax.experimental.pallas import tpu as pltpu

F32 = jnp.float32
BF16 = jnp.bfloat16

D_MODEL = 1024
RW_WIDTH = 512
RW_HEAD = 64
RW_RANK = 64
RW_GN_EPS = RW_HEAD * 1e-5
DECAY_SCALE = -0.6065306597126334
GD_WIDTH = 512
GD_HEAD = 128
GD_HEADS = GD_WIDTH // GD_HEAD
GD_CONV = 4
NORM_EPS = 1e-6
L2_EPS = 1e-12

LANES = 128
RW_CHUNK = 64
GD_CHUNK = 128
N_PAIRS = RW_WIDTH // LANES
RW_SHIFT_COLS = 3 * RW_WIDTH + 2 * RW_RANK
C_ZRW = RW_SHIFT_COLS
C_QKV = C_ZRW + RW_WIDTH
C_ZGD = C_QKV + 3 * GD_WIDTH
C_GATES = C_ZGD + GD_WIDTH

PROJ_TM = 512
CARRY_ROWS = 8
REC_TT = 512
CUM_ROWS = 256
MERGE_TM = 512
MERGE_COLS = 512
VMEM_LIMIT = 56 * 1024 * 1024


def _mm(a, b):
    return jnp.dot(a.astype(BF16), b.astype(BF16), preferred_element_type=F32)


def _mm_nt(a, b):
    return lax.dot_general(a.astype(BF16), b.astype(BF16), (((1,), (1,)), ((), ())),
                           preferred_element_type=F32)


def _mm_tn(a, b):
    return lax.dot_general(a.astype(BF16), b.astype(BF16), (((0,), (0,)), ((), ())),
                           preferred_element_type=F32)


def _sigmoid(x):
    return 0.5 * jnp.tanh(0.5 * x) + 0.5


def _silu(x):
    h = 0.5 * x
    return h * jnp.tanh(h) + h


def _softplus(x):
    return jnp.maximum(x, 0.0) + jnp.log(1.0 + jnp.exp(-jnp.abs(x)))


def _split3(x):
    hi = x.astype(BF16)
    r1 = x - hi.astype(F32)
    mid = r1.astype(BF16)
    lo = (r1 - mid.astype(F32)).astype(BF16)
    return hi, mid, lo


def _blk(idx, size):
    return lax.shift_right_logical(idx, size.bit_length() - 1)


def _tri_masks(n, chunk):
    ri = lax.broadcasted_iota(jnp.int32, (n, n), 0)
    ci = lax.broadcasted_iota(jnp.int32, (n, n), 1)
    same = _blk(ri, chunk) == _blk(ci, chunk)
    return ri, ci, same


def _unit_lower_inverses(lmats, ri, ci, chunk):
    eye = (ri == ci).astype(F32)
    blk8 = _blk(ri, 8) == _blk(ci, 8)
    n = lmats[0].shape[0]
    ds = [jnp.where(blk8, l, 0.0) for l in lmats]
    dbs = [d.astype(BF16) for d in ds]
    d2s = [_mm(d, d).astype(BF16) for d in dbs]
    ts = [eye + d for d in ds]
    xs = [_mm(d2, jnp.concatenate([t.astype(BF16), d2], axis=1)) for t, d2 in zip(ts, d2s)]
    ts = [t + x[:, :n] for t, x in zip(ts, xs)]
    ts = [t + _mm(x[:, n:], t) for t, x in zip(ts, xs)]
    s = 8
    while s < chunk:
        groups = range(n // (2 * s))
        lower = lambda x: jnp.concatenate(
            [x[g * 2 * s + s:(g + 1) * 2 * s] for g in groups], axis=0)
        zeros = jnp.zeros((s, n), F32)
        rc = lax.broadcasted_iota(jnp.int32, (n // 2, n), 0)
        cl = lax.broadcasted_iota(jnp.int32, (n // 2, n), 1)
        rl = _blk(rc, s) * (2 * s) + s + (rc & (s - 1))
        off = (_blk(rl, 2 * s) == _blk(cl, 2 * s)) & (_blk(rl, s) != _blk(cl, s))
        tbs = [t.astype(BF16) for t in ts]
        xs = [_mm(jnp.where(off, lower(l), 0.0), t) for l, t in zip(lmats, tbs)]
        xs = [jnp.concatenate([p for g in groups for p in (zeros, x[g * s:(g + 1) * s])], axis=0)
              for x in xs]
        ys = [_mm(lower(t), x) for t, x in zip(ts, xs)]
        ts = [jnp.concatenate([p for g in groups
                               for p in (t[g * 2 * s:g * 2 * s + s],
                                         t[g * 2 * s + s:(g + 1) * 2 * s] + y[g * s:(g + 1) * s])],
                              axis=0)
              for t, y in zip(ts, ys)]
        s *= 2
    return [t.astype(BF16) for t in ts]


def _chunk_cumsum(ltri, x):
    return jnp.concatenate([_mm(ltri, x[r0:r0 + CUM_ROWS]) for r0 in range(0, x.shape[0], CUM_ROWS)],
                           axis=0)


def _proj_kernel(x_ref, g_ref, w_ref, prw_ref, zrw_ref, pqkv_ref, zgd_ref, ba_ref):
    xc = x_ref[...]
    inv = lax.rsqrt(jnp.mean(xc * xc, axis=-1, keepdims=True) + NORM_EPS)
    xb = (xc * g_ref[...]).astype(BF16)

    def proj(w):
        return jnp.dot(xb, w, preferred_element_type=F32) * inv

    def lane_groups(c0, n_groups, out_ref):
        for g0 in range(0, n_groups, 4):
            ng = min(4, n_groups - g0)
            pe = proj(w_ref[:, c0 + g0 * LANES:c0 + (g0 + ng) * LANES])
            for j in range(ng):
                out_ref[g0 + j] = pe[:, j * LANES:(j + 1) * LANES].astype(BF16)

    lane_groups(0, RW_SHIFT_COLS // LANES, prw_ref)
    lane_groups(C_ZRW, N_PAIRS, zrw_ref)
    lane_groups(C_QKV, 3 * GD_HEADS, pqkv_ref)
    lane_groups(C_ZGD, GD_HEADS, zgd_ref)
    ba_ref[...] = proj(w_ref[:, C_GATES:C_GATES + LANES])


def _projection(x2, norm_w, w_bf):
    bt = x2.shape[0]
    tm = PROJ_TM
    const = lambda i: (0, 0)
    slabs = lambda n: pl.BlockSpec((n, tm, LANES), lambda i: (0, i, 0))
    n_rw = RW_SHIFT_COLS // LANES
    out_shapes = (
        jax.ShapeDtypeStruct((n_rw, bt, LANES), BF16),
        jax.ShapeDtypeStruct((N_PAIRS, bt, LANES), BF16),
        jax.ShapeDtypeStruct((3 * GD_HEADS, bt, LANES), BF16),
        jax.ShapeDtypeStruct((GD_HEADS, bt, LANES), BF16),
        jax.ShapeDtypeStruct((bt, LANES), F32),
    )
    out_specs = (
        slabs(n_rw), slabs(N_PAIRS), slabs(3 * GD_HEADS), slabs(GD_HEADS),
        pl.BlockSpec((tm, LANES), lambda i: (i, 0)),
    )
    in_specs = [
        pl.BlockSpec((tm, D_MODEL), lambda i: (i, 0)),
        pl.BlockSpec((1, D_MODEL), const),
        pl.BlockSpec((D_MODEL, C_GATES + LANES), const, pipeline_mode=pl.Buffered(1)),
    ]
    return pl.pallas_call(
        _proj_kernel,
        grid=(bt // tm,),
        in_specs=in_specs,
        out_specs=out_specs,
        out_shape=out_shapes,
        compiler_params=pltpu.CompilerParams(
            dimension_semantics=("arbitrary",), vmem_limit_bytes=VMEM_LIMIT),
        name="rmsnorm_in_proj",
    )(x2, norm_w, w_bf)


def _rwkv_kernel(prw_ref, z_ref, ltri_ref, wda_ref, mu_ref, w0_ref, a0_ref,
                 kk_ref, ka_ref, rk_ref, gnw_ref, gnb_ref, y_ref, s_scr, carry_scr):
    t_idx = pl.program_id(1)

    tt = z_ref.shape[1]
    n_chunks = tt // RW_CHUNK

    @pl.when(t_idx == 0)
    def _():
        s_scr[...] = jnp.zeros_like(s_scr)
        carry_scr[...] = jnp.zeros_like(carry_scr)

    def token_shift(g):
        cur = prw_ref[g].astype(F32)
        prev = pltpu.roll(jnp.concatenate([carry_scr[g], cur], axis=0), 1, 0)[CARRY_ROWS:]
        carry_scr[g] = cur[tt - CARRY_ROWS:]
        return cur + (prev - cur) * mu_ref[g]

    lane = lax.broadcasted_iota(jnp.int32, (1, LANES), 1)
    head_a = lane < RW_HEAD
    ri, ci, same = _tri_masks(LANES, RW_CHUNK)
    strict = same & (ci < ri)
    causal = same & (ci <= ri)

    def head_sum(x):
        sa = jnp.sum(jnp.where(head_a, x, 0.0), axis=-1, keepdims=True)
        sb = jnp.sum(jnp.where(head_a, 0.0, x), axis=-1, keepdims=True)
        return jnp.where(head_a, jnp.broadcast_to(sa, x.shape), jnp.broadcast_to(sb, x.shape))
    ltri = ltri_ref[...]

    def pair_rows(x):
        return jnp.concatenate([jnp.where(head_a, x, 0.0), jnp.where(head_a, 0.0, x)], axis=0)

    wa = token_shift(3 * N_PAIRS)
    da = jnp.where(lane < RW_RANK, jnp.tanh(wa), wa)
    pre = _mm(da, wda_ref[...])

    bonus, p_end, work = [], [], []
    for j in range(N_PAIRS):
        r = token_shift(j)
        k = token_shift(N_PAIRS + j)
        v = token_shift(2 * N_PAIRS + j)
        cols = slice(j * LANES, (j + 1) * LANES)
        lw = DECAY_SCALE * _sigmoid(w0_ref[j] + pre[:, cols])
        lr = _sigmoid(a0_ref[j] + pre[:, RW_WIDTH + j * LANES:RW_WIDTH + (j + 1) * LANES])
        kkr = k * kk_ref[j]
        kk = kkr * lax.rsqrt(head_sum(kkr * kkr) + L2_EPS)
        k2 = k * (1.0 + (lr - 1.0) * ka_ref[j])
        b = kk * lr
        bonus.append(head_sum(r * k2 * rk_ref[j]) * v)
        lw_hi = lw.astype(BF16)
        lw_lo = (lw - lw_hi.astype(F32)).astype(BF16)
        cl2 = _chunk_cumsum(ltri, jnp.concatenate([lw_hi, lw_lo], axis=1))
        clw = cl2[:, :LANES] + cl2[:, LANES:]
        e_neg = jnp.exp(-clw)
        rt = r * jnp.exp(clw)
        at = -kk * jnp.exp(clw - lw)
        bt = b * e_neg
        kt = k2 * e_neg
        for c in range(n_chunks):
            sl = slice(c * RW_CHUNK, (c + 1) * RW_CHUNK)
            p_c = jnp.exp(clw[(c + 1) * RW_CHUNK - 1:(c + 1) * RW_CHUNK, :])
            p_end.append(jnp.sum(jnp.where(ri == ci, p_c, 0.0), axis=-1, keepdims=True))
            work.append(dict(
                ar=jnp.concatenate([pair_rows(at[sl]), pair_rows(rt[sl])], axis=0).astype(BF16),
                bk=jnp.concatenate([bt[sl], bt[sl], kt[sl], kt[sl]], axis=0).astype(BF16),
                v=pair_rows(v[sl]).astype(BF16),
                bk_end=jnp.concatenate([pair_rows(bt[sl] * p_c), pair_rows(kt[sl] * p_c)],
                                       axis=0).astype(BF16)))

    n = len(work)
    causal2 = jnp.concatenate([causal, causal], axis=1)
    zeros = jnp.zeros((LANES, LANES), BF16)
    ms = [_mm_nt(w["ar"], w["bk"]) for w in work]
    lbs = [jnp.where(strict, m[:LANES, :LANES], 0.0) for m in ms]
    aks = [jnp.where(strict, m[:LANES, LANES:], 0.0).astype(BF16) for m in ms]
    rbks = [jnp.where(causal2, m[LANES:], 0.0).astype(BF16) for m in ms]
    t_invs = _unit_lower_inverses(lbs, ri, ci, RW_CHUNK)
    akvs = [_mm(ak, w["v"]).astype(BF16) for ak, w in zip(aks, work)]
    tws = [_mm(t, jnp.concatenate([w["ar"][:LANES], akv], axis=1)).astype(BF16)
           for t, w, akv in zip(t_invs, work, akvs)]
    xs = [jnp.concatenate([tw, jnp.concatenate([zeros, w["v"]], axis=1)], axis=0)
          for tw, w in zip(tws, work)]
    qys = [_mm(rbk, x) for rbk, x in zip(rbks, xs)]
    zs = [_mm_tn(w["bk_end"], x) for w, x in zip(work, xs)]
    zas = [z[:, :LANES].astype(BF16) for z in zs]

    states = [None] * n
    for c in range(n_chunks):
        for j in range(N_PAIRS):
            i = j * n_chunks + c
            if c == 0:
                states[i] = s_scr[j]
            h0 = states[i]
            h1 = p_end[i] * h0 + _mm(zas[i], h0) + zs[i][:, LANES:]
            if c == n_chunks - 1:
                s_scr[j] = h1
            else:
                states[i + 1] = h1

    for j in range(N_PAIRS):
        ys = []
        for c in range(n_chunks):
            i = j * n_chunks + c
            qc = work[i]["ar"][LANES:] + qys[i][:, :LANES]
            y_p = _mm(qc, states[i]) + qys[i][:, LANES:]
            ys.append(y_p[:RW_CHUNK] + y_p[RW_CHUNK:])
        y = jnp.concatenate(ys, axis=0)
        inv_n = 1.0 / RW_HEAD
        yc = y - head_sum(y) * inv_n
        var = head_sum(yc * yc) * inv_n
        yn = yc * lax.rsqrt(var + RW_GN_EPS) * gnw_ref[j] + gnb_ref[j]
        y_ref[j] = ((yn + bonus[j]) * _silu(z_ref[j].astype(F32))).astype(y_ref.dtype)


def _rwkv_branch(prw, zrw, ltri, wda, mu, w0, a0, k_k, k_a, r_k, gn_w, gn_b, batch, seq_len):
    tt = REC_TT
    nt = seq_len // tt
    bt = batch * seq_len
    tok = lambda n: pl.BlockSpec((n, tt, LANES), lambda b, t: (0, b * nt + t, 0))
    par = pl.BlockSpec((N_PAIRS, 1, LANES), lambda b, t: (0, 0, 0))
    n_rw = RW_SHIFT_COLS // LANES
    in_specs = [
        tok(n_rw),
        tok(N_PAIRS),
        pl.BlockSpec((CUM_ROWS, CUM_ROWS), lambda b, t: (0, 0)),
        pl.BlockSpec((LANES, 2 * RW_WIDTH), lambda b, t: (0, 0)),
        pl.BlockSpec((n_rw, 1, LANES), lambda b, t: (0, 0, 0)),
        par, par, par, par, par, par, par,
    ]
    return pl.pallas_call(
        _rwkv_kernel,
        grid=(batch, nt),
        in_specs=in_specs,
        out_specs=tok(N_PAIRS),
        out_shape=jax.ShapeDtypeStruct((N_PAIRS, bt, LANES), BF16),
        scratch_shapes=[pltpu.VMEM((N_PAIRS, LANES, LANES), F32),
                        pltpu.VMEM((n_rw, CARRY_ROWS, LANES), F32)],
        compiler_params=pltpu.CompilerParams(
            dimension_semantics=("arbitrary", "arbitrary"), vmem_limit_bytes=VMEM_LIMIT),
        name="rwkv7_chunked",
    )(prw, zrw, ltri, wda, mu, w0, a0, k_k, k_a, r_k, gn_w, gn_b)


def _gdn_kernel(pqkv_ref, z_ref, ba_ref, ltri_ref, cw_ref, alog_ref, dtb_ref, onw_ref,
                y_ref, s_scr, carry_scr):
    t_idx = pl.program_id(1)

    tt = ba_ref.shape[0]

    @pl.when(t_idx == 0)
    def _():
        s_scr[...] = jnp.zeros_like(s_scr)
        carry_scr[:, :CARRY_ROWS, :] = jnp.zeros((3 * GD_HEADS, CARRY_ROWS, LANES), F32)

    for g in range(3 * GD_HEADS):
        carry_scr[g, CARRY_ROWS:, :] = pqkv_ref[g].astype(F32)

    def conv_silu(g):
        cur = carry_scr[g, CARRY_ROWS:, :]
        taps = cw_ref[g]
        acc = cur * taps[GD_CONV - 1:GD_CONV]
        for s in range(1, GD_CONV):
            acc = acc + carry_scr[g, pl.ds(CARRY_ROWS - s, tt), :] * taps[GD_CONV - 1 - s:GD_CONV - s]
        carry_scr[g, :CARRY_ROWS, :] = cur[tt - CARRY_ROWS:]
        return _silu(acc)

    n_super = tt // LANES
    per_super = LANES // GD_CHUNK
    ri, ci, same = _tri_masks(LANES, GD_CHUNK)
    strict = same & (ci < ri)
    causal = same & (ci <= ri)
    ba = ba_ref[...]
    g_all = -jnp.exp(alog_ref[...]) * _softplus(ba + dtb_ref[...])
    gc3 = _chunk_cumsum(ltri_ref[...], jnp.concatenate(_split3(g_all), axis=1))
    gc_all = gc3[:, :LANES] + gc3[:, LANES:2 * LANES] + gc3[:, 2 * LANES:]
    gc_all_t = gc_all.T

    work = []
    for h in range(GD_HEADS):
        q = conv_silu(h)
        k = conv_silu(GD_HEADS + h)
        v = conv_silu(2 * GD_HEADS + h)
        qn = q * (lax.rsqrt(jnp.sum(q * q, axis=-1, keepdims=True) + L2_EPS) * (GD_HEAD ** -0.5))
        kn = k * lax.rsqrt(jnp.sum(k * k, axis=-1, keepdims=True) + L2_EPS)
        beta = jnp.broadcast_to(_sigmoid(ba[:, h:h + 1]), (tt, LANES))
        gc_col = jnp.broadcast_to(gc_all[:, GD_HEADS + h:GD_HEADS + h + 1], (tt, LANES))
        gc_row = gc_all_t[GD_HEADS + h:GD_HEADS + h + 1, :]
        e_col = jnp.exp(gc_col)
        kb = kn * beta
        vb = v * beta
        qd = qn * e_col
        kbe = kb * e_col
        for sc in range(n_super):
            sl = slice(sc * LANES, (sc + 1) * LANES)
            diff = gc_col[sl] - gc_row[:, sl]
            chunks = []
            for cc in range(per_super):
                rows = slice(sc * LANES + cc * GD_CHUNK, sc * LANES + (cc + 1) * GD_CHUNK)
                g_last = gc_col[rows.stop - 1:rows.stop, :]
                chunks.append(dict(kte=(kn[rows] * jnp.exp(g_last - gc_col[rows])).astype(BF16),
                                   decay=jnp.exp(g_last)))
            work.append(dict(dec=jnp.exp(jnp.where(causal, diff, -1e30)),
                             kq=jnp.concatenate([kb[sl], qn[sl]], axis=0).astype(BF16),
                             kn=kn[sl].astype(BF16),
                             rhs=jnp.concatenate([vb[sl], kbe[sl]], axis=1).astype(BF16),
                             qd=qd[sl], chunks=chunks))

    ms = [_mm_nt(w["kq"], w["kn"]) for w in work]
    lmats = [-jnp.where(strict, m[:LANES] * w["dec"], 0.0) for m, w in zip(ms, work)]
    qks = [(m[LANES:] * w["dec"]).astype(BF16) for m, w in zip(ms, work)]
    t_invs = _unit_lower_inverses(lmats, ri, ci, GD_CHUNK)
    uws = [_mm(t, w["rhs"]).astype(BF16) for t, w in zip(t_invs, work)]
    qws = [_mm(qk, uw) for qk, uw in zip(qks, uws)]
    for w, uw, qw in zip(work, uws, qws):
        w["qc"] = (w["qd"] - qw[:, LANES:]).astype(BF16)
        w["o0"] = qw[:, :LANES]
        for cc, ch in enumerate(w["chunks"]):
            tn = _mm_tn(ch["kte"], uw[cc * GD_CHUNK:(cc + 1) * GD_CHUNK])
            ch["tn_u"] = tn[:, :LANES]
            ch["tn_w"] = tn[:, LANES:].astype(BF16)

    state = [s_scr[h] for h in range(GD_HEADS)]
    for sc in range(n_super):
        for cc in range(per_super):
            for h in range(GD_HEADS):
                ch = work[h * n_super + sc]["chunks"][cc]
                s0 = state[h]
                ch["s0"] = s0
                state[h] = s0 * ch["decay"] - _mm(ch["tn_w"], s0) + ch["tn_u"]
    for h in range(GD_HEADS):
        s_scr[h] = state[h]

    for h in range(GD_HEADS):
        os = []
        for sc in range(n_super):
            w = work[h * n_super + sc]
            for cc, ch in enumerate(w["chunks"]):
                rows = slice(cc * GD_CHUNK, (cc + 1) * GD_CHUNK)
                os.append(_mm(w["qc"][rows], ch["s0"]) + w["o0"][rows])
        o = jnp.concatenate(os, axis=0)
        ms = jnp.mean(o * o, axis=-1, keepdims=True)
        on = o * lax.rsqrt(ms + NORM_EPS) * onw_ref[...]
        y_ref[h] = (on * _silu(z_ref[h].astype(F32))).astype(y_ref.dtype)


def _gdn_branch(pqkv, zgd, ba, ltri, conv_w, alog, dtb, onw, batch, seq_len):
    tt = REC_TT
    nt = seq_len // tt
    bt = batch * seq_len
    tok = lambda n: pl.BlockSpec((n, tt, LANES), lambda b, t: (0, b * nt + t, 0))
    par = pl.BlockSpec((1, LANES), lambda b, t: (0, 0))
    in_specs = [
        tok(3 * GD_HEADS),
        tok(GD_HEADS),
        pl.BlockSpec((tt, LANES), lambda b, t: (b * nt + t, 0)),
        pl.BlockSpec((CUM_ROWS, CUM_ROWS), lambda b, t: (0, 0)),
        pl.BlockSpec((3 * GD_HEADS, GD_CONV, LANES), lambda b, t: (0, 0, 0)),
        par, par,
        pl.BlockSpec((1, LANES), lambda b, t: (0, 0)),
    ]
    return pl.pallas_call(
        _gdn_kernel,
        grid=(batch, nt),
        in_specs=in_specs,
        out_specs=tok(GD_HEADS),
        out_shape=jax.ShapeDtypeStruct((GD_HEADS, bt, LANES), BF16),
        scratch_shapes=[pltpu.VMEM((GD_HEADS, LANES, LANES), F32),
                        pltpu.VMEM((3 * GD_HEADS, CARRY_ROWS + tt, LANES), F32)],
        compiler_params=pltpu.CompilerParams(
            dimension_semantics=("arbitrary", "arbitrary"), vmem_limit_bytes=VMEM_LIMIT),
        name="gated_delta_chunked",
    )(pqkv, zgd, ba, ltri, conv_w, alog, dtb, onw)


def _merge_kernel(x_ref, ya_ref, yb_ref, gin_ref, wg_ref, wa_ref, wb_ref, wo_ref, g_ref, o_ref):
    xc = x_ref[...]
    inv = lax.rsqrt(jnp.mean(xc * xc, axis=-1, keepdims=True) + NORM_EPS)
    xb = (xc * gin_ref[...]).astype(BF16)
    ya = jnp.concatenate([ya_ref[j] for j in range(N_PAIRS)], axis=1)
    yb = jnp.concatenate([yb_ref[j] for j in range(GD_HEADS)], axis=1)
    blocks = []
    for c0 in range(0, D_MODEL, MERGE_COLS):
        cols = slice(c0, c0 + MERGE_COLS)
        gate_a = _sigmoid(jnp.dot(xb, wg_ref[:, cols], preferred_element_type=F32) * inv)
        gate_b = _sigmoid(jnp.dot(xb, wg_ref[:, D_MODEL + c0:D_MODEL + c0 + MERGE_COLS],
                                  preferred_element_type=F32) * inv)
        ma = jnp.dot(ya, wa_ref[:, cols], preferred_element_type=F32)
        mb = jnp.dot(yb, wb_ref[:, cols], preferred_element_type=F32)
        blocks.append((gate_a * ma + gate_b * mb).astype(BF16))
    merged = jnp.concatenate(blocks, axis=1)
    res = xc + jnp.dot(merged, wo_ref[...], preferred_element_type=F32)
    ms = jnp.mean(res * res, axis=-1, keepdims=True)
    o_ref[...] = res * lax.rsqrt(ms + NORM_EPS) * g_ref[...]


def _merge(x2, ya, yb, norm_in, w_gates, w_a, w_b, w_o, norm_w):
    bt = x2.shape[0]
    tm = MERGE_TM
    const = lambda i: (0, 0)
    in_specs = [
        pl.BlockSpec((tm, D_MODEL), lambda i: (i, 0)),
        pl.BlockSpec((N_PAIRS, tm, LANES), lambda i: (0, i, 0)),
        pl.BlockSpec((GD_HEADS, tm, LANES), lambda i: (0, i, 0)),
        pl.BlockSpec((1, D_MODEL), const),
        pl.BlockSpec((D_MODEL, 2 * D_MODEL), const),
        pl.BlockSpec((RW_WIDTH, D_MODEL), const),
        pl.BlockSpec((GD_WIDTH, D_MODEL), const),
        pl.BlockSpec((D_MODEL, D_MODEL), const),
        pl.BlockSpec((1, D_MODEL), const),
    ]
    return pl.pallas_call(
        _merge_kernel,
        grid=(bt // tm,),
        in_specs=in_specs,
        out_specs=pl.BlockSpec((tm, D_MODEL), lambda i: (i, 0)),
        out_shape=jax.ShapeDtypeStruct((bt, D_MODEL), F32),
        compiler_params=pltpu.CompilerParams(
            dimension_semantics=("arbitrary",), vmem_limit_bytes=VMEM_LIMIT),
        name="merge_out_proj",
    )(x2, ya, yb, norm_in, w_gates, w_a, w_b, w_o, norm_w)


def _block_lower_tri(n, chunk):
    i = jnp.arange(n)
    return (((i[:, None] // chunk) == (i[None, :] // chunk)) & (i[None, :] <= i[:, None])).astype(BF16)


def _layer(x2, batch, seq_len, norm_in_w, w_in, rw_mu, rw_w0, rw_w2, rw_a0, rw_a2, rw_k_k,
           rw_k_a, rw_r_k, rw_gn_w, rw_gn_b, gd_conv_w, gd_A_log, gd_dt_bias, gd_o_norm_w,
           w_branch_a, w_branch_b, w_out):
    w_bf = w_in.astype(BF16)
    w_gates = w_bf[:, C_GATES + 2 * GD_HEADS:]

    prw, zrw, pqkv, zgd, ba = _projection(x2, norm_in_w.reshape(1, D_MODEL), w_bf)

    ltri = _block_lower_tri(CUM_ROWS, RW_CHUNK)
    pair = lambda p: p.reshape(N_PAIRS, 1, LANES)
    zero = jnp.zeros((RW_RANK, RW_WIDTH), F32)
    wda = jnp.concatenate([jnp.concatenate([rw_w2, zero], axis=1),
                           jnp.concatenate([zero, rw_a2], axis=1)], axis=0).astype(BF16)
    mu = rw_mu.reshape(RW_SHIFT_COLS // LANES, 1, LANES)
    ya = _rwkv_branch(prw, zrw, ltri, wda, mu, pair(rw_w0), pair(rw_a0), pair(rw_k_k),
                      pair(rw_k_a), pair(rw_r_k.reshape(RW_WIDTH)), pair(rw_gn_w),
                      pair(rw_gn_b), batch, seq_len)

    head = lambda p: jnp.pad(p.astype(F32), (GD_HEADS, LANES - 2 * GD_HEADS)).reshape(1, LANES)
    conv_w = jnp.swapaxes(gd_conv_w.reshape(GD_CONV, 3 * GD_HEADS, LANES), 0, 1)
    yb = _gdn_branch(pqkv, zgd, ba, _block_lower_tri(CUM_ROWS, GD_CHUNK), conv_w,
                     head(gd_A_log), head(gd_dt_bias),
                     gd_o_norm_w.reshape(1, GD_HEAD), batch, seq_len)
    return w_gates, ya, yb


def kernel(x, norm_in_w, w_in, rw_mu, rw_w0, rw_w2, rw_a0, rw_a2, rw_k_k, rw_k_a, rw_r_k,
           rw_gn_w, rw_gn_b, gd_conv_w, gd_A_log, gd_dt_bias, gd_o_norm_w,
           w_branch_a, w_branch_b, w_out, norm_out_w):
    batch, seq_len, d = x.shape
    assert d == D_MODEL and norm_in_w.shape[0] == 1
    assert seq_len % max(PROJ_TM, REC_TT) == 0 and (batch * seq_len) % MERGE_TM == 0
    x2 = x.reshape(batch * seq_len, d)
    w_gates, ya, yb = _layer(
        x2, batch, seq_len, norm_in_w[0], w_in[0], rw_mu[0], rw_w0[0], rw_w2[0], rw_a0[0],
        rw_a2[0], rw_k_k[0], rw_k_a[0], rw_r_k[0], rw_gn_w[0], rw_gn_b[0], gd_conv_w[0],
        gd_A_log[0], gd_dt_bias[0], gd_o_norm_w[0], w_branch_a[0], w_branch_b[0], w_out[0])
    out = _merge(x2, ya, yb, norm_in_w.reshape(1, D_MODEL), w_gates,
                 w_branch_a[0].astype(BF16), w_branch_b[0].astype(BF16), w_out[0].astype(BF16),
                 norm_out_w.reshape(1, D_MODEL))
    return out.reshape(batch, seq_len, d)
```

```python
import functools

import jax
import jax.numpy as jnp
from jax import lax
from jax.experimental import pallas as pl
from jax.experimental.pallas import tpu as pltpu

F32 = jnp.float32
BF16 = jnp.bfloat16

D_MODEL = 1024
RW_WIDTH = 512
RW_HEAD = 64
RW_RANK = 64
RW_GN_EPS = RW_HEAD * 1e-5
DECAY_SCALE = -0.6065306597126334
GD_WIDTH = 512
GD_HEAD = 128
GD_HEADS = GD_WIDTH // GD_HEAD
GD_CONV = 4
NORM_EPS = 1e-6
L2_EPS = 1e-12

LANES = 128
RW_CHUNK = 64
GD_CHUNK = 128
N_PAIRS = RW_WIDTH // LANES
RW_SHIFT_COLS = 3 * RW_WIDTH + 2 * RW_RANK
C_ZRW = RW_SHIFT_COLS
C_QKV = C_ZRW + RW_WIDTH
C_ZGD = C_QKV + 3 * GD_WIDTH
C_GATES = C_ZGD + GD_WIDTH

PROJ_TM = 512
CARRY_ROWS = 8
REC_TT = 512
CUM_ROWS = 256
MERGE_TM = 512
MERGE_COLS = 512
VMEM_LIMIT = 56 * 1024 * 1024


def _mm(a, b):
    return jnp.dot(a.astype(BF16), b.astype(BF16), preferred_element_type=F32)


def _mm_nt(a, b):
    return lax.dot_general(a.astype(BF16), b.astype(BF16), (((1,), (1,)), ((), ())),
                           preferred_element_type=F32)


def _mm_tn(a, b):
    return lax.dot_general(a.astype(BF16), b.astype(BF16), (((0,), (0,)), ((), ())),
                           preferred_element_type=F32)


def _sigmoid(x):
    return 0.5 * jnp.tanh(0.5 * x) + 0.5


def _silu(x):
    h = 0.5 * x
    return h * jnp.tanh(h) + h


def _softplus(x):
    return jnp.maximum(x, 0.0) + jnp.log(1.0 + jnp.exp(-jnp.abs(x)))


def _split3(x):
    hi = x.astype(BF16)
    r1 = x - hi.astype(F32)
    mid = r1.astype(BF16)
    lo = (r1 - mid.astype(F32)).astype(BF16)
    return hi, mid, lo


def _blk(idx, size):
    return lax.shift_right_logical(idx, size.bit_length() - 1)


def _tri_masks(n, chunk):
    ri = lax.broadcasted_iota(jnp.int32, (n, n), 0)
    ci = lax.broadcasted_iota(jnp.int32, (n, n), 1)
    same = _blk(ri, chunk) == _blk(ci, chunk)
    return ri, ci, same


def _unit_lower_inverses(lmats, ri, ci, chunk):
    eye = (ri == ci).astype(F32)
    blk8 = _blk(ri, 8) == _blk(ci, 8)
    n = lmats[0].shape[0]
    ds = [jnp.where(blk8, l, 0.0) for l in lmats]
    dbs = [d.astype(BF16) for d in ds]
    d2s = [_mm(d, d).astype(BF16) for d in dbs]
    ts = [eye + d for d in ds]
    xs = [_mm(d2, jnp.concatenate([t.astype(BF16), d2], axis=1)) for t, d2 in zip(ts, d2s)]
    ts = [t + x[:, :n] for t, x in zip(ts, xs)]
    ts = [t + _mm(x[:, n:], t) for t, x in zip(ts, xs)]
    s = 8
    while s < chunk:
        groups = range(n // (2 * s))
        lower = lambda x: jnp.concatenate(
            [x[g * 2 * s + s:(g + 1) * 2 * s] for g in groups], axis=0)
        zeros = jnp.zeros((s, n), F32)
        rc = lax.broadcasted_iota(jnp.int32, (n // 2, n), 0)
        cl = lax.broadcasted_iota(jnp.int32, (n // 2, n), 1)
        rl = _blk(rc, s) * (2 * s) + s + (rc & (s - 1))
        off = (_blk(rl, 2 * s) == _blk(cl, 2 * s)) & (_blk(rl, s) != _blk(cl, s))
        tbs = [t.astype(BF16) for t in ts]
        xs = [_mm(jnp.where(off, lower(l), 0.0), t) for l, t in zip(lmats, tbs)]
        xs = [jnp.concatenate([p for g in groups for p in (zeros, x[g * s:(g + 1) * s])], axis=0)
              for x in xs]
        ys = [_mm(lower(t), x) for t, x in zip(ts, xs)]
        ts = [jnp.concatenate([p for g in groups
                               for p in (t[g * 2 * s:g * 2 * s + s],
                                         t[g * 2 * s + s:(g + 1) * 2 * s] + y[g * s:(g + 1) * s])],
                              axis=0)
              for t, y in zip(ts, ys)]
        s *= 2
    return [t.astype(BF16) for t in ts]


def _chunk_cumsum(ltri, x):
    return jnp.concatenate([_mm(ltri, x[r0:r0 + CUM_ROWS]) for r0 in range(0, x.shape[0], CUM_ROWS)],
                           axis=0)


def _proj_kernel(x_ref, g_ref, w_ref, mu_ref, cw_ref, prw_ref, zrw_ref, qkv_ref, zgd_ref, ba_ref,
                 carry_scr, win_scr, *, tiles_per_seq):
    tm = x_ref.shape[0]

    @pl.when(pl.program_id(0) % tiles_per_seq == 0)
    def _():
        carry_scr[...] = jnp.zeros_like(carry_scr)
        win_scr[:, :CARRY_ROWS, :] = jnp.zeros((win_scr.shape[0], CARRY_ROWS, LANES), F32)

    xc = x_ref[...]
    inv = lax.rsqrt(jnp.mean(xc * xc, axis=-1, keepdims=True) + NORM_EPS)
    xb = (xc * g_ref[...]).astype(BF16)

    def proj(w):
        return jnp.dot(xb, w, preferred_element_type=F32) * inv

    def token_shift(g, cur):
        prev = pltpu.roll(jnp.concatenate([carry_scr[g], cur], axis=0), 1, 0)[CARRY_ROWS:]
        carry_scr[g] = cur[tm - CARRY_ROWS:]
        return cur + (prev - cur) * mu_ref[g]

    def conv_silu(g, cur):
        win_scr[g, CARRY_ROWS:, :] = cur
        taps = cw_ref[g]
        acc = cur * taps[GD_CONV - 1:GD_CONV]
        for s in range(1, GD_CONV):
            acc = acc + win_scr[g, pl.ds(CARRY_ROWS - s, tm), :] * taps[GD_CONV - 1 - s:GD_CONV - s]
        win_scr[g, :CARRY_ROWS, :] = cur[tm - CARRY_ROWS:]
        return _silu(acc)

    def lane_groups(c0, n_groups, out_ref, post=None):
        for g0 in range(0, n_groups, 4):
            ng = min(4, n_groups - g0)
            pe = proj(w_ref[:, c0 + g0 * LANES:c0 + (g0 + ng) * LANES])
            for j in range(ng):
                slab = pe[:, j * LANES:(j + 1) * LANES]
                out_ref[g0 + j] = (slab if post is None else post(g0 + j, slab)).astype(BF16)

    lane_groups(0, RW_SHIFT_COLS // LANES, prw_ref, token_shift)
    lane_groups(C_ZRW, N_PAIRS, zrw_ref)
    lane_groups(C_QKV, 3 * GD_HEADS, qkv_ref, conv_silu)
    lane_groups(C_ZGD, GD_HEADS, zgd_ref)
    ba_ref[...] = proj(w_ref[:, C_GATES:C_GATES + LANES])


def _projection(x2, norm_w, w_bf, mu, conv_w, seq_len):
    bt = x2.shape[0]
    tm = PROJ_TM
    const = lambda i: (0, 0)
    slabs = lambda n: pl.BlockSpec((n, tm, LANES), lambda i: (0, i, 0))
    n_rw = RW_SHIFT_COLS // LANES
    out_shapes = (
        jax.ShapeDtypeStruct((n_rw, bt, LANES), BF16),
        jax.ShapeDtypeStruct((N_PAIRS, bt, LANES), BF16),
        jax.ShapeDtypeStruct((3 * GD_HEADS, bt, LANES), BF16),
        jax.ShapeDtypeStruct((GD_HEADS, bt, LANES), BF16),
        jax.ShapeDtypeStruct((bt, LANES), F32),
    )
    out_specs = (
        slabs(n_rw), slabs(N_PAIRS), slabs(3 * GD_HEADS), slabs(GD_HEADS),
        pl.BlockSpec((tm, LANES), lambda i: (i, 0)),
    )
    in_specs = [
        pl.BlockSpec((tm, D_MODEL), lambda i: (i, 0)),
        pl.BlockSpec((1, D_MODEL), const),
        pl.BlockSpec((D_MODEL, C_GATES + LANES), const, pipeline_mode=pl.Buffered(1)),
        pl.BlockSpec((n_rw, 1, LANES), lambda i: (0, 0, 0)),
        pl.BlockSpec((3 * GD_HEADS, GD_CONV, LANES), lambda i: (0, 0, 0)),
    ]
    return pl.pallas_call(
        functools.partial(_proj_kernel, tiles_per_seq=seq_len // tm),
        grid=(bt // tm,),
        in_specs=in_specs,
        out_specs=out_specs,
        out_shape=out_shapes,
        scratch_shapes=[pltpu.VMEM((n_rw, CARRY_ROWS, LANES), F32),
                        pltpu.VMEM((3 * GD_HEADS, CARRY_ROWS + tm, LANES), F32)],
        compiler_params=pltpu.CompilerParams(
            dimension_semantics=("arbitrary",), vmem_limit_bytes=VMEM_LIMIT),
        name="rmsnorm_in_proj",
    )(x2, norm_w, w_bf, mu, conv_w)


def _rwkv_kernel(xs_ref, z_ref, ltri_ref, wda_ref, w0_ref, a0_ref,
                 kk_ref, ka_ref, rk_ref, gnw_ref, gnb_ref, y_ref, s_scr):
    t_idx = pl.program_id(1)

    tt = z_ref.shape[1]
    n_chunks = tt // RW_CHUNK

    @pl.when(t_idx == 0)
    def _():
        s_scr[...] = jnp.zeros_like(s_scr)

    lane = lax.broadcasted_iota(jnp.int32, (1, LANES), 1)
    head_a = lane < RW_HEAD
    ri, ci, same = _tri_masks(LANES, RW_CHUNK)
    strict = same & (ci < ri)
    causal = same & (ci <= ri)

    def head_sum(x):
        sa = jnp.sum(jnp.where(head_a, x, 0.0), axis=-1, keepdims=True)
        sb = jnp.sum(jnp.where(head_a, 0.0, x), axis=-1, keepdims=True)
        return jnp.where(head_a, jnp.broadcast_to(sa, x.shape), jnp.broadcast_to(sb, x.shape))
    ltri = ltri_ref[...]

    def pair_rows(x):
        return jnp.concatenate([jnp.where(head_a, x, 0.0), jnp.where(head_a, 0.0, x)], axis=0)

    wa = xs_ref[3 * N_PAIRS].astype(F32)
    da = jnp.where(lane < RW_RANK, jnp.tanh(wa), wa)
    pre = _mm(da, wda_ref[...])

    bonus, p_end, work = [], [], []
    for j in range(N_PAIRS):
        r = xs_ref[j].astype(F32)
        k = xs_ref[N_PAIRS + j].astype(F32)
        v = xs_ref[2 * N_PAIRS + j].astype(F32)
        cols = slice(j * LANES, (j + 1) * LANES)
        lw = DECAY_SCALE * _sigmoid(w0_ref[j] + pre[:, cols])
        lr = _sigmoid(a0_ref[j] + pre[:, RW_WIDTH + j * LANES:RW_WIDTH + (j + 1) * LANES])
        kkr = k * kk_ref[j]
        kk = kkr * lax.rsqrt(head_sum(kkr * kkr) + L2_EPS)
        k2 = k * (1.0 + (lr - 1.0) * ka_ref[j])
        b = kk * lr
        bonus.append(head_sum(r * k2 * rk_ref[j]) * v)
        lw_hi = lw.astype(BF16)
        lw_lo = (lw - lw_hi.astype(F32)).astype(BF16)
        cl2 = _chunk_cumsum(ltri, jnp.concatenate([lw_hi, lw_lo], axis=1))
        clw = cl2[:, :LANES] + cl2[:, LANES:]
        e_neg = jnp.exp(-clw)
        rt = r * jnp.exp(clw)
        at = -kk * jnp.exp(clw - lw)
        bt = b * e_neg
        kt = k2 * e_neg
        for c in range(n_chunks):
            sl = slice(c * RW_CHUNK, (c + 1) * RW_CHUNK)
            p_c = jnp.exp(clw[(c + 1) * RW_CHUNK - 1:(c + 1) * RW_CHUNK, :])
            p_end.append(jnp.sum(jnp.where(ri == ci, p_c, 0.0), axis=-1, keepdims=True))
            work.append(dict(
                ar=jnp.concatenate([pair_rows(at[sl]), pair_rows(rt[sl])], axis=0).astype(BF16),
                bk=jnp.concatenate([bt[sl], bt[sl], kt[sl], kt[sl]], axis=0).astype(BF16),
                v=pair_rows(v[sl]).astype(BF16),
                bk_end=jnp.concatenate([pair_rows(bt[sl] * p_c), pair_rows(kt[sl] * p_c)],
                                       axis=0).astype(BF16)))

    n = len(work)
    causal2 = jnp.concatenate([causal, causal], axis=1)
    zeros = jnp.zeros((LANES, LANES), BF16)
    ms = [_mm_nt(w["ar"], w["bk"]) for w in work]
    lbs = [jnp.where(strict, m[:LANES, :LANES], 0.0) for m in ms]
    aks = [jnp.where(strict, m[:LANES, LANES:], 0.0).astype(BF16) for m in ms]
    rbks = [jnp.where(causal2, m[LANES:], 0.0).astype(BF16) for m in ms]
    t_invs = _unit_lower_inverses(lbs, ri, ci, RW_CHUNK)
    akvs = [_mm(ak, w["v"]).astype(BF16) for ak, w in zip(aks, work)]
    tws = [_mm(t, jnp.concatenate([w["ar"][:LANES], akv], axis=1)).astype(BF16)
           for t, w, akv in zip(t_invs, work, akvs)]
    xs = [jnp.concatenate([tw, jnp.concatenate([zeros, w["v"]], axis=1)], axis=0)
          for tw, w in zip(tws, work)]
    qys = [_mm(rbk, x) for rbk, x in zip(rbks, xs)]
    zs = [_mm_tn(w["bk_end"], x) for w, x in zip(work, xs)]
    zas = [z[:, :LANES].astype(BF16) for z in zs]

    states = [None] * n
    for c in range(n_chunks):
        for j in range(N_PAIRS):
            i = j * n_chunks + c
            if c == 0:
                states[i] = s_scr[j]
            h0 = states[i]
            h1 = p_end[i] * h0 + _mm(zas[i], h0) + zs[i][:, LANES:]
            if c == n_chunks - 1:
                s_scr[j] = h1
            else:
                states[i + 1] = h1

    for j in range(N_PAIRS):
        ys = []
        for c in range(n_chunks):
            i = j * n_chunks + c
            qc = work[i]["ar"][LANES:] + qys[i][:, :LANES]
            y_p = _mm(qc, states[i]) + qys[i][:, LANES:]
            ys.append(y_p[:RW_CHUNK] + y_p[RW_CHUNK:])
        y = jnp.concatenate(ys, axis=0)
        inv_n = 1.0 / RW_HEAD
        yc = y - head_sum(y) * inv_n
        var = head_sum(yc * yc) * inv_n
        yn = yc * lax.rsqrt(var + RW_GN_EPS) * gnw_ref[j] + gnb_ref[j]
        y_ref[j] = ((yn + bonus[j]) * _silu(z_ref[j].astype(F32))).astype(y_ref.dtype)


def _rwkv_branch(xs, zrw, ltri, wda, w0, a0, k_k, k_a, r_k, gn_w, gn_b, batch, seq_len):
    tt = REC_TT
    nt = seq_len // tt
    bt = batch * seq_len
    tok = lambda n: pl.BlockSpec((n, tt, LANES), lambda b, t: (0, b * nt + t, 0))
    par = pl.BlockSpec((N_PAIRS, 1, LANES), lambda b, t: (0, 0, 0))
    n_rw = RW_SHIFT_COLS // LANES
    in_specs = [
        tok(n_rw),
        tok(N_PAIRS),
        pl.BlockSpec((CUM_ROWS, CUM_ROWS), lambda b, t: (0, 0)),
        pl.BlockSpec((LANES, 2 * RW_WIDTH), lambda b, t: (0, 0)),
        par, par, par, par, par, par, par,
    ]
    return pl.pallas_call(
        _rwkv_kernel,
        grid=(batch, nt),
        in_specs=in_specs,
        out_specs=tok(N_PAIRS),
        out_shape=jax.ShapeDtypeStruct((N_PAIRS, bt, LANES), BF16),
        scratch_shapes=[pltpu.VMEM((N_PAIRS, LANES, LANES), F32)],
        compiler_params=pltpu.CompilerParams(
            dimension_semantics=("arbitrary", "arbitrary"), vmem_limit_bytes=VMEM_LIMIT),
        name="rwkv7_chunked",
    )(xs, zrw, ltri, wda, w0, a0, k_k, k_a, r_k, gn_w, gn_b)


def _gdn_kernel(qkv_ref, z_ref, ba_ref, ltri_ref, alog_ref, dtb_ref, onw_ref, y_ref, s_scr):
    t_idx = pl.program_id(1)

    tt = ba_ref.shape[0]

    @pl.when(t_idx == 0)
    def _():
        s_scr[...] = jnp.zeros_like(s_scr)

    n_super = tt // LANES
    per_super = LANES // GD_CHUNK
    ri, ci, same = _tri_masks(LANES, GD_CHUNK)
    strict = same & (ci < ri)
    causal = same & (ci <= ri)
    ba = ba_ref[...]
    g_all = -jnp.exp(alog_ref[...]) * _softplus(ba + dtb_ref[...])
    gc3 = _chunk_cumsum(ltri_ref[...], jnp.concatenate(_split3(g_all), axis=1))
    gc_all = gc3[:, :LANES] + gc3[:, LANES:2 * LANES] + gc3[:, 2 * LANES:]
    gc_all_t = gc_all.T

    work = []
    for h in range(GD_HEADS):
        q = qkv_ref[h].astype(F32)
        k = qkv_ref[GD_HEADS + h].astype(F32)
        v = qkv_ref[2 * GD_HEADS + h].astype(F32)
        qn = q * (lax.rsqrt(jnp.sum(q * q, axis=-1, keepdims=True) + L2_EPS) * (GD_HEAD ** -0.5))
        kn = k * lax.rsqrt(jnp.sum(k * k, axis=-1, keepdims=True) + L2_EPS)
        beta = jnp.broadcast_to(_sigmoid(ba[:, h:h + 1]), (tt, LANES))
        gc_col = jnp.broadcast_to(gc_all[:, GD_HEADS + h:GD_HEADS + h + 1], (tt, LANES))
        gc_row = gc_all_t[GD_HEADS + h:GD_HEADS + h + 1, :]
        e_col = jnp.exp(gc_col)
        kb = kn * beta
        vb = v * beta
        qd = qn * e_col
        kbe = kb * e_col
        for sc in range(n_super):
            sl = slice(sc * LANES, (sc + 1) * LANES)
            diff = gc_col[sl] - gc_row[:, sl]
            chunks = []
            for cc in range(per_super):
                rows = slice(sc * LANES + cc * GD_CHUNK, sc * LANES + (cc + 1) * GD_CHUNK)
                g_last = gc_col[rows.stop - 1:rows.stop, :]
                chunks.append(dict(kte=(kn[rows] * jnp.exp(g_last - gc_col[rows])).astype(BF16),
                                   decay=jnp.exp(g_last)))
            work.append(dict(dec=jnp.exp(jnp.where(causal, diff, -1e30)),
                             kq=jnp.concatenate([kb[sl], qn[sl]], axis=0).astype(BF16),
                             kn=kn[sl].astype(BF16),
                             rhs=jnp.concatenate([vb[sl], kbe[sl]], axis=1).astype(BF16),
                             qd=qd[sl], chunks=chunks))

    ms = [_mm_nt(w["kq"], w["kn"]) for w in work]
    lmats = [-jnp.where(strict, m[:LANES] * w["dec"], 0.0) for m, w in zip(ms, work)]
    qks = [(m[LANES:] * w["dec"]).astype(BF16) for m, w in zip(ms, work)]
    t_invs = _unit_lower_inverses(lmats, ri, ci, GD_CHUNK)
    uws = [_mm(t, w["rhs"]).astype(BF16) for t, w in zip(t_invs, work)]
    qws = [_mm(qk, uw) for qk, uw in zip(qks, uws)]
    for w, uw, qw in zip(work, uws, qws):
        w["qc"] = (w["qd"] - qw[:, LANES:]).astype(BF16)
        w["o0"] = qw[:, :LANES]
        for cc, ch in enumerate(w["chunks"]):
            tn = _mm_tn(ch["kte"], uw[cc * GD_CHUNK:(cc + 1) * GD_CHUNK])
            ch["tn_u"] = tn[:, :LANES]
            ch["tn_w"] = tn[:, LANES:].astype(BF16)

    state = [s_scr[h] for h in range(GD_HEADS)]
    for sc in range(n_super):
        for cc in range(per_super):
            for h in range(GD_HEADS):
                ch = work[h * n_super + sc]["chunks"][cc]
                s0 = state[h]
                ch["s0"] = s0
                state[h] = s0 * ch["decay"] - _mm(ch["tn_w"], s0) + ch["tn_u"]
    for h in range(GD_HEADS):
        s_scr[h] = state[h]

    for h in range(GD_HEADS):
        os = []
        for sc in range(n_super):
            w = work[h * n_super + sc]
            for cc, ch in enumerate(w["chunks"]):
                rows = slice(cc * GD_CHUNK, (cc + 1) * GD_CHUNK)
                os.append(_mm(w["qc"][rows], ch["s0"]) + w["o0"][rows])
        o = jnp.concatenate(os, axis=0)
        ms = jnp.mean(o * o, axis=-1, keepdims=True)
        on = o * lax.rsqrt(ms + NORM_EPS) * onw_ref[...]
        y_ref[h] = (on * _silu(z_ref[h].astype(F32))).astype(y_ref.dtype)


def _gdn_branch(qkv, zgd, ba, ltri, alog, dtb, onw, batch, seq_len):
    tt = REC_TT
    nt = seq_len // tt
    bt = batch * seq_len
    tok = lambda n: pl.BlockSpec((n, tt, LANES), lambda b, t: (0, b * nt + t, 0))
    par = pl.BlockSpec((1, LANES), lambda b, t: (0, 0))
    in_specs = [
        tok(3 * GD_HEADS),
        tok(GD_HEADS),
        pl.BlockSpec((tt, LANES), lambda b, t: (b * nt + t, 0)),
        pl.BlockSpec((CUM_ROWS, CUM_ROWS), lambda b, t: (0, 0)),
        par, par,
        pl.BlockSpec((1, LANES), lambda b, t: (0, 0)),
    ]
    return pl.pallas_call(
        _gdn_kernel,
        grid=(batch, nt),
        in_specs=in_specs,
        out_specs=tok(GD_HEADS),
        out_shape=jax.ShapeDtypeStruct((GD_HEADS, bt, LANES), BF16),
        scratch_shapes=[pltpu.VMEM((GD_HEADS, LANES, LANES), F32)],
        compiler_params=pltpu.CompilerParams(
            dimension_semantics=("arbitrary", "arbitrary"), vmem_limit_bytes=VMEM_LIMIT),
        name="gated_delta_chunked",
    )(qkv, zgd, ba, ltri, alog, dtb, onw)


def _merge_kernel(x_ref, ya_ref, yb_ref, gin_ref, wg_ref, wa_ref, wb_ref, wo_ref, g_ref, o_ref):
    xc = x_ref[...]
    inv = lax.rsqrt(jnp.mean(xc * xc, axis=-1, keepdims=True) + NORM_EPS)
    xb = (xc * gin_ref[...]).astype(BF16)
    ya = jnp.concatenate([ya_ref[j] for j in range(N_PAIRS)], axis=1)
    yb = jnp.concatenate([yb_ref[j] for j in range(GD_HEADS)], axis=1)
    blocks = []
    for c0 in range(0, D_MODEL, MERGE_COLS):
        cols = slice(c0, c0 + MERGE_COLS)
        gate_a = _sigmoid(jnp.dot(xb, wg_ref[:, cols], preferred_element_type=F32) * inv)
        gate_b = _sigmoid(jnp.dot(xb, wg_ref[:, D_MODEL + c0:D_MODEL + c0 + MERGE_COLS],
                                  preferred_element_type=F32) * inv)
        ma = jnp.dot(ya, wa_ref[:, cols], preferred_element_type=F32)
        mb = jnp.dot(yb, wb_ref[:, cols], preferred_element_type=F32)
        blocks.append((gate_a * ma + gate_b * mb).astype(BF16))
    merged = jnp.concatenate(blocks, axis=1)
    res = xc + jnp.dot(merged, wo_ref[...], preferred_element_type=F32)
    ms = jnp.mean(res * res, axis=-1, keepdims=True)
    o_ref[...] = res * lax.rsqrt(ms + NORM_EPS) * g_ref[...]


def _merge(x2, ya, yb, norm_in, w_gates, w_a, w_b, w_o, norm_w):
    bt = x2.shape[0]
    tm = MERGE_TM
    const = lambda i: (0, 0)
    in_specs = [
        pl.BlockSpec((tm, D_MODEL), lambda i: (i, 0)),
        pl.BlockSpec((N_PAIRS, tm, LANES), lambda i: (0, i, 0)),
        pl.BlockSpec((GD_HEADS, tm, LANES), lambda i: (0, i, 0)),
        pl.BlockSpec((1, D_MODEL), const),
        pl.BlockSpec((D_MODEL, 2 * D_MODEL), const),
        pl.BlockSpec((RW_WIDTH, D_MODEL), const),
        pl.BlockSpec((GD_WIDTH, D_MODEL), const),
        pl.BlockSpec((D_MODEL, D_MODEL), const),
        pl.BlockSpec((1, D_MODEL), const),
    ]
    return pl.pallas_call(
        _merge_kernel,
        grid=(bt // tm,),
        in_specs=in_specs,
        out_specs=pl.BlockSpec((tm, D_MODEL), lambda i: (i, 0)),
        out_shape=jax.ShapeDtypeStruct((bt, D_MODEL), F32),
        compiler_params=pltpu.CompilerParams(
            dimension_semantics=("arbitrary",), vmem_limit_bytes=VMEM_LIMIT),
        name="merge_out_proj",
    )(x2, ya, yb, norm_in, w_gates, w_a, w_b, w_o, norm_w)


def _block_lower_tri(n, chunk):
    i = jnp.arange(n)
    return (((i[:, None] // chunk) == (i[None, :] // chunk)) & (i[None, :] <= i[:, None])).astype(BF16)


def _layer(x2, batch, seq_len, norm_in_w, w_in, rw_mu, rw_w0, rw_w2, rw_a0, rw_a2, rw_k_k,
           rw_k_a, rw_r_k, rw_gn_w, rw_gn_b, gd_conv_w, gd_A_log, gd_dt_bias, gd_o_norm_w,
           w_branch_a, w_branch_b, w_out):
    w_bf = w_in.astype(BF16)
    w_gates = w_bf[:, C_GATES + 2 * GD_HEADS:]

    mu = rw_mu.reshape(RW_SHIFT_COLS // LANES, 1, LANES)
    conv_w = jnp.swapaxes(gd_conv_w.reshape(GD_CONV, 3 * GD_HEADS, LANES), 0, 1)
    xs, zrw, qkv, zgd, ba = _projection(
        x2, norm_in_w.reshape(1, D_MODEL), w_bf, mu, conv_w, seq_len)

    ltri = _block_lower_tri(CUM_ROWS, RW_CHUNK)
    pair = lambda p: p.reshape(N_PAIRS, 1, LANES)
    zero = jnp.zeros((RW_RANK, RW_WIDTH), F32)
    wda = jnp.concatenate([jnp.concatenate([rw_w2, zero], axis=1),
                           jnp.concatenate([zero, rw_a2], axis=1)], axis=0).astype(BF16)
    ya = _rwkv_branch(xs, zrw, ltri, wda, pair(rw_w0), pair(rw_a0), pair(rw_k_k),
                      pair(rw_k_a), pair(rw_r_k.reshape(RW_WIDTH)), pair(rw_gn_w),
                      pair(rw_gn_b), batch, seq_len)

    head = lambda p: jnp.pad(p.astype(F32), (GD_HEADS, LANES - 2 * GD_HEADS)).reshape(1, LANES)
    yb = _gdn_branch(qkv, zgd, ba, _block_lower_tri(CUM_ROWS, GD_CHUNK),
                     head(gd_A_log), head(gd_dt_bias),
                     gd_o_norm_w.reshape(1, GD_HEAD), batch, seq_len)
    return w_gates, ya, yb


def kernel(x, norm_in_w, w_in, rw_mu, rw_w0, rw_w2, rw_a0, rw_a2, rw_k_k, rw_k_a, rw_r_k,
           rw_gn_w, rw_gn_b, gd_conv_w, gd_A_log, gd_dt_bias, gd_o_norm_w,
           w_branch_a, w_branch_b, w_out, norm_out_w):
    batch, seq_len, d = x.shape
    assert d == D_MODEL and norm_in_w.shape[0] == 1
    assert seq_len % max(PROJ_TM, REC_TT) == 0 and (batch * seq_len) % MERGE_TM == 0
    x2 = x.reshape(batch * seq_len, d)
    w_gates, ya, yb = _layer(
        x2, batch, seq_len, norm_in_w[0], w_in[0], rw_mu[0], rw_w0[0], rw_w2[0], rw_a0[0],
        rw_a2[0], rw_k_k[0], rw_k_a[0], rw_r_k[0], rw_gn_w[0], rw_gn_b[0], gd_conv_w[0],
        gd_A_log[0], gd_dt_bias[0], gd_o_norm_w[0], w_branch_a[0], w_branch_b[0], w_out[0])
    out = _merge(x2, ya, yb, norm_in_w.reshape(1, D_MODEL), w_gates,
                 w_branch_a[0].astype(BF16), w_branch_b[0].astype(BF16), w_out[0].astype(BF16),
                 norm_out_w.reshape(1, D_MODEL))
    return out.reshape(batch, seq_len, d)
```

```python
import functools

import jax
import jax.numpy as jnp
from jax import lax
from jax.experimental import pallas as pl
from jax.experimental.pallas import tpu as pltpu

F32 = jnp.float32
BF16 = jnp.bfloat16

D_MODEL = 1024
RW_WIDTH = 512
RW_HEAD = 64
RW_RANK = 64
RW_GN_EPS = RW_HEAD * 1e-5
DECAY_SCALE = -0.6065306597126334
GD_WIDTH = 512
GD_HEAD = 128
GD_HEADS = GD_WIDTH // GD_HEAD
GD_CONV = 4
NORM_EPS = 1e-6
L2_EPS = 1e-12

LANES = 128
RW_CHUNK = 64
GD_CHUNK = 128
N_PAIRS = RW_WIDTH // LANES
RW_SHIFT_COLS = 3 * RW_WIDTH + 2 * RW_RANK
C_ZRW = RW_SHIFT_COLS
C_QKV = C_ZRW + RW_WIDTH
C_ZGD = C_QKV + 3 * GD_WIDTH
C_GATES = C_ZGD + GD_WIDTH

PROJ_TM = 512
CARRY_ROWS = 8
REC_TT = 512
CUM_ROWS = 256
MERGE_TM = 512
MERGE_COLS = 512
VMEM_LIMIT = 56 * 1024 * 1024


def _mm(a, b):
    return jnp.dot(a.astype(BF16), b.astype(BF16), preferred_element_type=F32)


def _mm_nt(a, b):
    return lax.dot_general(a.astype(BF16), b.astype(BF16), (((1,), (1,)), ((), ())),
                           preferred_element_type=F32)


def _mm_tn(a, b):
    return lax.dot_general(a.astype(BF16), b.astype(BF16), (((0,), (0,)), ((), ())),
                           preferred_element_type=F32)


def _sigmoid(x):
    return 0.5 * jnp.tanh(0.5 * x) + 0.5


def _silu(x):
    h = 0.5 * x
    return h * jnp.tanh(h) + h


def _softplus(x):
    return jnp.maximum(x, 0.0) + jnp.log(1.0 + jnp.exp(-jnp.abs(x)))


def _split3(x):
    hi = x.astype(BF16)
    r1 = x - hi.astype(F32)
    mid = r1.astype(BF16)
    lo = (r1 - mid.astype(F32)).astype(BF16)
    return hi, mid, lo


def _blk(idx, size):
    return lax.shift_right_logical(idx, size.bit_length() - 1)


def _tri_masks(n, chunk):
    ri = lax.broadcasted_iota(jnp.int32, (n, n), 0)
    ci = lax.broadcasted_iota(jnp.int32, (n, n), 1)
    same = _blk(ri, chunk) == _blk(ci, chunk)
    return ri, ci, same


def _unit_lower_inverses(lmats, ri, ci, chunk):
    eye = (ri == ci).astype(F32)
    blk8 = _blk(ri, 8) == _blk(ci, 8)
    n = lmats[0].shape[0]
    ds = [jnp.where(blk8, l, 0.0) for l in lmats]
    dbs = [d.astype(BF16) for d in ds]
    d2s = [_mm(d, d).astype(BF16) for d in dbs]
    ts = [eye + d for d in ds]
    xs = [_mm(d2, jnp.concatenate([t.astype(BF16), d2], axis=1)) for t, d2 in zip(ts, d2s)]
    ts = [t + x[:, :n] for t, x in zip(ts, xs)]
    ts = [t + _mm(x[:, n:], t) for t, x in zip(ts, xs)]
    s = 8
    while s < chunk:
        groups = range(n // (2 * s))
        lower = lambda x: jnp.concatenate(
            [x[g * 2 * s + s:(g + 1) * 2 * s] for g in groups], axis=0)
        zeros = jnp.zeros((s, n), F32)
        rc = lax.broadcasted_iota(jnp.int32, (n // 2, n), 0)
        cl = lax.broadcasted_iota(jnp.int32, (n // 2, n), 1)
        rl = _blk(rc, s) * (2 * s) + s + (rc & (s - 1))
        off = (_blk(rl, 2 * s) == _blk(cl, 2 * s)) & (_blk(rl, s) != _blk(cl, s))
        tbs = [t.astype(BF16) for t in ts]
        xs = [_mm(jnp.where(off, lower(l), 0.0), t) for l, t in zip(lmats, tbs)]
        xs = [jnp.concatenate([p for g in groups for p in (zeros, x[g * s:(g + 1) * s])], axis=0)
              for x in xs]
        ys = [_mm(lower(t), x) for t, x in zip(ts, xs)]
        ts = [jnp.concatenate([p for g in groups
                               for p in (t[g * 2 * s:g * 2 * s + s],
                                         t[g * 2 * s + s:(g + 1) * 2 * s] + y[g * s:(g + 1) * s])],
                              axis=0)
              for t, y in zip(ts, ys)]
        s *= 2
    return [t.astype(BF16) for t in ts]


def _chunk_cumsum(ltri, x):
    return jnp.concatenate([_mm(ltri, x[r0:r0 + CUM_ROWS]) for r0 in range(0, x.shape[0], CUM_ROWS)],
                           axis=0)


def _proj_kernel(x_ref, g_ref, w_ref, mu_ref, prw_ref, zrw_ref, pqkv_ref, zgd_ref, ba_ref,
                 carry_scr, *, tiles_per_seq):
    tm = x_ref.shape[0]

    @pl.when(pl.program_id(0) % tiles_per_seq == 0)
    def _():
        carry_scr[...] = jnp.zeros_like(carry_scr)

    xc = x_ref[...]
    inv = lax.rsqrt(jnp.mean(xc * xc, axis=-1, keepdims=True) + NORM_EPS)
    xb = (xc * g_ref[...]).astype(BF16)

    def proj(w):
        return jnp.dot(xb, w, preferred_element_type=F32) * inv

    def token_shift(g, cur):
        prev = pltpu.roll(jnp.concatenate([carry_scr[g], cur], axis=0), 1, 0)[CARRY_ROWS:]
        carry_scr[g] = cur[tm - CARRY_ROWS:]
        return cur + (prev - cur) * mu_ref[g]

    def lane_groups(c0, n_groups, out_ref, post=None):
        for g0 in range(0, n_groups, 4):
            ng = min(4, n_groups - g0)
            pe = proj(w_ref[:, c0 + g0 * LANES:c0 + (g0 + ng) * LANES])
            for j in range(ng):
                slab = pe[:, j * LANES:(j + 1) * LANES]
                out_ref[g0 + j] = (slab if post is None else post(g0 + j, slab)).astype(BF16)

    lane_groups(0, RW_SHIFT_COLS // LANES, prw_ref, token_shift)
    lane_groups(C_ZRW, N_PAIRS, zrw_ref)
    lane_groups(C_QKV, 3 * GD_HEADS, pqkv_ref)
    lane_groups(C_ZGD, GD_HEADS, zgd_ref)
    ba_ref[...] = proj(w_ref[:, C_GATES:C_GATES + LANES])


def _projection(x2, norm_w, w_bf, mu, seq_len):
    bt = x2.shape[0]
    tm = PROJ_TM
    const = lambda i: (0, 0)
    slabs = lambda n: pl.BlockSpec((n, tm, LANES), lambda i: (0, i, 0))
    n_rw = RW_SHIFT_COLS // LANES
    out_shapes = (
        jax.ShapeDtypeStruct((n_rw, bt, LANES), BF16),
        jax.ShapeDtypeStruct((N_PAIRS, bt, LANES), BF16),
        jax.ShapeDtypeStruct((3 * GD_HEADS, bt, LANES), BF16),
        jax.ShapeDtypeStruct((GD_HEADS, bt, LANES), BF16),
        jax.ShapeDtypeStruct((bt, LANES), F32),
    )
    out_specs = (
        slabs(n_rw), slabs(N_PAIRS), slabs(3 * GD_HEADS), slabs(GD_HEADS),
        pl.BlockSpec((tm, LANES), lambda i: (i, 0)),
    )
    in_specs = [
        pl.BlockSpec((tm, D_MODEL), lambda i: (i, 0)),
        pl.BlockSpec((1, D_MODEL), const),
        pl.BlockSpec((D_MODEL, C_GATES + LANES), const, pipeline_mode=pl.Buffered(1)),
        pl.BlockSpec((n_rw, 1, LANES), lambda i: (0, 0, 0)),
    ]
    return pl.pallas_call(
        functools.partial(_proj_kernel, tiles_per_seq=seq_len // tm),
        grid=(bt // tm,),
        in_specs=in_specs,
        out_specs=out_specs,
        out_shape=out_shapes,
        scratch_shapes=[pltpu.VMEM((n_rw, CARRY_ROWS, LANES), F32)],
        compiler_params=pltpu.CompilerParams(
            dimension_semantics=("arbitrary",), vmem_limit_bytes=VMEM_LIMIT),
        name="rmsnorm_in_proj",
    )(x2, norm_w, w_bf, mu)


def _rwkv_kernel(xs_ref, z_ref, ltri_ref, wda_ref, w0_ref, a0_ref,
                 kk_ref, ka_ref, rk_ref, gnw_ref, gnb_ref, y_ref, s_scr):
    t_idx = pl.program_id(1)

    tt = z_ref.shape[1]
    n_chunks = tt // RW_CHUNK

    @pl.when(t_idx == 0)
    def _():
        s_scr[...] = jnp.zeros_like(s_scr)

    lane = lax.broadcasted_iota(jnp.int32, (1, LANES), 1)
    head_a = lane < RW_HEAD
    ri, ci, same = _tri_masks(LANES, RW_CHUNK)
    strict = same & (ci < ri)
    causal = same & (ci <= ri)

    def head_sum(x):
        sa = jnp.sum(jnp.where(head_a, x, 0.0), axis=-1, keepdims=True)
        sb = jnp.sum(jnp.where(head_a, 0.0, x), axis=-1, keepdims=True)
        return jnp.where(head_a, jnp.broadcast_to(sa, x.shape), jnp.broadcast_to(sb, x.shape))
    ltri = ltri_ref[...]

    def pair_rows(x):
        return jnp.concatenate([jnp.where(head_a, x, 0.0), jnp.where(head_a, 0.0, x)], axis=0)

    wa = xs_ref[3 * N_PAIRS].astype(F32)
    da = jnp.where(lane < RW_RANK, jnp.tanh(wa), wa)
    pre = _mm(da, wda_ref[...])

    bonus, p_end, work = [], [], []
    for j in range(N_PAIRS):
        r = xs_ref[j].astype(F32)
        k = xs_ref[N_PAIRS + j].astype(F32)
        v = xs_ref[2 * N_PAIRS + j].astype(F32)
        cols = slice(j * LANES, (j + 1) * LANES)
        lw = DECAY_SCALE * _sigmoid(w0_ref[j] + pre[:, cols])
        lr = _sigmoid(a0_ref[j] + pre[:, RW_WIDTH + j * LANES:RW_WIDTH + (j + 1) * LANES])
        kkr = k * kk_ref[j]
        kk = kkr * lax.rsqrt(head_sum(kkr * kkr) + L2_EPS)
        k2 = k * (1.0 + (lr - 1.0) * ka_ref[j])
        b = kk * lr
        bonus.append(head_sum(r * k2 * rk_ref[j]) * v)
        lw_hi = lw.astype(BF16)
        lw_lo = (lw - lw_hi.astype(F32)).astype(BF16)
        cl2 = _chunk_cumsum(ltri, jnp.concatenate([lw_hi, lw_lo], axis=1))
        clw = cl2[:, :LANES] + cl2[:, LANES:]
        e_neg = jnp.exp(-clw)
        rt = r * jnp.exp(clw)
        at = -kk * jnp.exp(clw - lw)
        bt = b * e_neg
        kt = k2 * e_neg
        for c in range(n_chunks):
            sl = slice(c * RW_CHUNK, (c + 1) * RW_CHUNK)
            p_c = jnp.exp(clw[(c + 1) * RW_CHUNK - 1:(c + 1) * RW_CHUNK, :])
            p_end.append(jnp.sum(jnp.where(ri == ci, p_c, 0.0), axis=-1, keepdims=True))
            work.append(dict(
                ar=jnp.concatenate([pair_rows(at[sl]), pair_rows(rt[sl])], axis=0).astype(BF16),
                bk=jnp.concatenate([bt[sl], bt[sl], kt[sl], kt[sl]], axis=0).astype(BF16),
                v=pair_rows(v[sl]).astype(BF16),
                bk_end=jnp.concatenate([pair_rows(bt[sl] * p_c), pair_rows(kt[sl] * p_c)],
                                       axis=0).astype(BF16)))

    n = len(work)
    causal2 = jnp.concatenate([causal, causal], axis=1)
    zeros = jnp.zeros((LANES, LANES), BF16)
    ms = [_mm_nt(w["ar"], w["bk"]) for w in work]
    lbs = [jnp.where(strict, m[:LANES, :LANES], 0.0) for m in ms]
    aks = [jnp.where(strict, m[:LANES, LANES:], 0.0).astype(BF16) for m in ms]
    rbks = [jnp.where(causal2, m[LANES:], 0.0).astype(BF16) for m in ms]
    t_invs = _unit_lower_inverses(lbs, ri, ci, RW_CHUNK)
    akvs = [_mm(ak, w["v"]).astype(BF16) for ak, w in zip(aks, work)]
    tws = [_mm(t, jnp.concatenate([w["ar"][:LANES], akv], axis=1)).astype(BF16)
           for t, w, akv in zip(t_invs, work, akvs)]
    xs = [jnp.concatenate([tw, jnp.concatenate([zeros, w["v"]], axis=1)], axis=0)
          for tw, w in zip(tws, work)]
    qys = [_mm(rbk, x) for rbk, x in zip(rbks, xs)]
    zs = [_mm_tn(w["bk_end"], x) for w, x in zip(work, xs)]
    zas = [z[:, :LANES].astype(BF16) for z in zs]

    states = [None] * n
    for c in range(n_chunks):
        for j in range(N_PAIRS):
            i = j * n_chunks + c
            if c == 0:
                states[i] = s_scr[j]
            h0 = states[i]
            h1 = p_end[i] * h0 + _mm(zas[i], h0) + zs[i][:, LANES:]
            if c == n_chunks - 1:
                s_scr[j] = h1
            else:
                states[i + 1] = h1

    for j in range(N_PAIRS):
        ys = []
        for c in range(n_chunks):
            i = j * n_chunks + c
            qc = work[i]["ar"][LANES:] + qys[i][:, :LANES]
            y_p = _mm(qc, states[i]) + qys[i][:, LANES:]
            ys.append(y_p[:RW_CHUNK] + y_p[RW_CHUNK:])
        y = jnp.concatenate(ys, axis=0)
        inv_n = 1.0 / RW_HEAD
        yc = y - head_sum(y) * inv_n
        var = head_sum(yc * yc) * inv_n
        yn = yc * lax.rsqrt(var + RW_GN_EPS) * gnw_ref[j] + gnb_ref[j]
        y_ref[j] = ((yn + bonus[j]) * _silu(z_ref[j].astype(F32))).astype(y_ref.dtype)


def _rwkv_branch(xs, zrw, ltri, wda, w0, a0, k_k, k_a, r_k, gn_w, gn_b, batch, seq_len):
    tt = REC_TT
    nt = seq_len // tt
    bt = batch * seq_len
    tok = lambda n: pl.BlockSpec((n, tt, LANES), lambda b, t: (0, b * nt + t, 0))
    par = pl.BlockSpec((N_PAIRS, 1, LANES), lambda b, t: (0, 0, 0))
    n_rw = RW_SHIFT_COLS // LANES
    in_specs = [
        tok(n_rw),
        tok(N_PAIRS),
        pl.BlockSpec((CUM_ROWS, CUM_ROWS), lambda b, t: (0, 0)),
        pl.BlockSpec((LANES, 2 * RW_WIDTH), lambda b, t: (0, 0)),
        par, par, par, par, par, par, par,
    ]
    return pl.pallas_call(
        _rwkv_kernel,
        grid=(batch, nt),
        in_specs=in_specs,
        out_specs=tok(N_PAIRS),
        out_shape=jax.ShapeDtypeStruct((N_PAIRS, bt, LANES), BF16),
        scratch_shapes=[pltpu.VMEM((N_PAIRS, LANES, LANES), F32)],
        compiler_params=pltpu.CompilerParams(
            dimension_semantics=("arbitrary", "arbitrary"), vmem_limit_bytes=VMEM_LIMIT),
        name="rwkv7_chunked",
    )(xs, zrw, ltri, wda, w0, a0, k_k, k_a, r_k, gn_w, gn_b)


def _gdn_kernel(pqkv_ref, z_ref, ba_ref, ltri_ref, cw_ref, alog_ref, dtb_ref, onw_ref,
                y_ref, s_scr, carry_scr):
    t_idx = pl.program_id(1)

    tt = ba_ref.shape[0]

    @pl.when(t_idx == 0)
    def _():
        s_scr[...] = jnp.zeros_like(s_scr)
        carry_scr[:, :CARRY_ROWS, :] = jnp.zeros((3 * GD_HEADS, CARRY_ROWS, LANES), F32)

    for g in range(3 * GD_HEADS):
        carry_scr[g, CARRY_ROWS:, :] = pqkv_ref[g].astype(F32)

    def conv_silu(g):
        cur = carry_scr[g, CARRY_ROWS:, :]
        taps = cw_ref[g]
        acc = cur * taps[GD_CONV - 1:GD_CONV]
        for s in range(1, GD_CONV):
            acc = acc + carry_scr[g, pl.ds(CARRY_ROWS - s, tt), :] * taps[GD_CONV - 1 - s:GD_CONV - s]
        carry_scr[g, :CARRY_ROWS, :] = cur[tt - CARRY_ROWS:]
        return _silu(acc)

    n_super = tt // LANES
    per_super = LANES // GD_CHUNK
    ri, ci, same = _tri_masks(LANES, GD_CHUNK)
    strict = same & (ci < ri)
    causal = same & (ci <= ri)
    ba = ba_ref[...]
    g_all = -jnp.exp(alog_ref[...]) * _softplus(ba + dtb_ref[...])
    gc3 = _chunk_cumsum(ltri_ref[...], jnp.concatenate(_split3(g_all), axis=1))
    gc_all = gc3[:, :LANES] + gc3[:, LANES:2 * LANES] + gc3[:, 2 * LANES:]
    gc_all_t = gc_all.T

    work = []
    for h in range(GD_HEADS):
        q = conv_silu(h)
        k = conv_silu(GD_HEADS + h)
        v = conv_silu(2 * GD_HEADS + h)
        qn = q * (lax.rsqrt(jnp.sum(q * q, axis=-1, keepdims=True) + L2_EPS) * (GD_HEAD ** -0.5))
        kn = k * lax.rsqrt(jnp.sum(k * k, axis=-1, keepdims=True) + L2_EPS)
        beta = jnp.broadcast_to(_sigmoid(ba[:, h:h + 1]), (tt, LANES))
        gc_col = jnp.broadcast_to(gc_all[:, GD_HEADS + h:GD_HEADS + h + 1], (tt, LANES))
        gc_row = gc_all_t[GD_HEADS + h:GD_HEADS + h + 1, :]
        e_col = jnp.exp(gc_col)
        kb = kn * beta
        vb = v * beta
        qd = qn * e_col
        kbe = kb * e_col
        for sc in range(n_super):
            sl = slice(sc * LANES, (sc + 1) * LANES)
            diff = gc_col[sl] - gc_row[:, sl]
            chunks = []
            for cc in range(per_super):
                rows = slice(sc * LANES + cc * GD_CHUNK, sc * LANES + (cc + 1) * GD_CHUNK)
                g_last = gc_col[rows.stop - 1:rows.stop, :]
                chunks.append(dict(kte=(kn[rows] * jnp.exp(g_last - gc_col[rows])).astype(BF16),
                                   decay=jnp.exp(g_last)))
            work.append(dict(dec=jnp.exp(jnp.where(causal, diff, -1e30)),
                             kq=jnp.concatenate([kb[sl], qn[sl]], axis=0).astype(BF16),
                             kn=kn[sl].astype(BF16),
                             rhs=jnp.concatenate([vb[sl], kbe[sl]], axis=1).astype(BF16),
                             qd=qd[sl], chunks=chunks))

    ms = [_mm_nt(w["kq"], w["kn"]) for w in work]
    lmats = [-jnp.where(strict, m[:LANES] * w["dec"], 0.0) for m, w in zip(ms, work)]
    qks = [(m[LANES:] * w["dec"]).astype(BF16) for m, w in zip(ms, work)]
    t_invs = _unit_lower_inverses(lmats, ri, ci, GD_CHUNK)
    uws = [_mm(t, w["rhs"]).astype(BF16) for t, w in zip(t_invs, work)]
    qws = [_mm(qk, uw) for qk, uw in zip(qks, uws)]
    for w, uw, qw in zip(work, uws, qws):
        w["qc"] = (w["qd"] - qw[:, LANES:]).astype(BF16)
        w["o0"] = qw[:, :LANES]
        for cc, ch in enumerate(w["chunks"]):
            tn = _mm_tn(ch["kte"], uw[cc * GD_CHUNK:(cc + 1) * GD_CHUNK])
            ch["tn_u"] = tn[:, :LANES]
            ch["tn_w"] = tn[:, LANES:].astype(BF16)

    state = [s_scr[h] for h in range(GD_HEADS)]
    for sc in range(n_super):
        for cc in range(per_super):
            for h in range(GD_HEADS):
                ch = work[h * n_super + sc]["chunks"][cc]
                s0 = state[h]
                ch["s0"] = s0
                state[h] = s0 * ch["decay"] - _mm(ch["tn_w"], s0) + ch["tn_u"]
    for h in range(GD_HEADS):
        s_scr[h] = state[h]

    for h in range(GD_HEADS):
        os = []
        for sc in range(n_super):
            w = work[h * n_super + sc]
            for cc, ch in enumerate(w["chunks"]):
                rows = slice(cc * GD_CHUNK, (cc + 1) * GD_CHUNK)
                os.append(_mm(w["qc"][rows], ch["s0"]) + w["o0"][rows])
        o = jnp.concatenate(os, axis=0)
        ms = jnp.mean(o * o, axis=-1, keepdims=True)
        on = o * lax.rsqrt(ms + NORM_EPS) * onw_ref[...]
        y_ref[h] = (on * _silu(z_ref[h].astype(F32))).astype(y_ref.dtype)


def _gdn_branch(pqkv, zgd, ba, ltri, conv_w, alog, dtb, onw, batch, seq_len):
    tt = REC_TT
    nt = seq_len // tt
    bt = batch * seq_len
    tok = lambda n: pl.BlockSpec((n, tt, LANES), lambda b, t: (0, b * nt + t, 0))
    par = pl.BlockSpec((1, LANES), lambda b, t: (0, 0))
    in_specs = [
        tok(3 * GD_HEADS),
        tok(GD_HEADS),
        pl.BlockSpec((tt, LANES), lambda b, t: (b * nt + t, 0)),
        pl.BlockSpec((CUM_ROWS, CUM_ROWS), lambda b, t: (0, 0)),
        pl.BlockSpec((3 * GD_HEADS, GD_CONV, LANES), lambda b, t: (0, 0, 0)),
        par, par,
        pl.BlockSpec((1, LANES), lambda b, t: (0, 0)),
    ]
    return pl.pallas_call(
        _gdn_kernel,
        grid=(batch, nt),
        in_specs=in_specs,
        out_specs=tok(GD_HEADS),
        out_shape=jax.ShapeDtypeStruct((GD_HEADS, bt, LANES), BF16),
        scratch_shapes=[pltpu.VMEM((GD_HEADS, LANES, LANES), F32),
                        pltpu.VMEM((3 * GD_HEADS, CARRY_ROWS + tt, LANES), F32)],
        compiler_params=pltpu.CompilerParams(
            dimension_semantics=("arbitrary", "arbitrary"), vmem_limit_bytes=VMEM_LIMIT),
        name="gated_delta_chunked",
    )(pqkv, zgd, ba, ltri, conv_w, alog, dtb, onw)


def _merge_kernel(x_ref, ya_ref, yb_ref, gin_ref, wg_ref, wa_ref, wb_ref, wo_ref, g_ref, o_ref):
    xc = x_ref[...]
    inv = lax.rsqrt(jnp.mean(xc * xc, axis=-1, keepdims=True) + NORM_EPS)
    xb = (xc * gin_ref[...]).astype(BF16)
    ya = jnp.concatenate([ya_ref[j] for j in range(N_PAIRS)], axis=1)
    yb = jnp.concatenate([yb_ref[j] for j in range(GD_HEADS)], axis=1)
    blocks = []
    for c0 in range(0, D_MODEL, MERGE_COLS):
        cols = slice(c0, c0 + MERGE_COLS)
        gate_a = _sigmoid(jnp.dot(xb, wg_ref[:, cols], preferred_element_type=F32) * inv)
        gate_b = _sigmoid(jnp.dot(xb, wg_ref[:, D_MODEL + c0:D_MODEL + c0 + MERGE_COLS],
                                  preferred_element_type=F32) * inv)
        ma = jnp.dot(ya, wa_ref[:, cols], preferred_element_type=F32)
        mb = jnp.dot(yb, wb_ref[:, cols], preferred_element_type=F32)
        blocks.append((gate_a * ma + gate_b * mb).astype(BF16))
    merged = jnp.concatenate(blocks, axis=1)
    res = xc + jnp.dot(merged, wo_ref[...], preferred_element_type=F32)
    ms = jnp.mean(res * res, axis=-1, keepdims=True)
    o_ref[...] = res * lax.rsqrt(ms + NORM_EPS) * g_ref[...]


def _merge(x2, ya, yb, norm_in, w_gates, w_a, w_b, w_o, norm_w):
    bt = x2.shape[0]
    tm = MERGE_TM
    const = lambda i: (0, 0)
    in_specs = [
        pl.BlockSpec((tm, D_MODEL), lambda i: (i, 0)),
        pl.BlockSpec((N_PAIRS, tm, LANES), lambda i: (0, i, 0)),
        pl.BlockSpec((GD_HEADS, tm, LANES), lambda i: (0, i, 0)),
        pl.BlockSpec((1, D_MODEL), const),
        pl.BlockSpec((D_MODEL, 2 * D_MODEL), const),
        pl.BlockSpec((RW_WIDTH, D_MODEL), const),
        pl.BlockSpec((GD_WIDTH, D_MODEL), const),
        pl.BlockSpec((D_MODEL, D_MODEL), const),
        pl.BlockSpec((1, D_MODEL), const),
    ]
    return pl.pallas_call(
        _merge_kernel,
        grid=(bt // tm,),
        in_specs=in_specs,
        out_specs=pl.BlockSpec((tm, D_MODEL), lambda i: (i, 0)),
        out_shape=jax.ShapeDtypeStruct((bt, D_MODEL), F32),
        compiler_params=pltpu.CompilerParams(
            dimension_semantics=("arbitrary",), vmem_limit_bytes=VMEM_LIMIT),
        name="merge_out_proj",
    )(x2, ya, yb, norm_in, w_gates, w_a, w_b, w_o, norm_w)


def _block_lower_tri(n, chunk):
    i = jnp.arange(n)
    return (((i[:, None] // chunk) == (i[None, :] // chunk)) & (i[None, :] <= i[:, None])).astype(BF16)


def _layer(x2, batch, seq_len, norm_in_w, w_in, rw_mu, rw_w0, rw_w2, rw_a0, rw_a2, rw_k_k,
           rw_k_a, rw_r_k, rw_gn_w, rw_gn_b, gd_conv_w, gd_A_log, gd_dt_bias, gd_o_norm_w,
           w_branch_a, w_branch_b, w_out):
    w_bf = w_in.astype(BF16)
    w_gates = w_bf[:, C_GATES + 2 * GD_HEADS:]

    mu = rw_mu.reshape(RW_SHIFT_COLS // LANES, 1, LANES)
    xs, zrw, pqkv, zgd, ba = _projection(x2, norm_in_w.reshape(1, D_MODEL), w_bf, mu, seq_len)

    ltri = _block_lower_tri(CUM_ROWS, RW_CHUNK)
    pair = lambda p: p.reshape(N_PAIRS, 1, LANES)
    zero = jnp.zeros((RW_RANK, RW_WIDTH), F32)
    wda = jnp.concatenate([jnp.concatenate([rw_w2, zero], axis=1),
                           jnp.concatenate([zero, rw_a2], axis=1)], axis=0).astype(BF16)
    ya = _rwkv_branch(xs, zrw, ltri, wda, pair(rw_w0), pair(rw_a0), pair(rw_k_k),
                      pair(rw_k_a), pair(rw_r_k.reshape(RW_WIDTH)), pair(rw_gn_w),
                      pair(rw_gn_b), batch, seq_len)

    head = lambda p: jnp.pad(p.astype(F32), (GD_HEADS, LANES - 2 * GD_HEADS)).reshape(1, LANES)
    conv_w = jnp.swapaxes(gd_conv_w.reshape(GD_CONV, 3 * GD_HEADS, LANES), 0, 1)
    yb = _gdn_branch(pqkv, zgd, ba, _block_lower_tri(CUM_ROWS, GD_CHUNK), conv_w,
                     head(gd_A_log), head(gd_dt_bias),
                     gd_o_norm_w.reshape(1, GD_HEAD), batch, seq_len)
    return w_gates, ya, yb


def kernel(x, norm_in_w, w_in, rw_mu, rw_w0, rw_w2, rw_a0, rw_a2, rw_k_k, rw_k_a, rw_r_k,
           rw_gn_w, rw_gn_b, gd_conv_w, gd_A_log, gd_dt_bias, gd_o_norm_w,
           w_branch_a, w_branch_b, w_out, norm_out_w):
    batch, seq_len, d = x.shape
    assert d == D_MODEL and norm_in_w.shape[0] == 1
    assert seq_len % max(PROJ_TM, REC_TT) == 0 and (batch * seq_len) % MERGE_TM == 0
    x2 = x.reshape(batch * seq_len, d)
    w_gates, ya, yb = _layer(
        x2, batch, seq_len, norm_in_w[0], w_in[0], rw_mu[0], rw_w0[0], rw_w2[0], rw_a0[0],
        rw_a2[0], rw_k_k[0], rw_k_a[0], rw_r_k[0], rw_gn_w[0], rw_gn_b[0], gd_conv_w[0],
        gd_A_log[0], gd_dt_bias[0], gd_o_norm_w[0], w_branch_a[0], w_branch_b[0], w_out[0])
    out = _merge(x2, ya, yb, norm_in_w.reshape(1, D_MODEL), w_gates,
                 w_branch_a[0].astype(BF16), w_branch_b[0].astype(BF16), w_out[0].astype(BF16),
                 norm_out_w.reshape(1, D_MODEL))
    return out.reshape(batch, seq_len, d)
```

```python
import functools

import jax
import jax.numpy as jnp
from jax import lax
from jax.experimental import pallas as pl
from jax.experimental.pallas import tpu as pltpu

F32 = jnp.float32
BF16 = jnp.bfloat16

D_MODEL = 1024
RW_WIDTH = 512
RW_HEAD = 64
RW_RANK = 64
RW_GN_EPS = RW_HEAD * 1e-5
DECAY_SCALE = -0.6065306597126334
GD_WIDTH = 512
GD_HEAD = 128
GD_HEADS = GD_WIDTH // GD_HEAD
GD_CONV = 4
NORM_EPS = 1e-6
L2_EPS = 1e-12

LANES = 128
RW_CHUNK = 64
GD_CHUNK = 128
N_PAIRS = RW_WIDTH // LANES
RW_SHIFT_COLS = 3 * RW_WIDTH + 2 * RW_RANK
C_ZRW = RW_SHIFT_COLS
C_QKV = C_ZRW + RW_WIDTH
C_ZGD = C_QKV + 3 * GD_WIDTH
C_GATES = C_ZGD + GD_WIDTH

PROJ_TM = 512
CARRY_ROWS = 8
REC_TT = 512
CUM_ROWS = 256
MERGE_TM = 512
MERGE_COLS = 512
VMEM_LIMIT = 56 * 1024 * 1024


def _mm(a, b):
    return jnp.dot(a.astype(BF16), b.astype(BF16), preferred_element_type=F32)


def _mm_nt(a, b):
    return lax.dot_general(a.astype(BF16), b.astype(BF16), (((1,), (1,)), ((), ())),
                           preferred_element_type=F32)


def _mm_tn(a, b):
    return lax.dot_general(a.astype(BF16), b.astype(BF16), (((0,), (0,)), ((), ())),
                           preferred_element_type=F32)


def _sigmoid(x):
    return 0.5 * jnp.tanh(0.5 * x) + 0.5


def _silu(x):
    h = 0.5 * x
    return h * jnp.tanh(h) + h


def _softplus(x):
    return jnp.maximum(x, 0.0) + jnp.log(1.0 + jnp.exp(-jnp.abs(x)))


def _split3(x):
    hi = x.astype(BF16)
    r1 = x - hi.astype(F32)
    mid = r1.astype(BF16)
    lo = (r1 - mid.astype(F32)).astype(BF16)
    return hi, mid, lo


def _blk(idx, size):
    return lax.shift_right_logical(idx, size.bit_length() - 1)


def _tri_masks(n, chunk):
    ri = lax.broadcasted_iota(jnp.int32, (n, n), 0)
    ci = lax.broadcasted_iota(jnp.int32, (n, n), 1)
    same = _blk(ri, chunk) == _blk(ci, chunk)
    return ri, ci, same


def _unit_lower_inverses(lmats, ri, ci, chunk):
    eye = (ri == ci).astype(F32)
    blk8 = _blk(ri, 8) == _blk(ci, 8)
    n = lmats[0].shape[0]
    ds = [jnp.where(blk8, l, 0.0) for l in lmats]
    dbs = [d.astype(BF16) for d in ds]
    d2s = [_mm(d, d).astype(BF16) for d in dbs]
    ts = [eye + d for d in ds]
    xs = [_mm(d2, jnp.concatenate([t.astype(BF16), d2], axis=1)) for t, d2 in zip(ts, d2s)]
    ts = [t + x[:, :n] for t, x in zip(ts, xs)]
    ts = [t + _mm(x[:, n:], t) for t, x in zip(ts, xs)]
    s = 8
    while s < chunk:
        groups = range(n // (2 * s))
        lower = lambda x: jnp.concatenate(
            [x[g * 2 * s + s:(g + 1) * 2 * s] for g in groups], axis=0)
        zeros = jnp.zeros((s, n), F32)
        rc = lax.broadcasted_iota(jnp.int32, (n // 2, n), 0)
        cl = lax.broadcasted_iota(jnp.int32, (n // 2, n), 1)
        rl = _blk(rc, s) * (2 * s) + s + (rc & (s - 1))
        off = (_blk(rl, 2 * s) == _blk(cl, 2 * s)) & (_blk(rl, s) != _blk(cl, s))
        tbs = [t.astype(BF16) for t in ts]
        xs = [_mm(jnp.where(off, lower(l), 0.0), t) for l, t in zip(lmats, tbs)]
        xs = [jnp.concatenate([p for g in groups for p in (zeros, x[g * s:(g + 1) * s])], axis=0)
              for x in xs]
        ys = [_mm(lower(t), x) for t, x in zip(ts, xs)]
        ts = [jnp.concatenate([p for g in groups
                               for p in (t[g * 2 * s:g * 2 * s + s],
                                         t[g * 2 * s + s:(g + 1) * 2 * s] + y[g * s:(g + 1) * s])],
                              axis=0)
              for t, y in zip(ts, ys)]
        s *= 2
    return [t.astype(BF16) for t in ts]


def _chunk_cumsum(ltri, x):
    return jnp.concatenate([_mm(ltri, x[r0:r0 + CUM_ROWS]) for r0 in range(0, x.shape[0], CUM_ROWS)],
                           axis=0)


def _proj_kernel(x_ref, g_ref, w_ref, mu_ref, prw_ref, zrw_ref, pqkv_ref, zgd_ref, ba_ref,
                 carry_scr, *, tiles_per_seq):
    tm = x_ref.shape[0]

    @pl.when(pl.program_id(0) % tiles_per_seq == 0)
    def _():
        carry_scr[...] = jnp.zeros_like(carry_scr)

    xc = x_ref[...]
    inv = lax.rsqrt(jnp.mean(xc * xc, axis=-1, keepdims=True) + NORM_EPS)
    xb = (xc * g_ref[...]).astype(BF16)

    def proj(w):
        return jnp.dot(xb, w, preferred_element_type=F32) * inv

    def token_shift(g, cur):
        prev = pltpu.roll(jnp.concatenate([carry_scr[g], cur], axis=0), 1, 0)[CARRY_ROWS:]
        carry_scr[g] = cur[tm - CARRY_ROWS:]
        return cur + (prev - cur) * mu_ref[g]

    def lane_groups(c0, n_groups, out_ref, post=None):
        for g0 in range(0, n_groups, 4):
            ng = min(4, n_groups - g0)
            pe = proj(w_ref[:, c0 + g0 * LANES:c0 + (g0 + ng) * LANES])
            for j in range(ng):
                slab = pe[:, j * LANES:(j + 1) * LANES]
                out_ref[g0 + j] = (slab if post is None else post(g0 + j, slab)).astype(BF16)

    lane_groups(0, RW_SHIFT_COLS // LANES, prw_ref, token_shift)
    lane_groups(C_ZRW, N_PAIRS, zrw_ref)
    lane_groups(C_QKV, 3 * GD_HEADS, pqkv_ref)
    lane_groups(C_ZGD, GD_HEADS, zgd_ref)
    ba_ref[...] = proj(w_ref[:, C_GATES:C_GATES + LANES])


def _projection(x2, norm_w, w_bf, mu, seq_len):
    bt = x2.shape[0]
    tm = PROJ_TM
    const = lambda i: (0, 0)
    slabs = lambda n: pl.BlockSpec((n, tm, LANES), lambda i: (0, i, 0))
    n_rw = RW_SHIFT_COLS // LANES
    out_shapes = (
        jax.ShapeDtypeStruct((n_rw, bt, LANES), BF16),
        jax.ShapeDtypeStruct((N_PAIRS, bt, LANES), BF16),
        jax.ShapeDtypeStruct((3 * GD_HEADS, bt, LANES), BF16),
        jax.ShapeDtypeStruct((GD_HEADS, bt, LANES), BF16),
        jax.ShapeDtypeStruct((bt, LANES), F32),
    )
    out_specs = (
        slabs(n_rw), slabs(N_PAIRS), slabs(3 * GD_HEADS), slabs(GD_HEADS),
        pl.BlockSpec((tm, LANES), lambda i: (i, 0)),
    )
    in_specs = [
        pl.BlockSpec((tm, D_MODEL), lambda i: (i, 0)),
        pl.BlockSpec((1, D_MODEL), const),
        pl.BlockSpec((D_MODEL, C_GATES + LANES), const, pipeline_mode=pl.Buffered(1)),
        pl.BlockSpec((n_rw, 1, LANES), lambda i: (0, 0, 0)),
    ]
    return pl.pallas_call(
        functools.partial(_proj_kernel, tiles_per_seq=seq_len // tm),
        grid=(bt // tm,),
        in_specs=in_specs,
        out_specs=out_specs,
        out_shape=out_shapes,
        scratch_shapes=[pltpu.VMEM((n_rw, CARRY_ROWS, LANES), F32)],
        compiler_params=pltpu.CompilerParams(
            dimension_semantics=("arbitrary",), vmem_limit_bytes=VMEM_LIMIT),
        name="rmsnorm_in_proj",
    )(x2, norm_w, w_bf, mu)


def _rwkv_kernel(xs_ref, z_ref, ltri_ref, wda_ref, w0_ref, a0_ref,
                 kk_ref, ka_ref, rk_ref, gnw_ref, gnb_ref, y_ref, s_scr):
    t_idx = pl.program_id(1)

    tt = z_ref.shape[1]
    n_chunks = tt // RW_CHUNK

    @pl.when(t_idx == 0)
    def _():
        s_scr[...] = jnp.zeros_like(s_scr)

    lane = lax.broadcasted_iota(jnp.int32, (1, LANES), 1)
    head_a = lane < RW_HEAD
    ri, ci, same = _tri_masks(LANES, RW_CHUNK)
    strict = same & (ci < ri)
    causal = same & (ci <= ri)

    def head_sum(x):
        sa = jnp.sum(jnp.where(head_a, x, 0.0), axis=-1, keepdims=True)
        sb = jnp.sum(jnp.where(head_a, 0.0, x), axis=-1, keepdims=True)
        return jnp.where(head_a, jnp.broadcast_to(sa, x.shape), jnp.broadcast_to(sb, x.shape))
    ltri = ltri_ref[...]

    def pair_rows(x):
        return jnp.concatenate([jnp.where(head_a, x, 0.0), jnp.where(head_a, 0.0, x)], axis=0)

    wa = xs_ref[3 * N_PAIRS].astype(F32)
    da = jnp.where(lane < RW_RANK, jnp.tanh(wa), wa)
    pre = _mm(da, wda_ref[...])

    bonus, p_end, work = [], [], []
    for j in range(N_PAIRS):
        r = xs_ref[j].astype(F32)
        k = xs_ref[N_PAIRS + j].astype(F32)
        v = xs_ref[2 * N_PAIRS + j].astype(F32)
        cols = slice(j * LANES, (j + 1) * LANES)
        lw = DECAY_SCALE * _sigmoid(w0_ref[j] + pre[:, cols])
        lr = _sigmoid(a0_ref[j] + pre[:, RW_WIDTH + j * LANES:RW_WIDTH + (j + 1) * LANES])
        kkr = k * kk_ref[j]
        kk = kkr * lax.rsqrt(head_sum(kkr * kkr) + L2_EPS)
        k2 = k * (1.0 + (lr - 1.0) * ka_ref[j])
        b = kk * lr
        bonus.append(head_sum(r * k2 * rk_ref[j]) * v)
        lw_hi = lw.astype(BF16)
        lw_lo = (lw - lw_hi.astype(F32)).astype(BF16)
        cl2 = _chunk_cumsum(ltri, jnp.concatenate([lw_hi, lw_lo], axis=1))
        clw = cl2[:, :LANES] + cl2[:, LANES:]
        e_neg = jnp.exp(-clw)
        rt = r * jnp.exp(clw)
        at = -kk * jnp.exp(clw - lw)
        bt = b * e_neg
        kt = k2 * e_neg
        for c in range(n_chunks):
            sl = slice(c * RW_CHUNK, (c + 1) * RW_CHUNK)
            p_c = jnp.exp(clw[(c + 1) * RW_CHUNK - 1:(c + 1) * RW_CHUNK, :])
            p_end.append(jnp.sum(jnp.where(ri == ci, p_c, 0.0), axis=-1, keepdims=True))
            work.append(dict(
                ar=jnp.concatenate([pair_rows(at[sl]), pair_rows(rt[sl])], axis=0).astype(BF16),
                bk=jnp.concatenate([bt[sl], bt[sl], kt[sl], kt[sl]], axis=0).astype(BF16),
                v=pair_rows(v[sl]).astype(BF16),
                bk_end=jnp.concatenate([pair_rows(bt[sl] * p_c), pair_rows(kt[sl] * p_c)],
                                       axis=0).astype(BF16)))

    n = len(work)
    causal2 = jnp.concatenate([causal, causal], axis=1)
    zeros = jnp.zeros((LANES, LANES), BF16)
    ms = [_mm_nt(w["ar"], w["bk"]) for w in work]
    lbs = [jnp.where(strict, m[:LANES, :LANES], 0.0) for m in ms]
    aks = [jnp.where(strict, m[:LANES, LANES:], 0.0).astype(BF16) for m in ms]
    rbks = [jnp.where(causal2, m[LANES:], 0.0).astype(BF16) for m in ms]
    t_invs = _unit_lower_inverses(lbs, ri, ci, RW_CHUNK)
    akvs = [_mm(ak, w["v"]).astype(BF16) for ak, w in zip(aks, work)]
    tws = [_mm(t, jnp.concatenate([w["ar"][:LANES], akv], axis=1)).astype(BF16)
           for t, w, akv in zip(t_invs, work, akvs)]
    xs = [jnp.concatenate([tw, jnp.concatenate([zeros, w["v"]], axis=1)], axis=0)
          for tw, w in zip(tws, work)]
    qys = [_mm(rbk, x) for rbk, x in zip(rbks, xs)]
    zs = [_mm_tn(w["bk_end"], x) for w, x in zip(work, xs)]
    zas = [z[:, :LANES].astype(BF16) for z in zs]

    inv_n = 1.0 / RW_HEAD
    state = [s_scr[j] for j in range(N_PAIRS)]
    for c in range(n_chunks):
        sl = slice(c * RW_CHUNK, (c + 1) * RW_CHUNK)
        for j in range(N_PAIRS):
            i = j * n_chunks + c
            h0 = state[j]
            state[j] = p_end[i] * h0 + _mm(zas[i], h0) + zs[i][:, LANES:]
            qc = work[i]["ar"][LANES:] + qys[i][:, :LANES]
            y_p = _mm(qc, h0) + qys[i][:, LANES:]
            y = y_p[:RW_CHUNK] + y_p[RW_CHUNK:]
            yc = y - head_sum(y) * inv_n
            var = head_sum(yc * yc) * inv_n
            yn = yc * lax.rsqrt(var + RW_GN_EPS) * gnw_ref[j] + gnb_ref[j]
            gate = _silu(z_ref[j, sl, :].astype(F32))
            y_ref[j, sl, :] = ((yn + bonus[j][sl]) * gate).astype(y_ref.dtype)
    for j in range(N_PAIRS):
        s_scr[j] = state[j]


def _rwkv_branch(xs, zrw, ltri, wda, w0, a0, k_k, k_a, r_k, gn_w, gn_b, batch, seq_len):
    tt = REC_TT
    nt = seq_len // tt
    bt = batch * seq_len
    tok = lambda n: pl.BlockSpec((n, tt, LANES), lambda b, t: (0, b * nt + t, 0))
    par = pl.BlockSpec((N_PAIRS, 1, LANES), lambda b, t: (0, 0, 0))
    n_rw = RW_SHIFT_COLS // LANES
    in_specs = [
        tok(n_rw),
        tok(N_PAIRS),
        pl.BlockSpec((CUM_ROWS, CUM_ROWS), lambda b, t: (0, 0)),
        pl.BlockSpec((LANES, 2 * RW_WIDTH), lambda b, t: (0, 0)),
        par, par, par, par, par, par, par,
    ]
    return pl.pallas_call(
        _rwkv_kernel,
        grid=(batch, nt),
        in_specs=in_specs,
        out_specs=tok(N_PAIRS),
        out_shape=jax.ShapeDtypeStruct((N_PAIRS, bt, LANES), BF16),
        scratch_shapes=[pltpu.VMEM((N_PAIRS, LANES, LANES), F32)],
        compiler_params=pltpu.CompilerParams(
            dimension_semantics=("arbitrary", "arbitrary"), vmem_limit_bytes=VMEM_LIMIT),
        name="rwkv7_chunked",
    )(xs, zrw, ltri, wda, w0, a0, k_k, k_a, r_k, gn_w, gn_b)


def _gdn_kernel(pqkv_ref, z_ref, ba_ref, ltri_ref, cw_ref, alog_ref, dtb_ref, onw_ref,
                y_ref, s_scr, carry_scr):
    t_idx = pl.program_id(1)

    tt = ba_ref.shape[0]

    @pl.when(t_idx == 0)
    def _():
        s_scr[...] = jnp.zeros_like(s_scr)
        carry_scr[:, :CARRY_ROWS, :] = jnp.zeros((3 * GD_HEADS, CARRY_ROWS, LANES), F32)

    for g in range(3 * GD_HEADS):
        carry_scr[g, CARRY_ROWS:, :] = pqkv_ref[g].astype(F32)

    def conv_silu(g):
        cur = carry_scr[g, CARRY_ROWS:, :]
        taps = cw_ref[g]
        acc = cur * taps[GD_CONV - 1:GD_CONV]
        for s in range(1, GD_CONV):
            acc = acc + carry_scr[g, pl.ds(CARRY_ROWS - s, tt), :] * taps[GD_CONV - 1 - s:GD_CONV - s]
        carry_scr[g, :CARRY_ROWS, :] = cur[tt - CARRY_ROWS:]
        return _silu(acc)

    n_super = tt // LANES
    per_super = LANES // GD_CHUNK
    ri, ci, same = _tri_masks(LANES, GD_CHUNK)
    strict = same & (ci < ri)
    causal = same & (ci <= ri)
    ba = ba_ref[...]
    g_all = -jnp.exp(alog_ref[...]) * _softplus(ba + dtb_ref[...])
    gc3 = _chunk_cumsum(ltri_ref[...], jnp.concatenate(_split3(g_all), axis=1))
    gc_all = gc3[:, :LANES] + gc3[:, LANES:2 * LANES] + gc3[:, 2 * LANES:]
    gc_all_t = gc_all.T

    work = []
    for h in range(GD_HEADS):
        q = conv_silu(h)
        k = conv_silu(GD_HEADS + h)
        v = conv_silu(2 * GD_HEADS + h)
        qn = q * (lax.rsqrt(jnp.sum(q * q, axis=-1, keepdims=True) + L2_EPS) * (GD_HEAD ** -0.5))
        kn = k * lax.rsqrt(jnp.sum(k * k, axis=-1, keepdims=True) + L2_EPS)
        beta = jnp.broadcast_to(_sigmoid(ba[:, h:h + 1]), (tt, LANES))
        gc_col = jnp.broadcast_to(gc_all[:, GD_HEADS + h:GD_HEADS + h + 1], (tt, LANES))
        gc_row = gc_all_t[GD_HEADS + h:GD_HEADS + h + 1, :]
        e_col = jnp.exp(gc_col)
        kb = kn * beta
        vb = v * beta
        qd = qn * e_col
        kbe = kb * e_col
        for sc in range(n_super):
            sl = slice(sc * LANES, (sc + 1) * LANES)
            diff = gc_col[sl] - gc_row[:, sl]
            chunks = []
            for cc in range(per_super):
                rows = slice(sc * LANES + cc * GD_CHUNK, sc * LANES + (cc + 1) * GD_CHUNK)
                g_last = gc_col[rows.stop - 1:rows.stop, :]
                chunks.append(dict(kte=(kn[rows] * jnp.exp(g_last - gc_col[rows])).astype(BF16),
                                   decay=jnp.exp(g_last)))
            work.append(dict(dec=jnp.exp(jnp.where(causal, diff, -1e30)),
                             kq=jnp.concatenate([kb[sl], qn[sl]], axis=0).astype(BF16),
                             kn=kn[sl].astype(BF16),
                             rhs=jnp.concatenate([vb[sl], kbe[sl]], axis=1).astype(BF16),
                             qd=qd[sl], chunks=chunks))

    ms = [_mm_nt(w["kq"], w["kn"]) for w in work]
    lmats = [-jnp.where(strict, m[:LANES] * w["dec"], 0.0) for m, w in zip(ms, work)]
    qks = [(m[LANES:] * w["dec"]).astype(BF16) for m, w in zip(ms, work)]
    t_invs = _unit_lower_inverses(lmats, ri, ci, GD_CHUNK)
    uws = [_mm(t, w["rhs"]).astype(BF16) for t, w in zip(t_invs, work)]
    qws = [_mm(qk, uw) for qk, uw in zip(qks, uws)]
    for w, uw, qw in zip(work, uws, qws):
        w["qc"] = (w["qd"] - qw[:, LANES:]).astype(BF16)
        w["o0"] = qw[:, :LANES]
        for cc, ch in enumerate(w["chunks"]):
            tn = _mm_tn(ch["kte"], uw[cc * GD_CHUNK:(cc + 1) * GD_CHUNK])
            ch["tn_u"] = tn[:, :LANES]
            ch["tn_w"] = tn[:, LANES:].astype(BF16)

    state = [s_scr[h] for h in range(GD_HEADS)]
    os = [[] for _ in range(GD_HEADS)]
    for sc in range(n_super):
        for cc in range(per_super):
            rows = slice(cc * GD_CHUNK, (cc + 1) * GD_CHUNK)
            for h in range(GD_HEADS):
                w = work[h * n_super + sc]
                ch = w["chunks"][cc]
                s0 = state[h]
                state[h] = s0 * ch["decay"] - _mm(ch["tn_w"], s0) + ch["tn_u"]
                os[h].append(_mm(w["qc"][rows], s0) + w["o0"][rows])
    for h in range(GD_HEADS):
        s_scr[h] = state[h]

    for h in range(GD_HEADS):
        o = jnp.concatenate(os[h], axis=0)
        ms = jnp.mean(o * o, axis=-1, keepdims=True)
        on = o * lax.rsqrt(ms + NORM_EPS) * onw_ref[...]
        y_ref[h] = (on * _silu(z_ref[h].astype(F32))).astype(y_ref.dtype)


def _gdn_branch(pqkv, zgd, ba, ltri, conv_w, alog, dtb, onw, batch, seq_len):
    tt = REC_TT
    nt = seq_len // tt
    bt = batch * seq_len
    tok = lambda n: pl.BlockSpec((n, tt, LANES), lambda b, t: (0, b * nt + t, 0))
    par = pl.BlockSpec((1, LANES), lambda b, t: (0, 0))
    in_specs = [
        tok(3 * GD_HEADS),
        tok(GD_HEADS),
        pl.BlockSpec((tt, LANES), lambda b, t: (b * nt + t, 0)),
        pl.BlockSpec((CUM_ROWS, CUM_ROWS), lambda b, t: (0, 0)),
        pl.BlockSpec((3 * GD_HEADS, GD_CONV, LANES), lambda b, t: (0, 0, 0)),
        par, par,
        pl.BlockSpec((1, LANES), lambda b, t: (0, 0)),
    ]
    return pl.pallas_call(
        _gdn_kernel,
        grid=(batch, nt),
        in_specs=in_specs,
        out_specs=tok(GD_HEADS),
        out_shape=jax.ShapeDtypeStruct((GD_HEADS, bt, LANES), BF16),
        scratch_shapes=[pltpu.VMEM((GD_HEADS, LANES, LANES), F32),
                        pltpu.VMEM((3 * GD_HEADS, CARRY_ROWS + tt, LANES), F32)],
        compiler_params=pltpu.CompilerParams(
            dimension_semantics=("arbitrary", "arbitrary"), vmem_limit_bytes=VMEM_LIMIT),
        name="gated_delta_chunked",
    )(pqkv, zgd, ba, ltri, conv_w, alog, dtb, onw)


def _merge_kernel(x_ref, ya_ref, yb_ref, gin_ref, wg_ref, wa_ref, wb_ref, wo_ref, g_ref, o_ref):
    xc = x_ref[...]
    inv = lax.rsqrt(jnp.mean(xc * xc, axis=-1, keepdims=True) + NORM_EPS)
    xb = (xc * gin_ref[...]).astype(BF16)
    ya = jnp.concatenate([ya_ref[j] for j in range(N_PAIRS)], axis=1)
    yb = jnp.concatenate([yb_ref[j] for j in range(GD_HEADS)], axis=1)
    blocks = []
    for c0 in range(0, D_MODEL, MERGE_COLS):
        cols = slice(c0, c0 + MERGE_COLS)
        gate_a = _sigmoid(jnp.dot(xb, wg_ref[:, cols], preferred_element_type=F32) * inv)
        gate_b = _sigmoid(jnp.dot(xb, wg_ref[:, D_MODEL + c0:D_MODEL + c0 + MERGE_COLS],
                                  preferred_element_type=F32) * inv)
        ma = jnp.dot(ya, wa_ref[:, cols], preferred_element_type=F32)
        mb = jnp.dot(yb, wb_ref[:, cols], preferred_element_type=F32)
        blocks.append((gate_a * ma + gate_b * mb).astype(BF16))
    merged = jnp.concatenate(blocks, axis=1)
    res = xc + jnp.dot(merged, wo_ref[...], preferred_element_type=F32)
    ms = jnp.mean(res * res, axis=-1, keepdims=True)
    o_ref[...] = res * lax.rsqrt(ms + NORM_EPS) * g_ref[...]


def _merge(x2, ya, yb, norm_in, w_gates, w_a, w_b, w_o, norm_w):
    bt = x2.shape[0]
    tm = MERGE_TM
    const = lambda i: (0, 0)
    in_specs = [
        pl.BlockSpec((tm, D_MODEL), lambda i: (i, 0)),
        pl.BlockSpec((N_PAIRS, tm, LANES), lambda i: (0, i, 0)),
        pl.BlockSpec((GD_HEADS, tm, LANES), lambda i: (0, i, 0)),
        pl.BlockSpec((1, D_MODEL), const),
        pl.BlockSpec((D_MODEL, 2 * D_MODEL), const),
        pl.BlockSpec((RW_WIDTH, D_MODEL), const),
        pl.BlockSpec((GD_WIDTH, D_MODEL), const),
        pl.BlockSpec((D_MODEL, D_MODEL), const),
        pl.BlockSpec((1, D_MODEL), const),
    ]
    return pl.pallas_call(
        _merge_kernel,
        grid=(bt // tm,),
        in_specs=in_specs,
        out_specs=pl.BlockSpec((tm, D_MODEL), lambda i: (i, 0)),
        out_shape=jax.ShapeDtypeStruct((bt, D_MODEL), F32),
        compiler_params=pltpu.CompilerParams(
            dimension_semantics=("arbitrary",), vmem_limit_bytes=VMEM_LIMIT),
        name="merge_out_proj",
    )(x2, ya, yb, norm_in, w_gates, w_a, w_b, w_o, norm_w)


def _block_lower_tri(n, chunk):
    i = jnp.arange(n)
    return (((i[:, None] // chunk) == (i[None, :] // chunk)) & (i[None, :] <= i[:, None])).astype(BF16)


def _layer(x2, batch, seq_len, norm_in_w, w_in, rw_mu, rw_w0, rw_w2, rw_a0, rw_a2, rw_k_k,
           rw_k_a, rw_r_k, rw_gn_w, rw_gn_b, gd_conv_w, gd_A_log, gd_dt_bias, gd_o_norm_w,
           w_branch_a, w_branch_b, w_out):
    w_bf = w_in.astype(BF16)
    w_gates = w_bf[:, C_GATES + 2 * GD_HEADS:]

    mu = rw_mu.reshape(RW_SHIFT_COLS // LANES, 1, LANES)
    xs, zrw, pqkv, zgd, ba = _projection(x2, norm_in_w.reshape(1, D_MODEL), w_bf, mu, seq_len)

    ltri = _block_lower_tri(CUM_ROWS, RW_CHUNK)
    pair = lambda p: p.reshape(N_PAIRS, 1, LANES)
    zero = jnp.zeros((RW_RANK, RW_WIDTH), F32)
    wda = jnp.concatenate([jnp.concatenate([rw_w2, zero], axis=1),
                           jnp.concatenate([zero, rw_a2], axis=1)], axis=0).astype(BF16)
    ya = _rwkv_branch(xs, zrw, ltri, wda, pair(rw_w0), pair(rw_a0), pair(rw_k_k),
                      pair(rw_k_a), pair(rw_r_k.reshape(RW_WIDTH)), pair(rw_gn_w),
                      pair(rw_gn_b), batch, seq_len)

    head = lambda p: jnp.pad(p.astype(F32), (GD_HEADS, LANES - 2 * GD_HEADS)).reshape(1, LANES)
    conv_w = jnp.swapaxes(gd_conv_w.reshape(GD_CONV, 3 * GD_HEADS, LANES), 0, 1)
    yb = _gdn_branch(pqkv, zgd, ba, _block_lower_tri(CUM_ROWS, GD_CHUNK), conv_w,
                     head(gd_A_log), head(gd_dt_bias),
                     gd_o_norm_w.reshape(1, GD_HEAD), batch, seq_len)
    return w_gates, ya, yb


def kernel(x, norm_in_w, w_in, rw_mu, rw_w0, rw_w2, rw_a0, rw_a2, rw_k_k, rw_k_a, rw_r_k,
           rw_gn_w, rw_gn_b, gd_conv_w, gd_A_log, gd_dt_bias, gd_o_norm_w,
           w_branch_a, w_branch_b, w_out, norm_out_w):
    batch, seq_len, d = x.shape
    assert d == D_MODEL and norm_in_w.shape[0] == 1
    assert seq_len % max(PROJ_TM, REC_TT) == 0 and (batch * seq_len) % MERGE_TM == 0
    x2 = x.reshape(batch * seq_len, d)
    w_gates, ya, yb = _layer(
        x2, batch, seq_len, norm_in_w[0], w_in[0], rw_mu[0], rw_w0[0], rw_w2[0], rw_a0[0],
        rw_a2[0], rw_k_k[0], rw_k_a[0], rw_r_k[0], rw_gn_w[0], rw_gn_b[0], gd_conv_w[0],
        gd_A_log[0], gd_dt_bias[0], gd_o_norm_w[0], w_branch_a[0], w_branch_b[0], w_out[0])
    out = _merge(x2, ya, yb, norm_in_w.reshape(1, D_MODEL), w_gates,
                 w_branch_a[0].astype(BF16), w_branch_b[0].astype(BF16), w_out[0].astype(BF16),
                 norm_out_w.reshape(1, D_MODEL))
    return out.reshape(batch, seq_len, d)
```

```python
import functools

import jax
import jax.numpy as jnp
from jax import lax
from jax.experimental import pallas as pl
from jax.experimental.pallas import tpu as pltpu

F32 = jnp.float32
BF16 = jnp.bfloat16

D_MODEL = 1024
RW_WIDTH = 512
RW_HEAD = 64
RW_RANK = 64
RW_GN_EPS = RW_HEAD * 1e-5
DECAY_SCALE = -0.6065306597126334
GD_WIDTH = 512
GD_HEAD = 128
GD_HEADS = GD_WIDTH // GD_HEAD
GD_CONV = 4
NORM_EPS = 1e-6
L2_EPS = 1e-12

LANES = 128
RW_CHUNK = 64
GD_CHUNK = 128
N_PAIRS = RW_WIDTH // LANES
RW_SHIFT_COLS = 3 * RW_WIDTH + 2 * RW_RANK
C_ZRW = RW_SHIFT_COLS
C_QKV = C_ZRW + RW_WIDTH
C_ZGD = C_QKV + 3 * GD_WIDTH
C_GATES = C_ZGD + GD_WIDTH

PROJ_TM = 512
CARRY_ROWS = 8
REC_TT = 512
CUM_ROWS = 256
MERGE_TM = 512
MERGE_COLS = 512
VMEM_LIMIT = 56 * 1024 * 1024


def _mm(a, b):
    return jnp.dot(a.astype(BF16), b.astype(BF16), preferred_element_type=F32)


def _mm_nt(a, b):
    return lax.dot_general(a.astype(BF16), b.astype(BF16), (((1,), (1,)), ((), ())),
                           preferred_element_type=F32)


def _mm_tn(a, b):
    return lax.dot_general(a.astype(BF16), b.astype(BF16), (((0,), (0,)), ((), ())),
                           preferred_element_type=F32)


def _sigmoid(x):
    return 0.5 * jnp.tanh(0.5 * x) + 0.5


def _silu(x):
    h = 0.5 * x
    return h * jnp.tanh(h) + h


def _softplus(x):
    return jnp.maximum(x, 0.0) + jnp.log(1.0 + jnp.exp(-jnp.abs(x)))


def _split3(x):
    hi = x.astype(BF16)
    r1 = x - hi.astype(F32)
    mid = r1.astype(BF16)
    lo = (r1 - mid.astype(F32)).astype(BF16)
    return hi, mid, lo


def _blk(idx, size):
    return lax.shift_right_logical(idx, size.bit_length() - 1)


def _tri_masks(n, chunk):
    ri = lax.broadcasted_iota(jnp.int32, (n, n), 0)
    ci = lax.broadcasted_iota(jnp.int32, (n, n), 1)
    same = _blk(ri, chunk) == _blk(ci, chunk)
    return ri, ci, same


def _unit_lower_inverses(lmats, ri, ci, chunk):
    eye = (ri == ci).astype(F32)
    blk8 = _blk(ri, 8) == _blk(ci, 8)
    n = lmats[0].shape[0]
    ds = [jnp.where(blk8, l, 0.0) for l in lmats]
    dbs = [d.astype(BF16) for d in ds]
    d2s = [_mm(d, d).astype(BF16) for d in dbs]
    ts = [eye + d for d in ds]
    xs = [_mm(d2, jnp.concatenate([t.astype(BF16), d2], axis=1)) for t, d2 in zip(ts, d2s)]
    ts = [t + x[:, :n] for t, x in zip(ts, xs)]
    ts = [t + _mm(x[:, n:], t) for t, x in zip(ts, xs)]
    s = 8
    while s < chunk:
        groups = range(n // (2 * s))
        lower = lambda x: jnp.concatenate(
            [x[g * 2 * s + s:(g + 1) * 2 * s] for g in groups], axis=0)
        zeros = jnp.zeros((s, n), F32)
        rc = lax.broadcasted_iota(jnp.int32, (n // 2, n), 0)
        cl = lax.broadcasted_iota(jnp.int32, (n // 2, n), 1)
        rl = _blk(rc, s) * (2 * s) + s + (rc & (s - 1))
        off = (_blk(rl, 2 * s) == _blk(cl, 2 * s)) & (_blk(rl, s) != _blk(cl, s))
        tbs = [t.astype(BF16) for t in ts]
        xs = [_mm(jnp.where(off, lower(l), 0.0), t) for l, t in zip(lmats, tbs)]
        xs = [jnp.concatenate([p for g in groups for p in (zeros, x[g * s:(g + 1) * s])], axis=0)
              for x in xs]
        ys = [_mm(lower(t), x) for t, x in zip(ts, xs)]
        ts = [jnp.concatenate([p for g in groups
                               for p in (t[g * 2 * s:g * 2 * s + s],
                                         t[g * 2 * s + s:(g + 1) * 2 * s] + y[g * s:(g + 1) * s])],
                              axis=0)
              for t, y in zip(ts, ys)]
        s *= 2
    return [t.astype(BF16) for t in ts]


def _chunk_cumsum(ltri, x):
    return jnp.concatenate([_mm(ltri, x[r0:r0 + CUM_ROWS]) for r0 in range(0, x.shape[0], CUM_ROWS)],
                           axis=0)


def _proj_kernel(x_ref, g_ref, w_ref, mu_ref, prw_ref, zrw_ref, pqkv_ref, zgd_ref, ba_ref,
                 carry_scr, *, tiles_per_seq):
    tm = x_ref.shape[0]

    @pl.when(pl.program_id(0) % tiles_per_seq == 0)
    def _():
        carry_scr[...] = jnp.zeros_like(carry_scr)

    xc = x_ref[...]
    inv = lax.rsqrt(jnp.mean(xc * xc, axis=-1, keepdims=True) + NORM_EPS)
    xb = (xc * g_ref[...]).astype(BF16)

    def proj(w):
        return jnp.dot(xb, w, preferred_element_type=F32) * inv

    def token_shift(g, cur):
        prev = pltpu.roll(jnp.concatenate([carry_scr[g], cur], axis=0), 1, 0)[CARRY_ROWS:]
        carry_scr[g] = cur[tm - CARRY_ROWS:]
        return cur + (prev - cur) * mu_ref[g]

    def lane_groups(c0, n_groups, out_ref, post=None):
        for g0 in range(0, n_groups, 4):
            ng = min(4, n_groups - g0)
            pe = proj(w_ref[:, c0 + g0 * LANES:c0 + (g0 + ng) * LANES])
            for j in range(ng):
                slab = pe[:, j * LANES:(j + 1) * LANES]
                out_ref[g0 + j] = (slab if post is None else post(g0 + j, slab)).astype(BF16)

    lane_groups(0, RW_SHIFT_COLS // LANES, prw_ref, token_shift)
    lane_groups(C_ZRW, N_PAIRS, zrw_ref)
    lane_groups(C_QKV, 3 * GD_HEADS, pqkv_ref)
    lane_groups(C_ZGD, GD_HEADS, zgd_ref)
    ba_ref[...] = proj(w_ref[:, C_GATES:C_GATES + LANES])


def _projection(x2, norm_w, w_bf, mu, seq_len):
    bt = x2.shape[0]
    tm = PROJ_TM
    const = lambda i: (0, 0)
    slabs = lambda n: pl.BlockSpec((n, tm, LANES), lambda i: (0, i, 0))
    n_rw = RW_SHIFT_COLS // LANES
    out_shapes = (
        jax.ShapeDtypeStruct((n_rw, bt, LANES), BF16),
        jax.ShapeDtypeStruct((N_PAIRS, bt, LANES), BF16),
        jax.ShapeDtypeStruct((3 * GD_HEADS, bt, LANES), BF16),
        jax.ShapeDtypeStruct((GD_HEADS, bt, LANES), BF16),
        jax.ShapeDtypeStruct((bt, LANES), F32),
    )
    out_specs = (
        slabs(n_rw), slabs(N_PAIRS), slabs(3 * GD_HEADS), slabs(GD_HEADS),
        pl.BlockSpec((tm, LANES), lambda i: (i, 0)),
    )
    in_specs = [
        pl.BlockSpec((tm, D_MODEL), lambda i: (i, 0)),
        pl.BlockSpec((1, D_MODEL), const),
        pl.BlockSpec((D_MODEL, C_GATES + LANES), const, pipeline_mode=pl.Buffered(1)),
        pl.BlockSpec((n_rw, 1, LANES), lambda i: (0, 0, 0)),
    ]
    return pl.pallas_call(
        functools.partial(_proj_kernel, tiles_per_seq=seq_len // tm),
        grid=(bt // tm,),
        in_specs=in_specs,
        out_specs=out_specs,
        out_shape=out_shapes,
        scratch_shapes=[pltpu.VMEM((n_rw, CARRY_ROWS, LANES), F32)],
        compiler_params=pltpu.CompilerParams(
            dimension_semantics=("arbitrary",), vmem_limit_bytes=VMEM_LIMIT),
        name="rmsnorm_in_proj",
    )(x2, norm_w, w_bf, mu)


def _rwkv_kernel(xs_ref, z_ref, ltri_ref, wda_ref, w0_ref, a0_ref,
                 kk_ref, ka_ref, rk_ref, gnw_ref, gnb_ref, y_ref, s_scr):
    t_idx = pl.program_id(1)

    tt = z_ref.shape[1]
    n_chunks = tt // RW_CHUNK

    @pl.when(t_idx == 0)
    def _():
        s_scr[...] = jnp.zeros_like(s_scr)

    lane = lax.broadcasted_iota(jnp.int32, (1, LANES), 1)
    head_a = lane < RW_HEAD
    ri, ci, same = _tri_masks(LANES, RW_CHUNK)
    strict = same & (ci < ri)
    causal = same & (ci <= ri)

    def head_sum(x):
        sa = jnp.sum(jnp.where(head_a, x, 0.0), axis=-1, keepdims=True)
        sb = jnp.sum(jnp.where(head_a, 0.0, x), axis=-1, keepdims=True)
        return jnp.where(head_a, jnp.broadcast_to(sa, x.shape), jnp.broadcast_to(sb, x.shape))
    ltri = ltri_ref[...]

    def pair_rows(x):
        return jnp.concatenate([jnp.where(head_a, x, 0.0), jnp.where(head_a, 0.0, x)], axis=0)

    wa = xs_ref[3 * N_PAIRS].astype(F32)
    da = jnp.where(lane < RW_RANK, jnp.tanh(wa), wa)
    pre = _mm(da, wda_ref[...])

    bonus, p_end, work = [], [], []
    for j in range(N_PAIRS):
        r = xs_ref[j].astype(F32)
        k = xs_ref[N_PAIRS + j].astype(F32)
        v = xs_ref[2 * N_PAIRS + j].astype(F32)
        cols = slice(j * LANES, (j + 1) * LANES)
        lw = DECAY_SCALE * _sigmoid(w0_ref[j] + pre[:, cols])
        lr = _sigmoid(a0_ref[j] + pre[:, RW_WIDTH + j * LANES:RW_WIDTH + (j + 1) * LANES])
        kkr = k * kk_ref[j]
        kk = kkr * lax.rsqrt(head_sum(kkr * kkr) + L2_EPS)
        k2 = k * (1.0 + (lr - 1.0) * ka_ref[j])
        b = kk * lr
        bonus.append(head_sum(r * k2 * rk_ref[j]) * v)
        lw_hi = lw.astype(BF16)
        lw_lo = (lw - lw_hi.astype(F32)).astype(BF16)
        cl2 = _chunk_cumsum(ltri, jnp.concatenate([lw_hi, lw_lo], axis=1))
        clw = cl2[:, :LANES] + cl2[:, LANES:]
        e_neg = jnp.exp(-clw)
        rt = r * jnp.exp(clw)
        at = -kk * jnp.exp(clw - lw)
        bt = b * e_neg
        kt = k2 * e_neg
        for c in range(n_chunks):
            sl = slice(c * RW_CHUNK, (c + 1) * RW_CHUNK)
            p_c = jnp.exp(clw[(c + 1) * RW_CHUNK - 1:(c + 1) * RW_CHUNK, :])
            p_end.append(jnp.sum(jnp.where(ri == ci, p_c, 0.0), axis=-1, keepdims=True))
            work.append(dict(
                ar=jnp.concatenate([pair_rows(at[sl]), pair_rows(rt[sl])], axis=0).astype(BF16),
                bk=jnp.concatenate([bt[sl], bt[sl], kt[sl], kt[sl]], axis=0).astype(BF16),
                v=pair_rows(v[sl]).astype(BF16),
                bk_end=jnp.concatenate([pair_rows(bt[sl] * p_c), pair_rows(kt[sl] * p_c)],
                                       axis=0).astype(BF16)))

    n = len(work)
    causal2 = jnp.concatenate([causal, causal], axis=1)
    zeros = jnp.zeros((LANES, LANES), BF16)
    ms = [_mm_nt(w["ar"], w["bk"]) for w in work]
    lbs = [jnp.where(strict, m[:LANES, :LANES], 0.0) for m in ms]
    aks = [jnp.where(strict, m[:LANES, LANES:], 0.0).astype(BF16) for m in ms]
    rbks = [jnp.where(causal2, m[LANES:], 0.0).astype(BF16) for m in ms]
    t_invs = _unit_lower_inverses(lbs, ri, ci, RW_CHUNK)
    akvs = [_mm(ak, w["v"]).astype(BF16) for ak, w in zip(aks, work)]
    tws = [_mm(t, jnp.concatenate([w["ar"][:LANES], akv], axis=1)).astype(BF16)
           for t, w, akv in zip(t_invs, work, akvs)]
    xs = [jnp.concatenate([tw, jnp.concatenate([zeros, w["v"]], axis=1)], axis=0)
          for tw, w in zip(tws, work)]
    qys = [_mm(rbk, x) for rbk, x in zip(rbks, xs)]
    zs = [_mm_tn(w["bk_end"], x) for w, x in zip(work, xs)]
    zas = [z[:, :LANES].astype(BF16) for z in zs]

    inv_n = 1.0 / RW_HEAD
    state = [s_scr[j] for j in range(N_PAIRS)]
    for c in range(n_chunks):
        sl = slice(c * RW_CHUNK, (c + 1) * RW_CHUNK)
        for j in range(N_PAIRS):
            i = j * n_chunks + c
            h0 = state[j]
            state[j] = p_end[i] * h0 + _mm(zas[i], h0) + zs[i][:, LANES:]
            qc = work[i]["ar"][LANES:] + qys[i][:, :LANES]
            y_p = _mm(qc, h0) + qys[i][:, LANES:]
            y = y_p[:RW_CHUNK] + y_p[RW_CHUNK:]
            yc = y - head_sum(y) * inv_n
            var = head_sum(yc * yc) * inv_n
            yn = yc * lax.rsqrt(var + RW_GN_EPS) * gnw_ref[j] + gnb_ref[j]
            gate = _silu(z_ref[j, sl, :].astype(F32))
            y_ref[j, sl, :] = ((yn + bonus[j][sl]) * gate).astype(y_ref.dtype)
    for j in range(N_PAIRS):
        s_scr[j] = state[j]


def _rwkv_branch(xs, zrw, ltri, wda, w0, a0, k_k, k_a, r_k, gn_w, gn_b, batch, seq_len):
    tt = REC_TT
    nt = seq_len // tt
    bt = batch * seq_len
    tok = lambda n: pl.BlockSpec((n, tt, LANES), lambda b, t: (0, b * nt + t, 0))
    par = pl.BlockSpec((N_PAIRS, 1, LANES), lambda b, t: (0, 0, 0))
    n_rw = RW_SHIFT_COLS // LANES
    in_specs = [
        tok(n_rw),
        tok(N_PAIRS),
        pl.BlockSpec((CUM_ROWS, CUM_ROWS), lambda b, t: (0, 0)),
        pl.BlockSpec((LANES, 2 * RW_WIDTH), lambda b, t: (0, 0)),
        par, par, par, par, par, par, par,
    ]
    return dict(
        args=(xs, zrw, ltri, wda, w0, a0, k_k, k_a, r_k, gn_w, gn_b),
        in_specs=in_specs,
        out_spec=tok(N_PAIRS),
        out_shape=jax.ShapeDtypeStruct((N_PAIRS, bt, LANES), BF16),
        scratch=[pltpu.VMEM((N_PAIRS, LANES, LANES), F32)])


def _gdn_kernel(pqkv_ref, z_ref, ba_ref, ltri_ref, cw_ref, alog_ref, dtb_ref, onw_ref,
                y_ref, s_scr, carry_scr):
    t_idx = pl.program_id(1)

    tt = ba_ref.shape[0]

    @pl.when(t_idx == 0)
    def _():
        s_scr[...] = jnp.zeros_like(s_scr)
        carry_scr[:, :CARRY_ROWS, :] = jnp.zeros((3 * GD_HEADS, CARRY_ROWS, LANES), F32)

    for g in range(3 * GD_HEADS):
        carry_scr[g, CARRY_ROWS:, :] = pqkv_ref[g].astype(F32)

    def conv_silu(g):
        cur = carry_scr[g, CARRY_ROWS:, :]
        taps = cw_ref[g]
        acc = cur * taps[GD_CONV - 1:GD_CONV]
        for s in range(1, GD_CONV):
            acc = acc + carry_scr[g, pl.ds(CARRY_ROWS - s, tt), :] * taps[GD_CONV - 1 - s:GD_CONV - s]
        carry_scr[g, :CARRY_ROWS, :] = cur[tt - CARRY_ROWS:]
        return _silu(acc)

    n_super = tt // LANES
    per_super = LANES // GD_CHUNK
    ri, ci, same = _tri_masks(LANES, GD_CHUNK)
    strict = same & (ci < ri)
    causal = same & (ci <= ri)
    ba = ba_ref[...]
    g_all = -jnp.exp(alog_ref[...]) * _softplus(ba + dtb_ref[...])
    gc3 = _chunk_cumsum(ltri_ref[...], jnp.concatenate(_split3(g_all), axis=1))
    gc_all = gc3[:, :LANES] + gc3[:, LANES:2 * LANES] + gc3[:, 2 * LANES:]
    gc_all_t = gc_all.T

    work = []
    for h in range(GD_HEADS):
        q = conv_silu(h)
        k = conv_silu(GD_HEADS + h)
        v = conv_silu(2 * GD_HEADS + h)
        qn = q * (lax.rsqrt(jnp.sum(q * q, axis=-1, keepdims=True) + L2_EPS) * (GD_HEAD ** -0.5))
        kn = k * lax.rsqrt(jnp.sum(k * k, axis=-1, keepdims=True) + L2_EPS)
        beta = jnp.broadcast_to(_sigmoid(ba[:, h:h + 1]), (tt, LANES))
        gc_col = jnp.broadcast_to(gc_all[:, GD_HEADS + h:GD_HEADS + h + 1], (tt, LANES))
        gc_row = gc_all_t[GD_HEADS + h:GD_HEADS + h + 1, :]
        e_col = jnp.exp(gc_col)
        kb = kn * beta
        vb = v * beta
        qd = qn * e_col
        kbe = kb * e_col
        for sc in range(n_super):
            sl = slice(sc * LANES, (sc + 1) * LANES)
            diff = gc_col[sl] - gc_row[:, sl]
            chunks = []
            for cc in range(per_super):
                rows = slice(sc * LANES + cc * GD_CHUNK, sc * LANES + (cc + 1) * GD_CHUNK)
                g_last = gc_col[rows.stop - 1:rows.stop, :]
                chunks.append(dict(kte=(kn[rows] * jnp.exp(g_last - gc_col[rows])).astype(BF16),
                                   decay=jnp.exp(g_last)))
            work.append(dict(dec=jnp.exp(jnp.where(causal, diff, -1e30)),
                             kq=jnp.concatenate([kb[sl], qn[sl]], axis=0).astype(BF16),
                             kn=kn[sl].astype(BF16),
                             rhs=jnp.concatenate([vb[sl], kbe[sl]], axis=1).astype(BF16),
                             qd=qd[sl], chunks=chunks))

    ms = [_mm_nt(w["kq"], w["kn"]) for w in work]
    lmats = [-jnp.where(strict, m[:LANES] * w["dec"], 0.0) for m, w in zip(ms, work)]
    qks = [(m[LANES:] * w["dec"]).astype(BF16) for m, w in zip(ms, work)]
    t_invs = _unit_lower_inverses(lmats, ri, ci, GD_CHUNK)
    uws = [_mm(t, w["rhs"]).astype(BF16) for t, w in zip(t_invs, work)]
    qws = [_mm(qk, uw) for qk, uw in zip(qks, uws)]
    for w, uw, qw in zip(work, uws, qws):
        w["qc"] = (w["qd"] - qw[:, LANES:]).astype(BF16)
        w["o0"] = qw[:, :LANES]
        for cc, ch in enumerate(w["chunks"]):
            tn = _mm_tn(ch["kte"], uw[cc * GD_CHUNK:(cc + 1) * GD_CHUNK])
            ch["tn_u"] = tn[:, :LANES]
            ch["tn_w"] = tn[:, LANES:].astype(BF16)

    state = [s_scr[h] for h in range(GD_HEADS)]
    os = [[] for _ in range(GD_HEADS)]
    for sc in range(n_super):
        for cc in range(per_super):
            rows = slice(cc * GD_CHUNK, (cc + 1) * GD_CHUNK)
            for h in range(GD_HEADS):
                w = work[h * n_super + sc]
                ch = w["chunks"][cc]
                s0 = state[h]
                state[h] = s0 * ch["decay"] - _mm(ch["tn_w"], s0) + ch["tn_u"]
                os[h].append(_mm(w["qc"][rows], s0) + w["o0"][rows])
    for h in range(GD_HEADS):
        s_scr[h] = state[h]

    for h in range(GD_HEADS):
        o = jnp.concatenate(os[h], axis=0)
        ms = jnp.mean(o * o, axis=-1, keepdims=True)
        on = o * lax.rsqrt(ms + NORM_EPS) * onw_ref[...]
        y_ref[h] = (on * _silu(z_ref[h].astype(F32))).astype(y_ref.dtype)


def _gdn_branch(pqkv, zgd, ba, ltri, conv_w, alog, dtb, onw, batch, seq_len):
    tt = REC_TT
    nt = seq_len // tt
    bt = batch * seq_len
    tok = lambda n: pl.BlockSpec((n, tt, LANES), lambda b, t: (0, b * nt + t, 0))
    par = pl.BlockSpec((1, LANES), lambda b, t: (0, 0))
    in_specs = [
        tok(3 * GD_HEADS),
        tok(GD_HEADS),
        pl.BlockSpec((tt, LANES), lambda b, t: (b * nt + t, 0)),
        pl.BlockSpec((CUM_ROWS, CUM_ROWS), lambda b, t: (0, 0)),
        pl.BlockSpec((3 * GD_HEADS, GD_CONV, LANES), lambda b, t: (0, 0, 0)),
        par, par,
        pl.BlockSpec((1, LANES), lambda b, t: (0, 0)),
    ]
    return dict(
        args=(pqkv, zgd, ba, ltri, conv_w, alog, dtb, onw),
        in_specs=in_specs,
        out_spec=tok(GD_HEADS),
        out_shape=jax.ShapeDtypeStruct((GD_HEADS, bt, LANES), BF16),
        scratch=[pltpu.VMEM((GD_HEADS, LANES, LANES), F32),
                 pltpu.VMEM((3 * GD_HEADS, CARRY_ROWS + tt, LANES), F32)])


def _branches_kernel(*refs, n_rw_in, n_gd_in):
    n_in = n_rw_in + n_gd_in
    ya_ref, yb_ref, rw_state, gd_state, gd_window = refs[n_in:]
    _rwkv_kernel(*refs[:n_rw_in], ya_ref, rw_state)
    _gdn_kernel(*refs[n_rw_in:n_in], yb_ref, gd_state, gd_window)


def _branches(rw, gd, batch, seq_len):
    kern = functools.partial(_branches_kernel, n_rw_in=len(rw["args"]), n_gd_in=len(gd["args"]))
    return pl.pallas_call(
        kern,
        grid=(batch, seq_len // REC_TT),
        in_specs=rw["in_specs"] + gd["in_specs"],
        out_specs=(rw["out_spec"], gd["out_spec"]),
        out_shape=(rw["out_shape"], gd["out_shape"]),
        scratch_shapes=rw["scratch"] + gd["scratch"],
        compiler_params=pltpu.CompilerParams(
            dimension_semantics=("arbitrary", "arbitrary"), vmem_limit_bytes=VMEM_LIMIT),
        name="rwkv7_and_gated_delta_chunked",
    )(*rw["args"], *gd["args"])


def _merge_kernel(x_ref, ya_ref, yb_ref, gin_ref, wg_ref, wa_ref, wb_ref, wo_ref, g_ref, o_ref):
    xc = x_ref[...]
    inv = lax.rsqrt(jnp.mean(xc * xc, axis=-1, keepdims=True) + NORM_EPS)
    xb = (xc * gin_ref[...]).astype(BF16)
    ya = jnp.concatenate([ya_ref[j] for j in range(N_PAIRS)], axis=1)
    yb = jnp.concatenate([yb_ref[j] for j in range(GD_HEADS)], axis=1)
    blocks = []
    for c0 in range(0, D_MODEL, MERGE_COLS):
        cols = slice(c0, c0 + MERGE_COLS)
        gate_a = _sigmoid(jnp.dot(xb, wg_ref[:, cols], preferred_element_type=F32) * inv)
        gate_b = _sigmoid(jnp.dot(xb, wg_ref[:, D_MODEL + c0:D_MODEL + c0 + MERGE_COLS],
                                  preferred_element_type=F32) * inv)
        ma = jnp.dot(ya, wa_ref[:, cols], preferred_element_type=F32)
        mb = jnp.dot(yb, wb_ref[:, cols], preferred_element_type=F32)
        blocks.append((gate_a * ma + gate_b * mb).astype(BF16))
    merged = jnp.concatenate(blocks, axis=1)
    res = xc + jnp.dot(merged, wo_ref[...], preferred_element_type=F32)
    ms = jnp.mean(res * res, axis=-1, keepdims=True)
    o_ref[...] = res * lax.rsqrt(ms + NORM_EPS) * g_ref[...]


def _merge(x2, ya, yb, norm_in, w_gates, w_a, w_b, w_o, norm_w):
    bt = x2.shape[0]
    tm = MERGE_TM
    const = lambda i: (0, 0)
    in_specs = [
        pl.BlockSpec((tm, D_MODEL), lambda i: (i, 0)),
        pl.BlockSpec((N_PAIRS, tm, LANES), lambda i: (0, i, 0)),
        pl.BlockSpec((GD_HEADS, tm, LANES), lambda i: (0, i, 0)),
        pl.BlockSpec((1, D_MODEL), const),
        pl.BlockSpec((D_MODEL, 2 * D_MODEL), const),
        pl.BlockSpec((RW_WIDTH, D_MODEL), const),
        pl.BlockSpec((GD_WIDTH, D_MODEL), const),
        pl.BlockSpec((D_MODEL, D_MODEL), const),
        pl.BlockSpec((1, D_MODEL), const),
    ]
    return pl.pallas_call(
        _merge_kernel,
        grid=(bt // tm,),
        in_specs=in_specs,
        out_specs=pl.BlockSpec((tm, D_MODEL), lambda i: (i, 0)),
        out_shape=jax.ShapeDtypeStruct((bt, D_MODEL), F32),
        compiler_params=pltpu.CompilerParams(
            dimension_semantics=("arbitrary",), vmem_limit_bytes=VMEM_LIMIT),
        name="merge_out_proj",
    )(x2, ya, yb, norm_in, w_gates, w_a, w_b, w_o, norm_w)


def _block_lower_tri(n, chunk):
    i = jnp.arange(n)
    return (((i[:, None] // chunk) == (i[None, :] // chunk)) & (i[None, :] <= i[:, None])).astype(BF16)


def _layer(x2, batch, seq_len, norm_in_w, w_in, rw_mu, rw_w0, rw_w2, rw_a0, rw_a2, rw_k_k,
           rw_k_a, rw_r_k, rw_gn_w, rw_gn_b, gd_conv_w, gd_A_log, gd_dt_bias, gd_o_norm_w,
           w_branch_a, w_branch_b, w_out):
    w_bf = w_in.astype(BF16)
    w_gates = w_bf[:, C_GATES + 2 * GD_HEADS:]

    mu = rw_mu.reshape(RW_SHIFT_COLS // LANES, 1, LANES)
    xs, zrw, pqkv, zgd, ba = _projection(x2, norm_in_w.reshape(1, D_MODEL), w_bf, mu, seq_len)

    ltri = _block_lower_tri(CUM_ROWS, RW_CHUNK)
    pair = lambda p: p.reshape(N_PAIRS, 1, LANES)
    zero = jnp.zeros((RW_RANK, RW_WIDTH), F32)
    wda = jnp.concatenate([jnp.concatenate([rw_w2, zero], axis=1),
                           jnp.concatenate([zero, rw_a2], axis=1)], axis=0).astype(BF16)
    rw = _rwkv_branch(xs, zrw, ltri, wda, pair(rw_w0), pair(rw_a0), pair(rw_k_k),
                      pair(rw_k_a), pair(rw_r_k.reshape(RW_WIDTH)), pair(rw_gn_w),
                      pair(rw_gn_b), batch, seq_len)

    head = lambda p: jnp.pad(p.astype(F32), (GD_HEADS, LANES - 2 * GD_HEADS)).reshape(1, LANES)
    conv_w = jnp.swapaxes(gd_conv_w.reshape(GD_CONV, 3 * GD_HEADS, LANES), 0, 1)
    gd = _gdn_branch(pqkv, zgd, ba, _block_lower_tri(CUM_ROWS, GD_CHUNK), conv_w,
                     head(gd_A_log), head(gd_dt_bias),
                     gd_o_norm_w.reshape(1, GD_HEAD), batch, seq_len)
    ya, yb = _branches(rw, gd, batch, seq_len)
    return w_gates, ya, yb


def kernel(x, norm_in_w, w_in, rw_mu, rw_w0, rw_w2, rw_a0, rw_a2, rw_k_k, rw_k_a, rw_r_k,
           rw_gn_w, rw_gn_b, gd_conv_w, gd_A_log, gd_dt_bias, gd_o_norm_w,
           w_branch_a, w_branch_b, w_out, norm_out_w):
    batch, seq_len, d = x.shape
    assert d == D_MODEL and norm_in_w.shape[0] == 1
    assert seq_len % max(PROJ_TM, REC_TT) == 0 and (batch * seq_len) % MERGE_TM == 0
    x2 = x.reshape(batch * seq_len, d)
    w_gates, ya, yb = _layer(
        x2, batch, seq_len, norm_in_w[0], w_in[0], rw_mu[0], rw_w0[0], rw_w2[0], rw_a0[0],
        rw_a2[0], rw_k_k[0], rw_k_a[0], rw_r_k[0], rw_gn_w[0], rw_gn_b[0], gd_conv_w[0],
        gd_A_log[0], gd_dt_bias[0], gd_o_norm_w[0], w_branch_a[0], w_branch_b[0], w_out[0])
    out = _merge(x2, ya, yb, norm_in_w.reshape(1, D_MODEL), w_gates,
                 w_branch_a[0].astype(BF16), w_branch_b[0].astype(BF16), w_out[0].astype(BF16),
                 norm_out_w.reshape(1, D_MODEL))
    return out.reshape(batch, seq_len, d)
```
